```python
import functools
import jax
import jax.numpy as jnp
from jax import lax
import numpy as np

D_MODEL = 2048
BATCH = 1
SEQ = 8192
DEPTH = 2
DEC_BATCH = 32
DEC_SEQ = 1
PAST_LEN = 8192
PAGE_SIZE = 128

RET_HEADS = 4
RET_DK = 128
RET_DV = 256
RET_CHUNK = 128
ROPE_BASE = 10000.0
ATT_HEADS = 4
ATT_HD = 128
IDX_HEADS = 16
IDX_DIM = 64
TOPK_MAX = 256
Q_BLOCK = 128
GM_GROUPS = 4
GM_GW = 128
GM_CHUNK = 128
N_MEM = 256
MEM_HEADS = 4
MEM_HD = 128
D_FF = 5504
CONV_W = 3
EPS = 1e-6

RET_QK = RET_HEADS * RET_DK
RET_V = RET_HEADS * RET_DV
ATT_W = ATT_HEADS * ATT_HD
IDX_Q = IDX_HEADS * IDX_DIM
GM_W = GM_GROUPS * GM_GW
MEM_W = MEM_HEADS * MEM_HD
D_MIX = RET_V + ATT_W + GM_W
SPLITS = (RET_QK, RET_QK, RET_V, RET_V, ATT_W, ATT_W, ATT_W, IDX_Q, IDX_HEADS, IDX_DIM, GM_W, GM_W)
D_IN = sum(SPLITS)

kernel_name = 'hybrid_retention_dsa_gmlp_step'


def rmsnorm(x, g):
    xf = x.astype(jnp.float32)
    y = xf * lax.rsqrt(jnp.mean(xf * xf, axis=-1, keepdims=True) + EPS)
    return (y * g.astype(jnp.float32)).astype(x.dtype)


def split_cols(p):
    offs, acc = [], 0
    for w in SPLITS[:-1]:
        acc += w
        offs.append(acc)
    return jnp.split(p, offs, axis=-1)


def rotary(x, pos):
    half = x.shape[-1] // 2
    inv = ROPE_BASE ** (-jnp.arange(half, dtype=jnp.float32) / half)
    ang = pos.astype(jnp.float32)[:, None] * inv[None, :]
    cos = jnp.cos(ang)[None, :, None, :]
    sin = jnp.sin(ang)[None, :, None, :]
    xf = x.astype(jnp.float32)
    x1, x2 = xf[..., :half], xf[..., half:]
    return jnp.concatenate([x1 * cos - x2 * sin, x1 * sin + x2 * cos], axis=-1)


def retention_chunk(q, k, v, state, log_g):
    c = q.shape[1]
    n = jnp.arange(c, dtype=jnp.float32)
    diff = n[:, None] - n[None, :]
    causal = diff >= 0
    dmat = jnp.where(causal[None], jnp.exp(log_g[:, None, None] * jnp.where(causal, diff, 0.0)[None]), 0.0)
    inner = jnp.einsum('bhqk,bkhe->bqhe', jnp.einsum('bqhd,bkhd->bhqk', q, k) * dmat[None], v)
    cross = jnp.einsum('bqhd,bhde->bqhe', q, state) * jnp.exp(log_g[None, :] * (n[:, None] + 1.0))[None, :, :, None]
    k_dec = k * jnp.exp(log_g[None, :] * (c - 1.0 - n[:, None]))[None, :, :, None]
    new_state = jnp.exp(log_g * c)[None, :, None, None] * state + jnp.einsum('bkhd,bkhe->bhde', k_dec, v)
    return inner + cross, new_state


def retention(q, k, v, pos, state):
    B, L = q.shape[:2]
    q = rotary(q, pos)
    k = rotary(k, pos) * RET_DK ** -0.5
    v = v.astype(jnp.float32)
    log_g = jnp.log(1.0 - 2.0 ** (-5.0 - jnp.arange(RET_HEADS, dtype=jnp.float32)))
    c = min(RET_CHUNK, L)
    nc = L // c

    def chunks(a):
        return jnp.moveaxis(a.reshape((B, nc, c) + a.shape[2:]), 1, 0)

    def step(s, qkv):
        o, s = retention_chunk(qkv[0], qkv[1], qkv[2], s, log_g)
        return s, o

    s_fin, o = lax.scan(step, state.astype(jnp.float32), (chunks(q), chunks(k), chunks(v)))
    return jnp.moveaxis(o, 0, 1).reshape(B, L, RET_HEADS, RET_DV), s_fin


def indexer_scores(qi, wi, ki):
    s = jnp.einsum('bqhd,bsd->bqhs', qi.astype(jnp.float32), ki.astype(jnp.float32)) * IDX_DIM ** -0.5
    return jnp.einsum('bqh,bqhs->bqs', wi.astype(jnp.float32) * IDX_HEADS ** -0.5, jax.nn.relu(s))


def take_rows(a, idx):
    return jax.vmap(lambda ab, ib: ab[ib])(a, idx)


def sparse_attend(q, kg, vg, valid):
    s = jnp.einsum('bqhd,bqkhd->bhqk', q.astype(jnp.float32), kg.astype(jnp.float32)) * ATT_HD ** -0.5
    p = jax.nn.softmax(jnp.where(valid[:, None], s, -jnp.inf), axis=-1)
    return jnp.einsum('bhqk,bqkhd->bqhd', p, vg.astype(jnp.float32)).astype(q.dtype)


def dsa_prompt(q, k, v, qi, wi, ki):
    B, S = q.shape[:2]
    topk = min(TOPK_MAX, S // 4)
    qb = min(Q_BLOCK, S)
    nb = S // qb
    key_pos = jnp.arange(S)

    def block(i):
        s0 = i * qb
        sl = lambda a: lax.dynamic_slice_in_dim(a, s0, qb, axis=1)
        pos_q = s0 + jnp.arange(qb)
        vis = key_pos[None, None, :] <= pos_q[None, :, None]
        scores = jnp.where(vis, indexer_scores(sl(qi), sl(wi), ki), -jnp.inf)
        _, idx = lax.top_k(scores, topk)
        valid = idx <= pos_q[None, :, None]
        return sparse_attend(sl(q), take_rows(k, idx), take_rows(v, idx), valid)

    o = lax.map(block, jnp.arange(nb))
    return jnp.moveaxis(o, 0, 1).reshape(B, S, ATT_HEADS, ATT_HD)


def dsa_sample(q, k, v, qi, wi, ki, ck, cv, cki, page_table):
    Bd, T = q.shape[:2]
    P = page_table.shape[1] * PAGE_SIZE
    L = P + T
    topk = min(TOPK_MAX, L // 4)
    ki_all = jnp.concatenate([cki[page_table].reshape(Bd, P, IDX_DIM), ki.astype(cki.dtype)], axis=1)
    pos_q = P + jnp.arange(T)
    vis = jnp.arange(L)[None, None, :] <= pos_q[None, :, None]
    scores = jnp.where(vis, indexer_scores(qi, wi, ki_all), -jnp.inf)
    _, idx = lax.top_k(scores, topk)
    past = idx < P
    pidx = jnp.minimum(idx, P - 1)
    phys = jax.vmap(lambda pt, i: pt[i // PAGE_SIZE])(page_table, pidx)
    off = pidx % PAGE_SIZE
    nidx = jnp.clip(idx - P, 0, T - 1)
    kg = jnp.where(past[..., None, None], ck[phys, off].astype(k.dtype), take_rows(k, nidx))
    vg = jnp.where(past[..., None, None], cv[phys, off].astype(v.dtype), take_rows(v, nidx))
    valid = idx <= pos_q[None, :, None]
    return sparse_attend(q, kg, vg, valid)


def chunk_gmlp(u, v, g, ws, bs):
    B, L, _ = u.shape
    v = rmsnorm(v, g)
    c = min(GM_CHUNK, L)
    nc = L // c
    causal = jnp.tril(jnp.ones((c, c), dtype=bool))
    w = jnp.where(causal[None], ws[:, :c, :c], 0.0).astype(v.dtype)
    vg = v.reshape(B, nc, c, GM_GROUPS, GM_GW)
    mixed = jnp.einsum('gts,bnsgc->bntgc', w, vg) + bs[:, :c].T[None, None, :, :, None]
    return u * mixed.reshape(B, L, GM_W), v


def token_mixers(h, pos, ret_state, attend, w_in, ret_g, gm_g, gm_ws, gm_bs, w_out):
    B, L, _ = h.shape
    rq, rk, rv, rg, aq, ak, av, iq, iw, ik, gu, gv = split_cols(h @ w_in)
    r_o, r_state = retention(rq.reshape(B, L, RET_HEADS, RET_DK), rk.reshape(B, L, RET_HEADS, RET_DK),
                             rv.reshape(B, L, RET_HEADS, RET_DV), pos, ret_state)
    a_out = (rmsnorm(r_o.astype(h.dtype), ret_g) * jax.nn.silu(rg.reshape(B, L, RET_HEADS, RET_DV))).reshape(B, L, RET_V)
    k = ak.reshape(B, L, ATT_HEADS, ATT_HD)
    v = av.reshape(B, L, ATT_HEADS, ATT_HD)
    b_out = attend(aq.reshape(B, L, ATT_HEADS, ATT_HD), k, v, iq.reshape(B, L, IDX_HEADS, IDX_DIM), iw, ik).reshape(B, L, ATT_W)
    c_out, gm_v = chunk_gmlp(jax.nn.gelu(gu), jax.nn.gelu(gv), gm_g, gm_ws, gm_bs)
    y = jnp.concatenate([a_out, b_out, c_out.astype(h.dtype)], axis=-1) @ w_out
    return y, k, v, ik, r_state.astype(ret_state.dtype), gm_v


def mem_kv(mem, g, w_k, w_v):
    B, M, _ = mem.shape
    m = rmsnorm(mem, g)
    return (m @ w_k).reshape(B, M, MEM_HEADS, MEM_HD), (m @ w_v).reshape(B, M, MEM_HEADS, MEM_HD)


def mem_attend(h, w_q, mk, mv, w_o):
    B, L, _ = h.shape
    q = (h @ w_q).reshape(B, L, MEM_HEADS, MEM_HD).astype(jnp.float32)
    s = jnp.einsum('blhd,bmhd->bhlm', q, mk.astype(jnp.float32)) * MEM_HD ** -0.5
    o = jnp.einsum('bhlm,bmhd->blhd', jax.nn.softmax(s, axis=-1), mv.astype(jnp.float32))
    return o.reshape(B, L, MEM_W).astype(h.dtype) @ w_o


def conv_ffn(h, prev, w_up, conv_w, conv_b, w_down):
    L = h.shape[1]
    a = h @ w_up
    ext = jnp.concatenate([prev.astype(a.dtype), a], axis=1)
    c = conv_b
    for j in range(CONV_W):
        c = c + conv_w[j] * ext[:, j:j + L]
    gate, up = jnp.split(c, 2, axis=-1)
    return (jax.nn.silu(gate) * up) @ w_down, ext[:, L:]


def setup_inputs(seed: int = 0) -> dict:
    key = jax.random.key(seed)
    ks = list(jax.random.split(key, 40))

    def nrm(shape, scale=1.0):
        return jax.random.normal(ks.pop(), shape, jnp.float32) * scale

    def gain(shape):
        return 1.0 + 0.05 * nrm(shape)

    n_pages = PAST_LEN // PAGE_SIZE
    used = DEC_BATCH * n_pages
    n_pool = used + max(1, used // 4)
    perm = jax.random.permutation(ks.pop(), n_pool)
    page_table = perm[:used].reshape(DEC_BATCH, n_pages).astype(jnp.int32)
    return {
        'x_prompt': nrm((BATCH, SEQ, D_MODEL)),
        'x_sample': nrm((DEC_BATCH, DEC_SEQ, D_MODEL)),
        'mem_prompt': nrm((BATCH, N_MEM, D_MODEL)),
        'cache_k': nrm((DEPTH, n_pool, PAGE_SIZE, ATT_HEADS, ATT_HD)),
        'cache_v': nrm((DEPTH, n_pool, PAGE_SIZE, ATT_HEADS, ATT_HD)),
        'cache_kidx': nrm((DEPTH, n_pool, PAGE_SIZE, IDX_DIM)),
        'page_table': page_table,
        'cache_mem_k': nrm((DEPTH, DEC_BATCH, N_MEM, MEM_HEADS, MEM_HD)),
        'cache_mem_v': nrm((DEPTH, DEC_BATCH, N_MEM, MEM_HEADS, MEM_HD)),
        'state_ret': nrm((DEPTH, DEC_BATCH, RET_HEADS, RET_DK, RET_DV)),
        'state_conv': nrm((DEPTH, DEC_BATCH, CONV_W - 1, 2 * D_FF)),
        'norm_mix_g': gain((DEPTH, D_MODEL)),
        'w_in': nrm((DEPTH, D_MODEL, D_IN), D_MODEL ** -0.5),
        'ret_norm_g': gain((DEPTH, RET_HEADS, RET_DV)),
        'gm_norm_g': gain((DEPTH, GM_W)),
        'gm_ws': nrm((DEPTH, GM_GROUPS, GM_CHUNK, GM_CHUNK), GM_CHUNK ** -0.5),
        'gm_bs': gain((DEPTH, GM_GROUPS, GM_CHUNK)),
        'w_out': nrm((DEPTH, D_MIX, D_MODEL), D_MIX ** -0.5),
        'norm_mem_g': gain((DEPTH, D_MODEL)),
        'mem_in_g': gain((DEPTH, D_MODEL)),
        'w_mq': nrm((DEPTH, D_MODEL, MEM_W), D_MODEL ** -0.5),
        'w_mk': nrm((DEPTH, D_MODEL, MEM_W), D_MODEL ** -0.5),
        'w_mv': nrm((DEPTH, D_MODEL, MEM_W), D_MODEL ** -0.5),
        'w_mo': nrm((DEPTH, MEM_W, D_MODEL), MEM_W ** -0.5),
        'norm_ffn_g': gain((DEPTH, D_MODEL)),
        'w_up': nrm((DEPTH, D_MODEL, 2 * D_FF), D_MODEL ** -0.5),
        'conv_w': nrm((DEPTH, CONV_W, 2 * D_FF), 0.5),
        'conv_b': nrm((DEPTH, 2 * D_FF), 0.01),
        'w_down': nrm((DEPTH, D_FF, D_MODEL), D_FF ** -0.5),
        'final_norm_g': gain((D_MODEL,)),
    }


def reference(x_prompt, x_sample, mem_prompt, cache_k, cache_v, cache_kidx, page_table, cache_mem_k, cache_mem_v,
              state_ret, state_conv, norm_mix_g, w_in, ret_norm_g, gm_norm_g, gm_ws, gm_bs, w_out, norm_mem_g,
              mem_in_g, w_mq, w_mk, w_mv, w_mo, norm_ffn_g, w_up, conv_w, conv_b, w_down, final_norm_g):
    B, S, _ = x_prompt.shape
    Bd, T, _ = x_sample.shape
    past = page_table.shape[1] * PAGE_SIZE
    pos_p = jnp.arange(S)
    pos_s = past + jnp.arange(T)
    ret0 = jnp.zeros((B, RET_HEADS, RET_DK, RET_DV), x_prompt.dtype)
    conv0 = jnp.zeros((B, CONV_W - 1, 2 * D_FF), x_prompt.dtype)

    xp, xs = x_prompt, x_sample
    kp, vp, kip, ks_, vs_, kis = [], [], [], [], [], []
    mkp, mvp, rsp, rss, csp, css, gvp, gvs = [], [], [], [], [], [], [], []
    for l in range(DEPTH):
        y, k, v, ki, rs, gv = token_mixers(rmsnorm(xp, norm_mix_g[l]), pos_p, ret0, dsa_prompt,
                                           w_in[l], ret_norm_g[l], gm_norm_g[l], gm_ws[l], gm_bs[l], w_out[l])
        xp = xp + y
        mk, mv = mem_kv(mem_prompt, mem_in_g[l], w_mk[l], w_mv[l])
        xp = xp + mem_attend(rmsnorm(xp, norm_mem_g[l]), w_mq[l], mk, mv, w_mo[l])
        f, cs = conv_ffn(rmsnorm(xp, norm_ffn_g[l]), conv0, w_up[l], conv_w[l], conv_b[l], w_down[l])
        xp = xp + f
        kp.append(k); vp.append(v); kip.append(ki); mkp.append(mk); mvp.append(mv)
        rsp.append(rs); csp.append(cs); gvp.append(gv[:, -min(GM_CHUNK, S):])
        attend = functools.partial(dsa_sample, ck=cache_k[l], cv=cache_v[l], cki=cache_kidx[l], page_table=page_table)
        y, k, v, ki, rs, gv = token_mixers(rmsnorm(xs, norm_mix_g[l]), pos_s, state_ret[l], attend,
                                           w_in[l], ret_norm_g[l], gm_norm_g[l], gm_ws[l], gm_bs[l], w_out[l])
        xs = xs + y
        xs = xs + mem_attend(rmsnorm(xs, norm_mem_g[l]), w_mq[l], cache_mem_k[l], cache_mem_v[l], w_mo[l])
        f, cs = conv_ffn(rmsnorm(xs, norm_ffn_g[l]), state_conv[l], w_up[l], conv_w[l], conv_b[l], w_down[l])
        xs = xs + f
        ks_.append(k); vs_.append(v); kis.append(ki)
        rss.append(rs); css.append(cs); gvs.append(gv)

    y_prompt = rmsnorm(xp, final_norm_g)
    y_sample = rmsnorm(xs, final_norm_g)
    return (y_prompt, y_sample,
            jnp.stack(kp), jnp.stack(vp), jnp.stack(kip),
            jnp.stack(ks_), jnp.stack(vs_), jnp.stack(kis),
            jnp.stack(mkp), jnp.stack(mvp),
            jnp.stack(rsp), jnp.stack(rss),
            jnp.stack(csp), jnp.stack(css),
            jnp.stack(gvp), jnp.stack(gvs))
```

```python
import functools

import jax
import jax.numpy as jnp
from jax import lax
from jax.experimental import pallas as pl
from jax.experimental.pallas import tpu as pltpu

F32 = jnp.float32
BF16 = jnp.bfloat16
I32 = jnp.int32

D_MODEL = 2048
PAGE_SIZE = 128
RET_HEADS = 4
RET_DK = 128
RET_DV = 256
CHUNK = 128
ROPE_BASE = 10000.0
ATT_HEADS = 4
ATT_HD = 128
IDX_HEADS = 16
IDX_DIM = 64
TOPK_MAX = 256
GM_GROUPS = 4
GM_GW = 128
MEM_HEADS = 4
MEM_HD = 128
D_FF = 5504
CONV_W = 3
EPS = 1e-6

RET_QK = RET_HEADS * RET_DK
RET_V = RET_HEADS * RET_DV
ATT_W = ATT_HEADS * ATT_HD
IDX_Q = IDX_HEADS * IDX_DIM
GM_W = GM_GROUPS * GM_GW
MEM_W = MEM_HEADS * MEM_HD

_O_IQ_END = 2 * RET_QK + 2 * RET_V + 3 * ATT_W + IDX_Q
_O_IW = _O_IQ_END
_O_IK = _O_IW + IDX_HEADS
_O_GU = _O_IK + IDX_DIM
_M_AQ = 2 * RET_QK + 2 * RET_V
_M_AK = _M_AQ + ATT_W
_M_AV = _M_AK + ATT_W
_M_IQ = _M_AV + ATT_W
_M_GU = _M_IQ + IDX_Q
_M_GV = _M_GU + GM_W
N_MAIN = _M_GV + GM_W
LANE = 128
D_FF_PAD = 5632
FFN_TF = 512

INT_MIN = -2 ** 31
INT_MAX = 2 ** 31 - 1
V7X_VMEM_BYTES = 64 * 1024 * 1024
NT_DIMS = (((1,), (1,)), ((), ()))


def _cparams(semantics, vmem_bytes=None):
    return pltpu.CompilerParams(dimension_semantics=semantics, vmem_limit_bytes=vmem_bytes)


def _sortable_key(x):
    b = lax.bitcast_convert_type(x, I32)
    return b ^ ((b >> 31) & INT_MAX)


def _bisect_threshold(count_ge, shape, topk):
    def step(_, lh):
        lo, hi = lh
        mid = (lo >> 1) + (hi >> 1) + (lo & hi & 1)
        ge = count_ge(mid) >= topk
        return jnp.where(ge, mid, lo), jnp.where(ge, hi, mid)

    lo, _ = lax.fori_loop(0, 32, step, (jnp.full(shape, INT_MIN, I32), jnp.full(shape, INT_MAX, I32)))
    return lo


def _norm_kernel(x_ref, g_ref, o_ref):
    x = x_ref[...]
    ms = jnp.mean(x * x, axis=-1, keepdims=True)
    o_ref[...] = (x * lax.rsqrt(ms + EPS) * g_ref[...]).astype(o_ref.dtype)


def rmsnorm_rows(x, g, out_dtype):
    m, d = x.shape
    tm = min(m, 256)
    return pl.pallas_call(
        _norm_kernel,
        grid=(m // tm,),
        in_specs=[pl.BlockSpec((tm, d), lambda i: (i, 0)), pl.BlockSpec((1, d), lambda i: (0, 0))],
        out_specs=pl.BlockSpec((tm, d), lambda i: (i, 0)),
        out_shape=jax.ShapeDtypeStruct((m, d), out_dtype),
        compiler_params=_cparams(("parallel",)),
    )(x, g.reshape(1, d))


def _mm_kernel(x_ref, w_ref, o_ref):
    o_ref[...] = jnp.dot(x_ref[...], w_ref[...], preferred_element_type=F32)


def _mm_res_kernel(x_ref, w_ref, r_ref, o_ref):
    o_ref[...] = r_ref[...] + jnp.dot(x_ref[...], w_ref[...], preferred_element_type=F32)


def matmul(x, w, res=None, tm=1024, tn=512):
    m, k = x.shape
    n = w.shape[1]
    tm = min(tm, m)
    tn = min(tn, n)
    assert m % tm == 0 and n % tn == 0
    in_specs = [pl.BlockSpec((tm, k), lambda i, j: (i, 0)), pl.BlockSpec((k, tn), lambda i, j: (0, j))]
    args = [x, w]
    body = _mm_kernel
    if res is not None:
        in_specs.append(pl.BlockSpec((tm, tn), lambda i, j: (i, j)))
        args.append(res)
        body = _mm_res_kernel
    return pl.pallas_call(
        body,
        grid=(m // tm, n // tn),
        in_specs=in_specs,
        out_specs=pl.BlockSpec((tm, tn), lambda i, j: (i, j)),
        out_shape=jax.ShapeDtypeStruct((m, n), F32),
        compiler_params=_cparams(("parallel", "parallel"), 48 * 1024 * 1024),
    )(*args)


def _mix_kernel(rq_ref, rk_ref, rv_ref, rg_ref, gu_ref, gv_ref, cos_ref, sin_ref, dmat_ref, cdec_ref, kdec_ref,
                gc_ref, retg_ref, gmg_ref, wtril_ref, bsm_ref, a_ref, c_ref, sfin_ref, gmv_ref, s_ref):
    c = pl.program_id(0)
    last = pl.num_programs(0) - 1

    @pl.when(c == 0)
    def _():
        s_ref[...] = jnp.zeros_like(s_ref)

    cosf = cos_ref[...]
    sinf = sin_ref[...]
    for h in range(RET_HEADS):
        q = rq_ref[:, h * RET_DK:(h + 1) * RET_DK]
        k = rk_ref[:, h * RET_DK:(h + 1) * RET_DK]
        qr = q * cosf + pltpu.roll(q, RET_DK // 2, axis=1) * sinf
        kr = (k * cosf + pltpu.roll(k, RET_DK // 2, axis=1) * sinf) * RET_DK ** -0.5
        vb = rv_ref[:, h * RET_DV:(h + 1) * RET_DV].astype(BF16)
        qb = qr.astype(BF16)
        state = s_ref[h]
        a = lax.dot_general(qb, kr.astype(BF16), NT_DIMS, preferred_element_type=F32) * dmat_ref[h]
        inner = jnp.dot(a.astype(BF16), vb, preferred_element_type=F32)
        cross = jnp.dot(qb, state.astype(BF16), preferred_element_type=F32) * cdec_ref[h]
        o = inner + cross
        kd_t = jnp.transpose(kr * kdec_ref[h]).astype(BF16)
        s_ref[h] = gc_ref[h] * state + jnp.dot(kd_t, vb, preferred_element_type=F32)
        ms = jnp.mean(o * o, axis=-1, keepdims=True)
        y = o * lax.rsqrt(ms + EPS) * retg_ref[:, h * RET_DV:(h + 1) * RET_DV]
        g = rg_ref[:, h * RET_DV:(h + 1) * RET_DV]
        a_ref[:, h * RET_DV:(h + 1) * RET_DV] = (y * (g * jax.nn.sigmoid(g))).astype(a_ref.dtype)

    u = jax.nn.gelu(gu_ref[...])
    v = jax.nn.gelu(gv_ref[...])
    ms = jnp.mean(v * v, axis=-1, keepdims=True)
    vn = v * lax.rsqrt(ms + EPS) * gmg_ref[...]
    vb = vn.astype(BF16)
    for g in range(GM_GROUPS):
        sl = slice(g * GM_GW, (g + 1) * GM_GW)
        mixed = jnp.dot(wtril_ref[g], vb[:, sl], preferred_element_type=F32) + bsm_ref[g]
        c_ref[:, sl] = (u[:, sl] * mixed).astype(c_ref.dtype)

    @pl.when(c == last)
    def _():
        sfin_ref[...] = s_ref[...]
        gmv_ref[...] = vn


def prompt_mixers(p_main, consts, retg_row, gmg_row, wtril, bsm):
    seq = p_main.shape[0]
    nc = seq // CHUNK
    cosf, sinf, dmat, cdec, kdec, gc = consts
    w512 = lambda blk: pl.BlockSpec((CHUNK, 512), lambda c: (c, blk))
    w1024 = lambda blk: pl.BlockSpec((CHUNK, 1024), lambda c: (c, blk))
    full = lambda shape: pl.BlockSpec(shape, lambda c: (0,) * len(shape))
    return pl.pallas_call(
        _mix_kernel,
        grid=(nc,),
        in_specs=[w512(0), w512(1), w1024(1), w1024(2), w512(_M_GU // 512), w512(_M_GV // 512),
                  pl.BlockSpec((CHUNK, RET_DK), lambda c: (c, 0)), pl.BlockSpec((CHUNK, RET_DK), lambda c: (c, 0)),
                  full(dmat.shape), full(cdec.shape), full(kdec.shape), full(gc.shape),
                  full(retg_row.shape), full(gmg_row.shape), full(wtril.shape), full(bsm.shape)],
        out_specs=[pl.BlockSpec((CHUNK, RET_V), lambda c: (c, 0)), pl.BlockSpec((CHUNK, GM_W), lambda c: (c, 0)),
                   full((RET_HEADS, RET_DK, RET_DV)), full((CHUNK, GM_W))],
        out_shape=[jax.ShapeDtypeStruct((seq, RET_V), BF16), jax.ShapeDtypeStruct((seq, GM_W), BF16),
                   jax.ShapeDtypeStruct((RET_HEADS, RET_DK, RET_DV), F32),
                   jax.ShapeDtypeStruct((CHUNK, GM_W), F32)],
        scratch_shapes=[pltpu.VMEM((RET_HEADS, RET_DK, RET_DV), F32)],
        compiler_params=_cparams(("arbitrary",)),
    )(p_main, p_main, p_main, p_main, p_main, p_main, cosf, sinf, dmat, cdec, kdec, gc, retg_row, gmg_row, wtril, bsm)


def _dsa_prompt_kernel(kix_ref, qit_ref, w_ref, k_ref, vt_ref, qt_ref, o_ref, keys_ref, *, topk, scale):
    i = pl.program_id(0)
    t_sz = CHUNK
    row = lax.broadcasted_iota(I32, (t_sz, t_sz), 0)
    col = lax.broadcasted_iota(I32, (t_sz, t_sz), 1)

    def tile_rows(t):
        return pl.ds(pl.multiple_of(t * t_sz, t_sz), t_sz)

    def tile_keys(t):
        kt = kix_ref[tile_rows(t), :]
        acc = jnp.zeros((t_sz, t_sz), F32)
        for hp in range(IDX_HEADS // 2):
            cs = slice(hp * 2 * t_sz, (hp + 1) * 2 * t_sz)
            r = jnp.dot(kt, qit_ref[0, :, cs], preferred_element_type=F32)
            r = jnp.maximum(r, 0.0) * w_ref[0, :, cs]
            acc = acc + r[:, :t_sz] + r[:, t_sz:]
        return _sortable_key(acc)

    def full_tile(t, carry):
        keys_ref[tile_rows(t), :] = tile_keys(t)
        return carry

    lax.fori_loop(0, i, full_tile, 0)
    keys_ref[tile_rows(i), :] = jnp.where(row <= col, tile_keys(i), INT_MIN)

    def count_ge(mid):
        def body(t, cnt):
            return cnt + (keys_ref[tile_rows(t), :] >= mid).astype(I32)

        cnt = lax.fori_loop(0, i + 1, body, jnp.zeros((t_sz, t_sz), I32))
        return jnp.sum(cnt, axis=0, keepdims=True)

    thr = jnp.maximum(_bisect_threshold(count_ge, (1, t_sz), topk), INT_MIN + 1)

    for h in range(ATT_HEADS):
        hs = slice(h * ATT_HD, (h + 1) * ATT_HD)
        qt = qt_ref[0, hs, :]

        def body(t, carry, hs=hs, qt=qt):
            m, l, acc = carry
            s = jnp.dot(k_ref[tile_rows(t), hs], qt, preferred_element_type=F32) * scale
            s = jnp.where(keys_ref[tile_rows(t), :] >= thr, s, -jnp.inf)
            m_new = jnp.maximum(m, jnp.max(s, axis=0, keepdims=True))
            m_safe = jnp.where(m_new == -jnp.inf, 0.0, m_new)
            p = jnp.exp(s - m_safe)
            alpha = jnp.exp(m - m_safe)
            l = alpha * l + jnp.sum(p, axis=0, keepdims=True)
            acc = acc * alpha + jnp.dot(vt_ref[t, hs, :], p.astype(BF16), preferred_element_type=F32)
            return m_new, l, acc

        init = (jnp.full((1, t_sz), -jnp.inf, F32), jnp.zeros((1, t_sz), F32), jnp.zeros((ATT_HD, t_sz), F32))
        _, l, acc = lax.fori_loop(0, i + 1, body, init)
        o_ref[0, hs, :] = acc / l


def dsa_prompt(aq, ak, av, iq, iw, ik):
    seq = aq.shape[0]
    nb = seq // CHUNK
    topk = min(TOPK_MAX, seq // 4)
    iq3 = iq.reshape(seq, IDX_HEADS, IDX_DIM)
    q_hi = iq3.astype(BF16)
    q_lo = (iq3 - q_hi.astype(F32)).astype(BF16)
    qx = jnp.concatenate([q_hi, q_lo, q_hi, q_lo], axis=-1)
    qit = qx.reshape(nb, CHUNK, IDX_HEADS, 4 * IDX_DIM).transpose(0, 3, 2, 1).reshape(nb, 4 * IDX_DIM, IDX_HEADS * CHUNK)
    k_hi = ik.astype(BF16)
    k_lo = (ik - k_hi.astype(F32)).astype(BF16)
    kix = jnp.concatenate([k_hi, k_hi, k_lo, k_lo], axis=-1)
    w_s = (iw * (IDX_DIM ** -0.5 * IDX_HEADS ** -0.5)).reshape(nb, CHUNK, IDX_HEADS).transpose(0, 2, 1)
    w_s = w_s.reshape(nb, 1, IDX_HEADS * CHUNK)
    qt = aq.reshape(nb, CHUNK, ATT_W).transpose(0, 2, 1).astype(BF16)
    kb = ak.astype(BF16)
    vt = av.reshape(nb, CHUNK, ATT_W).transpose(0, 2, 1).astype(BF16)
    resident = lambda shape: pl.BlockSpec(shape, lambda i: (0,) * len(shape))
    o_t = pl.pallas_call(
        functools.partial(_dsa_prompt_kernel, topk=topk, scale=ATT_HD ** -0.5),
        grid=(nb,),
        in_specs=[resident(kix.shape),
                  pl.BlockSpec((1, 4 * IDX_DIM, IDX_HEADS * CHUNK), lambda i: (i, 0, 0)),
                  pl.BlockSpec((1, 1, IDX_HEADS * CHUNK), lambda i: (i, 0, 0)),
                  resident(kb.shape), resident(vt.shape),
                  pl.BlockSpec((1, ATT_W, CHUNK), lambda i: (i, 0, 0))],
        out_specs=pl.BlockSpec((1, ATT_W, CHUNK), lambda i: (i, 0, 0)),
        out_shape=jax.ShapeDtypeStruct((nb, ATT_W, CHUNK), F32),
        scratch_shapes=[pltpu.VMEM((seq, CHUNK), I32)],
        compiler_params=_cparams(("arbitrary",), 56 * 1024 * 1024),
    )(kix, qit, w_s, kb, vt, qt)
    return o_t.transpose(0, 2, 1).reshape(seq, ATT_W)


def _pmem_kernel(hm_ref, wq_ref, mk_ref, mv_ref, wo_ref, x_ref, o_ref):
    q = jnp.dot(hm_ref[...], wq_ref[...], preferred_element_type=F32)
    outs = []
    for h in range(MEM_HEADS):
        hs = slice(h * MEM_HD, (h + 1) * MEM_HD)
        s = lax.dot_general(q[:, hs].astype(BF16), mk_ref[:, hs], NT_DIMS, preferred_element_type=F32) * MEM_HD ** -0.5
        e = jnp.exp(s - jnp.max(s, axis=-1, keepdims=True))
        oh = jnp.dot(e.astype(BF16), mv_ref[:, hs], preferred_element_type=F32) / jnp.sum(e, axis=-1, keepdims=True)
        outs.append(oh.astype(BF16))
    o = jnp.concatenate(outs, axis=1)
    o_ref[...] = x_ref[...] + jnp.dot(o, wo_ref[...], preferred_element_type=F32)


def prompt_mem_attend(x, hm, wq, mk, mv, wo, tm=512):
    m, d = x.shape
    n_mem = mk.shape[0]
    full = lambda shape: pl.BlockSpec(shape, lambda i: (0,) * len(shape))
    return pl.pallas_call(
        _pmem_kernel,
        grid=(m // tm,),
        in_specs=[pl.BlockSpec((tm, d), lambda i: (i, 0)), full((d, MEM_W)), full((n_mem, MEM_W)),
                  full((n_mem, MEM_W)), full((MEM_W, d)), pl.BlockSpec((tm, d), lambda i: (i, 0))],
        out_specs=pl.BlockSpec((tm, d), lambda i: (i, 0)),
        out_shape=jax.ShapeDtypeStruct((m, d), F32),
        compiler_params=_cparams(("parallel",), 48 * 1024 * 1024),
    )(hm, wq, mk, mv, wo, x)


def _conv_seq(a, i, j, part, carry_ref, cw_ref, cb_ref, row):
    tm = a.shape[0]
    prev = carry_ref[j, part]
    p1 = prev[7:8, :]
    p2 = prev[6:7, :]
    r1 = jnp.where(row == 0, p1, pltpu.roll(a, 1, axis=0))
    r2 = jnp.where(row == 0, p2, jnp.where(row == 1, p1, pltpu.roll(a, 2, axis=0)))
    carry_ref[j, part] = a[tm - 8:tm, :]
    return cb_ref[...] + cw_ref[0:1, :] * r2 + cw_ref[1:2, :] * r1 + cw_ref[2:3, :] * a


def _ffn_seq_kernel(hx_ref, wg_ref, wu_ref, cwg_ref, cwu_ref, cbg_ref, cbu_ref, wd_ref, x_ref,
                    o_ref, lg_ref, lu_ref, carry_ref):
    i = pl.program_id(0)
    j = pl.program_id(1)
    hx = hx_ref[...]
    tm = hx.shape[0]
    row = lax.broadcasted_iota(I32, (tm, 1), 0)

    @pl.when(i == 0)
    def _():
        carry_ref[j] = jnp.zeros(carry_ref.shape[1:], F32)

    ag = jnp.dot(hx, wg_ref[...], preferred_element_type=F32)
    au = jnp.dot(hx, wu_ref[...], preferred_element_type=F32)
    lg_ref[0] = ag[tm - 8:tm, :]
    lu_ref[0] = au[tm - 8:tm, :]
    cg = _conv_seq(ag, i, j, 0, carry_ref, cwg_ref, cbg_ref, row)
    cu = _conv_seq(au, i, j, 1, carry_ref, cwu_ref, cbu_ref, row)
    act = ((cg * jax.nn.sigmoid(cg)) * cu).astype(BF16)
    f = jnp.dot(act, wd_ref[...], preferred_element_type=F32)

    @pl.when(j == 0)
    def _():
        o_ref[...] = x_ref[...] + f

    @pl.when(j > 0)
    def _():
        o_ref[...] += f


def ffn_seq(x, hx, ffn_w, tm=512):
    wg, wu, cwg, cwu, cbg, cbu, wd = ffn_w
    m, d = x.shape
    tf = FFN_TF
    nf = D_FF_PAD // tf
    col = lambda r: pl.BlockSpec((r, tf), lambda i, j: (0, j))
    tail = pl.BlockSpec((1, 8, tf), lambda i, j: (i, 0, j))
    return pl.pallas_call(
        _ffn_seq_kernel,
        grid=(m // tm, nf),
        in_specs=[pl.BlockSpec((tm, d), lambda i, j: (i, 0)), col(d), col(d), col(CONV_W), col(CONV_W), col(1), col(1),
                  pl.BlockSpec((tf, d), lambda i, j: (j, 0)), pl.BlockSpec((tm, d), lambda i, j: (i, 0))],
        out_specs=[pl.BlockSpec((tm, d), lambda i, j: (i, 0)), tail, tail],
        out_shape=[jax.ShapeDtypeStruct((m, d), F32), jax.ShapeDtypeStruct((m // tm, 8, D_FF_PAD), F32),
                   jax.ShapeDtypeStruct((m // tm, 8, D_FF_PAD), F32)],
        scratch_shapes=[pltpu.VMEM((nf, 2, 8, tf), F32)],
        compiler_params=_cparams(("arbitrary", "arbitrary"), 48 * 1024 * 1024),
    )(hx, wg, wu, cwg, cwu, cbg, cbu, wd, x)


def _ffn_step_kernel(hx_ref, wg_ref, wu_ref, cwg_ref, cwu_ref, cbg_ref, cbu_ref, wd_ref, x_ref,
                     pg0_ref, pg1_ref, pu0_ref, pu1_ref, o_ref, ag_ref, au_ref):
    j = pl.program_id(0)
    hx = hx_ref[...]
    ag = jnp.dot(hx, wg_ref[...], preferred_element_type=F32)
    au = jnp.dot(hx, wu_ref[...], preferred_element_type=F32)
    ag_ref[...] = ag
    au_ref[...] = au
    cg = cbg_ref[...] + cwg_ref[0:1, :] * pg0_ref[...] + cwg_ref[1:2, :] * pg1_ref[...] + cwg_ref[2:3, :] * ag
    cu = cbu_ref[...] + cwu_ref[0:1, :] * pu0_ref[...] + cwu_ref[1:2, :] * pu1_ref[...] + cwu_ref[2:3, :] * au
    act = ((cg * jax.nn.sigmoid(cg)) * cu).astype(BF16)
    f = jnp.dot(act, wd_ref[...], preferred_element_type=F32)

    @pl.when(j == 0)
    def _():
        o_ref[...] = x_ref[...] + f

    @pl.when(j > 0)
    def _():
        o_ref[...] += f


def ffn_step(x, hx, ffn_w, prev):
    wg, wu, cwg, cwu, cbg, cbu, wd = ffn_w
    m, d = x.shape
    tf = FFN_TF
    nf = D_FF_PAD // tf
    col = lambda r: pl.BlockSpec((r, tf), lambda j: (0, j))
    row = pl.BlockSpec((m, d), lambda j: (0, 0))
    return pl.pallas_call(
        _ffn_step_kernel,
        grid=(nf,),
        in_specs=[row, col(d), col(d), col(CONV_W), col(CONV_W), col(1), col(1),
                  pl.BlockSpec((tf, d), lambda j: (j, 0)), row, col(m), col(m), col(m), col(m)],
        out_specs=[row, col(m), col(m)],
        out_shape=[jax.ShapeDtypeStruct((m, d), F32), jax.ShapeDtypeStruct((m, D_FF_PAD), F32),
                   jax.ShapeDtypeStruct((m, D_FF_PAD), F32)],
        compiler_params=_cparams(("arbitrary",), 48 * 1024 * 1024),
    )(hx, wg, wu, cwg, cwu, cbg, cbu, wd, x, *prev)


def _smix_kernel(qc_ref, kc_ref, v_ref, rg_ref, gu_ref, gv_ref, st_ref, cos_ref, sin_ref, g1_ref, retg_ref,
                 gmg_ref, wrow_ref, brow_ref, a_ref, c_ref, ns_ref, gmv_ref):
    cosc = cos_ref[...]
    sinc = sin_ref[...]
    half = RET_DK // 2

    def rot(x):
        x1, x2 = x[:half], x[half:]
        return jnp.concatenate([x1 * cosc - x2 * sinc, x1 * sinc + x2 * cosc], axis=0)

    for h in range(RET_HEADS):
        vs = slice(h * RET_DV, (h + 1) * RET_DV)
        qr = rot(qc_ref[0, h])
        kr = rot(kc_ref[0, h]) * RET_DK ** -0.5
        v = v_ref[0, :, vs]
        state = st_ref[0, h]
        g1 = g1_ref[h]
        inner = jnp.sum(qr * kr, axis=0, keepdims=True) * v
        cross = jnp.sum(qr * state, axis=0, keepdims=True) * g1
        o = inner + cross
        ns_ref[0, h] = g1 * state + kr * v
        ms = jnp.mean(o * o, axis=-1, keepdims=True)
        y = o * lax.rsqrt(ms + EPS) * retg_ref[:, vs]
        g = rg_ref[0, :, vs]
        a_ref[0, :, vs] = y * (g * jax.nn.sigmoid(g))

    u = jax.nn.gelu(gu_ref[0])
    v = jax.nn.gelu(gv_ref[0])
    ms = jnp.mean(v * v, axis=-1, keepdims=True)
    vn = v * lax.rsqrt(ms + EPS) * gmg_ref[...]
    c_ref[0] = u * (wrow_ref[...] * vn + brow_ref[...])
    gmv_ref[0] = vn


def sample_mixers(p_main, state, consts, retg_row, gmg_row, wrow, brow):
    nb = p_main.shape[0]
    cosc, sinc, g1 = consts
    qc = p_main[:, 0:RET_QK].reshape(nb, RET_HEADS, RET_DK, 1)
    kc = p_main[:, RET_QK:2 * RET_QK].reshape(nb, RET_HEADS, RET_DK, 1)
    p3 = p_main.reshape(nb, 1, N_MAIN)
    per_b = lambda width, blk: pl.BlockSpec((1, 1, width), lambda b: (b, 0, blk))
    col4 = pl.BlockSpec((1, RET_HEADS, RET_DK, 1), lambda b: (b, 0, 0, 0))
    st4 = pl.BlockSpec((1, RET_HEADS, RET_DK, RET_DV), lambda b: (b, 0, 0, 0))
    full = lambda shape: pl.BlockSpec(shape, lambda b: (0,) * len(shape))
    return pl.pallas_call(
        _smix_kernel,
        grid=(nb,),
        in_specs=[col4, col4, per_b(1024, 1), per_b(1024, 2), per_b(512, _M_GU // 512), per_b(512, _M_GV // 512), st4,
                  full(cosc.shape), full(sinc.shape), full(g1.shape), full(retg_row.shape), full(gmg_row.shape),
                  full(wrow.shape), full(brow.shape)],
        out_specs=[per_b(RET_V, 0), per_b(GM_W, 0), st4, per_b(GM_W, 0)],
        out_shape=[jax.ShapeDtypeStruct((nb, 1, RET_V), F32), jax.ShapeDtypeStruct((nb, 1, GM_W), F32),
                   jax.ShapeDtypeStruct(state.shape, F32), jax.ShapeDtypeStruct((nb, 1, GM_W), F32)],
        compiler_params=_cparams(("parallel",)),
    )(qc, kc, p3, p3, p3, p3, state, cosc, sinc, g1, retg_row, gmg_row, wrow, brow)


def _page_scores_kernel(pt_ref, qx_ref, w_ref, qbd_ref, *refs, n_pg, scale):
    ki_refs = refs[:n_pg]
    k_refs = refs[n_pg:2 * n_pg]
    isc_ref, asc_ref = refs[2 * n_pg:]
    qx = qx_ref[0]
    q_hi = qx[:IDX_HEADS]
    w = w_ref[0]
    qbd = qbd_ref[0]
    for n in range(n_pg):
        kp = ki_refs[n][0]
        k_hi = kp.astype(BF16)
        k_lo = (kp - k_hi.astype(F32)).astype(BF16)
        s2 = lax.dot_general(qx, k_hi, NT_DIMS, preferred_element_type=F32)
        s = s2[:IDX_HEADS] + s2[IDX_HEADS:] + lax.dot_general(q_hi, k_lo, NT_DIMS, preferred_element_type=F32)
        isc_ref[0, n] = jnp.sum(jnp.maximum(s, 0.0) * w, axis=0, keepdims=True)
        asc_ref[0, n] = lax.dot_general(qbd, k_refs[n][0].astype(BF16), NT_DIMS, preferred_element_type=F32) * scale


def page_scores(page_table, qx, w_col, qbd, pool_ki, pool_k, n_pg):
    nb, n_pages = page_table.shape
    groups = n_pages // n_pg
    pt = page_table.reshape(-1)

    def pool_spec(width, n):
        return pl.BlockSpec((1, PAGE_SIZE, width), lambda b, g, pt_ref: (pt_ref[b * n_pages + g * n_pg + n], 0, 0))

    per_b = lambda shape: pl.BlockSpec((1,) + shape, lambda b, g, pt_ref: (b,) + (0,) * len(shape))
    grid_spec = pltpu.PrefetchScalarGridSpec(
        num_scalar_prefetch=1,
        grid=(nb, groups),
        in_specs=[per_b((2 * IDX_HEADS, IDX_DIM)), per_b((IDX_HEADS, 1)), per_b((8, ATT_W))]
        + [pool_spec(IDX_DIM, n) for n in range(n_pg)] + [pool_spec(ATT_W, n) for n in range(n_pg)],
        out_specs=[pl.BlockSpec((1, n_pg, 1, PAGE_SIZE), lambda b, g, pt_ref: (b, g, 0, 0)),
                   pl.BlockSpec((1, n_pg, 8, PAGE_SIZE), lambda b, g, pt_ref: (b, g, 0, 0))],
    )
    return pl.pallas_call(
        functools.partial(_page_scores_kernel, n_pg=n_pg, scale=ATT_HD ** -0.5),
        grid_spec=grid_spec,
        out_shape=[jax.ShapeDtypeStruct((nb, n_pages, 1, PAGE_SIZE), F32),
                   jax.ShapeDtypeStruct((nb, n_pages, 8, PAGE_SIZE), F32)],
        compiler_params=_cparams(("arbitrary", "arbitrary")),
    )(pt, qx, w_col, qbd, *([pool_ki] * n_pg), *([pool_k] * n_pg))


def _select_softmax_kernel(isc_ref, asc_ref, p_ref, keys_ref, *, topk, n_valid):
    colid = lax.broadcasted_iota(I32, isc_ref.shape, 1)
    keys_ref[...] = jnp.where(colid < n_valid, _sortable_key(isc_ref[...]), INT_MIN)

    def count_ge(mid):
        return jnp.sum((keys_ref[...] >= mid).astype(I32), axis=1, keepdims=True)

    thr = jnp.maximum(_bisect_threshold(count_ge, (isc_ref.shape[0], 1), topk), INT_MIN + 1)
    msk = keys_ref[...] >= thr
    for h in range(ATT_HEADS):
        s = jnp.where(msk, asc_ref[h], -jnp.inf)
        e = jnp.exp(s - jnp.max(s, axis=1, keepdims=True))
        p_ref[h] = e / jnp.sum(e, axis=1, keepdims=True)


def select_softmax(isc, asc, topk, n_valid):
    return pl.pallas_call(
        functools.partial(_select_softmax_kernel, topk=topk, n_valid=n_valid),
        out_shape=jax.ShapeDtypeStruct(asc.shape, F32),
        scratch_shapes=[pltpu.VMEM(isc.shape, I32)],
        compiler_params=_cparams(None, 48 * 1024 * 1024),
    )(isc, asc)


def _page_values_kernel(pt_ref, p_ref, *refs, n_pg):
    v_refs = refs[:n_pg]
    o_ref = refs[n_pg]
    acc = jnp.zeros(o_ref.shape[1:], F32)
    for n in range(n_pg):
        acc = acc + jnp.dot(p_ref[0, n], v_refs[n][0].astype(BF16), preferred_element_type=F32)

    @pl.when(pl.program_id(1) == 0)
    def _():
        o_ref[0] = acc

    @pl.when(pl.program_id(1) > 0)
    def _():
        o_ref[0] += acc


def page_values(page_table, p_pages, pool_v, n_pg):
    nb, n_pages = page_table.shape
    groups = n_pages // n_pg
    pt = page_table.reshape(-1)

    def pool_spec(n):
        return pl.BlockSpec((1, PAGE_SIZE, ATT_W), lambda b, g, pt_ref: (pt_ref[b * n_pages + g * n_pg + n], 0, 0))

    grid_spec = pltpu.PrefetchScalarGridSpec(
        num_scalar_prefetch=1,
        grid=(nb, groups),
        in_specs=[pl.BlockSpec((1, n_pg, 8, PAGE_SIZE), lambda b, g, pt_ref: (b, g, 0, 0))]
        + [pool_spec(n) for n in range(n_pg)],
        out_specs=pl.BlockSpec((1, 8, ATT_W), lambda b, g, pt_ref: (b, 0, 0)),
    )
    return pl.pallas_call(
        functools.partial(_page_values_kernel, n_pg=n_pg),
        grid_spec=grid_spec,
        out_shape=jax.ShapeDtypeStruct((nb, 8, ATT_W), F32),
        compiler_params=_cparams(("arbitrary", "arbitrary")),
    )(pt, p_pages, *([pool_v] * n_pg))


def _head_rows(q):
    nb = q.shape[0]
    q4 = q.reshape(nb, 1, ATT_HEADS, ATT_HD)
    eye = jnp.eye(8, ATT_HEADS, dtype=q.dtype).reshape(1, 8, ATT_HEADS, 1)
    return (q4 * eye).reshape(nb, 8, ATT_HEADS * ATT_HD).astype(BF16)


def _diag_heads(o):
    nb = o.shape[0]
    o4 = o[:, :ATT_HEADS].reshape(nb, ATT_HEADS, ATT_HEADS, ATT_HD)
    return jnp.stack([o4[:, h, h] for h in range(ATT_HEADS)], axis=1).reshape(nb, ATT_HEADS * ATT_HD)


def dsa_sample(aq, ak, av, iq, iw, ik, ck, cv, cki, page_table, n_pg=8):
    nb, n_pages = page_table.shape
    past = n_pages * PAGE_SIZE
    topk = min(TOPK_MAX, (past + 1) // 4)
    iq3 = iq.reshape(nb, IDX_HEADS, IDX_DIM)
    q_hi = iq3.astype(BF16)
    q_lo = (iq3 - q_hi.astype(F32)).astype(BF16)
    qx = jnp.concatenate([q_hi, q_lo], axis=1)
    w_col = (iw * (IDX_DIM ** -0.5 * IDX_HEADS ** -0.5)).reshape(nb, IDX_HEADS, 1)
    qbd = _head_rows(aq)
    pool_k = ck.reshape(ck.shape[0], PAGE_SIZE, ATT_W)
    pool_v = cv.reshape(cv.shape[0], PAGE_SIZE, ATT_W)
    own = jnp.arange(nb, dtype=I32).reshape(nb, 1)
    pad_page = lambda a: jnp.pad(a[:, None, :], ((0, 0), (0, PAGE_SIZE - 1), (0, 0)))
    isc_p, asc_p = page_scores(page_table, qx, w_col, qbd, cki, pool_k, n_pg)
    isc_n, asc_n = page_scores(own, qx, w_col, qbd, pad_page(ik), pad_page(ak), 1)
    isc = jnp.concatenate([isc_p, isc_n], axis=1).reshape(nb, past + PAGE_SIZE)
    asc = jnp.concatenate([asc_p, asc_n], axis=1)[:, :, :ATT_HEADS]
    asc = asc.transpose(2, 0, 1, 3).reshape(ATT_HEADS, nb, past + PAGE_SIZE)
    p = select_softmax(isc, asc, topk, past + 1)
    p = p.reshape(ATT_HEADS, nb, n_pages + 1, PAGE_SIZE).transpose(1, 2, 0, 3)
    p = jnp.pad(p, ((0, 0), (0, 0), (0, 8 - ATT_HEADS), (0, 0))).astype(BF16)
    o_p = page_values(page_table, p[:, :n_pages], pool_v, n_pg)
    o_n = page_values(own, p[:, n_pages:], pad_page(av), 1)
    return _diag_heads(o_p + o_n)


def _smem_kernel(qbd_ref, mk_ref, mv_ref, o_ref):
    for b in range(qbd_ref.shape[0]):
        s = lax.dot_general(qbd_ref[b], mk_ref[b].astype(BF16), NT_DIMS, preferred_element_type=F32) * MEM_HD ** -0.5
        e = jnp.exp(s - jnp.max(s, axis=-1, keepdims=True))
        p = e / jnp.sum(e, axis=-1, keepdims=True)
        o_ref[b] = jnp.dot(p.astype(BF16), mv_ref[b].astype(BF16), preferred_element_type=F32)


def sample_mem_core(q, mem_k, mem_v, bb=8):
    nb, n_mem = mem_k.shape[:2]
    qbd = _head_rows(q)
    mk = mem_k.reshape(nb, n_mem, MEM_W)
    mv = mem_v.reshape(nb, n_mem, MEM_W)
    o = pl.pallas_call(
        _smem_kernel,
        grid=(nb // bb,),
        in_specs=[pl.BlockSpec((bb, 8, MEM_W), lambda i: (i, 0, 0)), pl.BlockSpec((bb, n_mem, MEM_W), lambda i: (i, 0, 0)),
                  pl.BlockSpec((bb, n_mem, MEM_W), lambda i: (i, 0, 0))],
        out_specs=pl.BlockSpec((bb, 8, MEM_W), lambda i: (i, 0, 0)),
        out_shape=jax.ShapeDtypeStruct((nb, 8, MEM_W), F32),
        compiler_params=_cparams(("parallel",), 48 * 1024 * 1024),
    )(qbd, mk, mv)
    return _diag_heads(o)


def _prep_layer(w_in, w_out, w_mq, w_mk, w_mv, w_mo, w_up, conv_w, conv_b, w_down, gm_ws, gm_bs):
    w_main = jnp.concatenate([w_in[:, :_O_IQ_END], w_in[:, _O_GU:]], axis=1).astype(BF16)
    w_idx = jnp.concatenate([w_in[:, _O_IK:_O_GU], w_in[:, _O_IW:_O_IK],
                             jnp.zeros((D_MODEL, LANE - IDX_DIM - IDX_HEADS), w_in.dtype)], axis=1).astype(BF16)
    padc = lambda a: jnp.pad(a, ((0, 0), (0, D_FF_PAD - D_FF)))
    ffn_w = (padc(w_up[:, :D_FF]).astype(BF16), padc(w_up[:, D_FF:]).astype(BF16),
             padc(conv_w[:, :D_FF]), padc(conv_w[:, D_FF:]),
             padc(conv_b[None, :D_FF]), padc(conv_b[None, D_FF:]),
             jnp.pad(w_down, ((0, D_FF_PAD - D_FF), (0, 0))).astype(BF16))
    causal = jnp.tril(jnp.ones((CHUNK, CHUNK), dtype=bool))
    wtril = jnp.where(causal[None], gm_ws, 0.0).astype(BF16)
    bsm = jnp.broadcast_to(gm_bs[:, :, None], (GM_GROUPS, CHUNK, GM_GW))
    wrow = jnp.repeat(gm_ws[:, 0, 0], GM_GW).reshape(1, GM_W)
    brow = jnp.repeat(gm_bs[:, 0], GM_GW).reshape(1, GM_W)
    return dict(w_main=w_main, w_idx=w_idx, w_out=w_out.astype(BF16), w_mq=w_mq.astype(BF16),
                w_mk=w_mk.astype(BF16), w_mv=w_mv.astype(BF16), w_mo=w_mo.astype(BF16), ffn_w=ffn_w,
                wtril=wtril, bsm=bsm, wrow=wrow, brow=brow)


def _retention_consts(seq, past):
    log_g = jnp.log(1.0 - 2.0 ** (-5.0 - jnp.arange(RET_HEADS, dtype=F32)))
    half = RET_DK // 2
    inv = ROPE_BASE ** (-jnp.arange(half, dtype=F32) / half)
    ang = jnp.arange(seq).astype(F32)[:, None] * inv[None, :]
    cos, sin = jnp.cos(ang), jnp.sin(ang)
    cosf = jnp.concatenate([cos, cos], axis=1)
    sinf = jnp.concatenate([-sin, sin], axis=1)
    n = jnp.arange(CHUNK, dtype=F32)
    diff = n[:, None] - n[None, :]
    causal = diff >= 0
    dmat = jnp.where(causal[None], jnp.exp(log_g[:, None, None] * jnp.where(causal, diff, 0.0)[None]), 0.0)
    cdec = jnp.broadcast_to(jnp.exp(log_g[:, None] * (n[None, :] + 1.0))[:, :, None], (RET_HEADS, CHUNK, RET_DV))
    kdec = jnp.broadcast_to(jnp.exp(log_g[:, None] * (CHUNK - 1.0 - n[None, :]))[:, :, None], (RET_HEADS, CHUNK, RET_DK))
    gc = jnp.broadcast_to(jnp.exp(log_g * CHUNK)[:, None, None], (RET_HEADS, RET_DK, RET_DV))
    ang_s = jnp.full((1,), past, dtype=F32)[:, None] * inv[None, :]
    cosc = jnp.cos(ang_s).reshape(half, 1)
    sinc = jnp.sin(ang_s).reshape(half, 1)
    g1 = jnp.broadcast_to(jnp.exp(log_g * 1.0)[:, None, None], (RET_HEADS, 1, RET_DV))
    return (cosf, sinf, dmat, cdec, kdec, gc), (cosc, sinc, g1)


def _pad_ff(a):
    return jnp.pad(a, ((0, 0), (0, D_FF_PAD - D_FF)))


def kernel(x_prompt, x_sample, mem_prompt, cache_k, cache_v, cache_kidx, page_table, cache_mem_k, cache_mem_v, state_ret, state_conv, norm_mix_g, w_in, ret_norm_g, gm_norm_g, gm_ws, gm_bs, w_out, norm_mem_g, mem_in_g, w_mq, w_mk, w_mv, w_mo, norm_ffn_g, w_up, conv_w, conv_b, w_down, final_norm_g):
    depth = w_in.shape[0]
    seq = x_prompt.shape[1]
    nbd = x_sample.shape[0]
    n_mem = mem_prompt.shape[1]
    past = page_table.shape[1] * PAGE_SIZE
    p_consts, s_consts = _retention_consts(seq, past)

    xp = x_prompt.reshape(seq, D_MODEL)
    xs = x_sample.reshape(nbd, D_MODEL)
    mem = mem_prompt.reshape(n_mem, D_MODEL)
    outs = {k: [] for k in ("kp", "vp", "kip", "ks", "vs", "kis", "mkp", "mvp", "rsp", "rss", "csp", "css", "gvp", "gvs")}

    for l in range(depth):
        w = _prep_layer(w_in[l], w_out[l], w_mq[l], w_mk[l], w_mv[l], w_mo[l], w_up[l], conv_w[l], conv_b[l],
                        w_down[l], gm_ws[l], gm_bs[l])
        retg_row = ret_norm_g[l].reshape(1, RET_V)
        gmg_row = gm_norm_g[l].reshape(1, GM_W)

        h1 = rmsnorm_rows(xp, norm_mix_g[l], BF16)
        pm = matmul(h1, w["w_main"])
        pi = matmul(h1, w["w_idx"])
        ak, av = pm[:, _M_AK:_M_AV], pm[:, _M_AV:_M_IQ]
        ik = pi[:, :IDX_DIM]
        a_out, c_out, r_state, gm_v = prompt_mixers(pm, p_consts, retg_row, gmg_row, w["wtril"], w["bsm"])
        b_out = dsa_prompt(pm[:, _M_AQ:_M_AK], ak, av, pm[:, _M_IQ:_M_GU], pi[:, IDX_DIM:IDX_DIM + IDX_HEADS], ik)
        mixed = jnp.concatenate([a_out, b_out.astype(BF16), c_out], axis=1)
        xp = matmul(mixed, w["w_out"], res=xp)
        m_in = rmsnorm_rows(mem, mem_in_g[l], BF16)
        mk = matmul(m_in, w["w_mk"])
        mv = matmul(m_in, w["w_mv"])
        hm = rmsnorm_rows(xp, norm_mem_g[l], BF16)
        xp = prompt_mem_attend(xp, hm, w["w_mq"], mk.astype(BF16), mv.astype(BF16), w["w_mo"])
        hf = rmsnorm_rows(xp, norm_ffn_g[l], BF16)
        xp, tail_g, tail_u = ffn_seq(xp, hf, w["ffn_w"])
        last_g, last_u = tail_g[-1], tail_u[-1]
        outs["kp"].append(ak.reshape(1, seq, ATT_HEADS, ATT_HD))
        outs["vp"].append(av.reshape(1, seq, ATT_HEADS, ATT_HD))
        outs["kip"].append(ik.reshape(1, seq, IDX_DIM))
        outs["mkp"].append(mk.reshape(1, n_mem, MEM_HEADS, MEM_HD))
        outs["mvp"].append(mv.reshape(1, n_mem, MEM_HEADS, MEM_HD))
        outs["rsp"].append(r_state.reshape(1, RET_HEADS, RET_DK, RET_DV))
        outs["csp"].append(jnp.concatenate([last_g[8 - (CONV_W - 1):, :D_FF], last_u[8 - (CONV_W - 1):, :D_FF]], axis=1)[None])
        outs["gvp"].append(gm_v[None])

        h1 = rmsnorm_rows(xs, norm_mix_g[l], BF16)
        pm = matmul(h1, w["w_main"])
        pi = matmul(h1, w["w_idx"])
        ak, av = pm[:, _M_AK:_M_AV], pm[:, _M_AV:_M_IQ]
        ik = pi[:, :IDX_DIM]
        a_out, c_out, r_state, gm_v = sample_mixers(pm, state_ret[l], s_consts, retg_row, gmg_row, w["wrow"], w["brow"])
        b_out = dsa_sample(pm[:, _M_AQ:_M_AK], ak, av, pm[:, _M_IQ:_M_GU], pi[:, IDX_DIM:IDX_DIM + IDX_HEADS], ik,
                           cache_k[l], cache_v[l], cache_kidx[l], page_table)
        mixed = jnp.concatenate([a_out.reshape(nbd, RET_V), b_out, c_out.reshape(nbd, GM_W)], axis=1).astype(BF16)
        xs = matmul(mixed, w["w_out"], res=xs)
        hm = rmsnorm_rows(xs, norm_mem_g[l], BF16)
        q = matmul(hm, w["w_mq"])
        o = sample_mem_core(q, cache_mem_k[l], cache_mem_v[l])
        xs = matmul(o.astype(BF16), w["w_mo"], res=xs)
        hf = rmsnorm_rows(xs, norm_ffn_g[l], BF16)
        sc = state_conv[l]
        prev = (_pad_ff(sc[:, 0, :D_FF]), _pad_ff(sc[:, 1, :D_FF]), _pad_ff(sc[:, 0, D_FF:]), _pad_ff(sc[:, 1, D_FF:]))
        xs, a_g, a_u = ffn_step(xs, hf, w["ffn_w"], prev)
        a_new = jnp.concatenate([a_g[:, :D_FF], a_u[:, :D_FF]], axis=1)
        outs["ks"].append(ak.reshape(nbd, 1, ATT_HEADS, ATT_HD))
        outs["vs"].append(av.reshape(nbd, 1, ATT_HEADS, ATT_HD))
        outs["kis"].append(ik.reshape(nbd, 1, IDX_DIM))
        outs["rss"].append(r_state)
        outs["css"].append(jnp.stack([sc[:, 1, :], a_new], axis=1))
        outs["gvs"].append(gm_v)

    y_prompt = rmsnorm_rows(xp, final_norm_g, F32).reshape(1, seq, D_MODEL)
    y_sample = rmsnorm_rows(xs, final_norm_g, F32).reshape(nbd, 1, D_MODEL)
    st = lambda k: jnp.stack(outs[k])
    return (y_prompt, y_sample, st("kp"), st("vp"), st("kip"), st("ks"), st("vs"), st("kis"), st("mkp"), st("mvp"),
            st("rsp"), st("rss"), st("csp"), st("css"), st("gvp"), st("gvs"))
```

```python
import functools

import jax
import jax.numpy as jnp
from jax import lax
from jax.experimental import pallas as pl
from jax.experimental.pallas import tpu as pltpu

F32 = jnp.float32
BF16 = jnp.bfloat16
I32 = jnp.int32

D_MODEL = 2048
PAGE_SIZE = 128
RET_HEADS = 4
RET_DK = 128
RET_DV = 256
CHUNK = 128
ROPE_BASE = 10000.0
ATT_HEADS = 4
ATT_HD = 128
IDX_HEADS = 16
IDX_DIM = 64
TOPK_MAX = 256
GM_GROUPS = 4
GM_GW = 128
MEM_HEADS = 4
MEM_HD = 128
D_FF = 5504
CONV_W = 3
EPS = 1e-6

RET_QK = RET_HEADS * RET_DK
RET_V = RET_HEADS * RET_DV
ATT_W = ATT_HEADS * ATT_HD
IDX_Q = IDX_HEADS * IDX_DIM
GM_W = GM_GROUPS * GM_GW
MEM_W = MEM_HEADS * MEM_HD

_O_IQ_END = 2 * RET_QK + 2 * RET_V + 3 * ATT_W + IDX_Q
_O_IW = _O_IQ_END
_O_IK = _O_IW + IDX_HEADS
_O_GU = _O_IK + IDX_DIM
_M_AQ = 2 * RET_QK + 2 * RET_V
_M_AK = _M_AQ + ATT_W
_M_AV = _M_AK + ATT_W
_M_IQ = _M_AV + ATT_W
_M_GU = _M_IQ + IDX_Q
_M_GV = _M_GU + GM_W
N_MAIN = _M_GV + GM_W
LANE = 128
D_FF_PAD = 5632
FFN_TF = 512
DSA_GROUP = 512
DSA_SCORE_ROWS = 256
BISECT_STEPS_PER_CHECK = 4

INT_MIN = -2 ** 31
INT_MAX = 2 ** 31 - 1
V7X_VMEM_BYTES = 64 * 1024 * 1024
NT_DIMS = (((1,), (1,)), ((), ()))


def _cparams(semantics, vmem_bytes=None):
    return pltpu.CompilerParams(dimension_semantics=semantics, vmem_limit_bytes=vmem_bytes)


def _sortable_key(x):
    b = lax.bitcast_convert_type(x, I32)
    return b ^ ((b >> 31) & INT_MAX)


def _bisect_threshold(count_ge, lo, hi, topk):
    def cond(st):
        it, active, _, _ = st
        return jnp.logical_and(it < 32, active > 0)

    def body(st):
        it, _, lo, hi = st
        for _ in range(BISECT_STEPS_PER_CHECK):
            mid = (lo >> 1) + (hi >> 1) + (lo & hi & 1)
            cnt = count_ge(mid)
            ge = cnt >= topk
            lo = jnp.where(ge, mid, lo)
            hi = jnp.where(cnt == topk, mid + 1, jnp.where(ge, hi, mid))
        active = jnp.max((hi - 1 > lo).astype(I32))
        return it + BISECT_STEPS_PER_CHECK, active, lo, hi

    _, _, lo, _ = lax.while_loop(cond, body, (jnp.int32(0), jnp.int32(1), lo, hi))
    return lo


def _norm_kernel(x_ref, g_ref, o_ref):
    x = x_ref[...]
    ms = jnp.mean(x * x, axis=-1, keepdims=True)
    o_ref[...] = (x * lax.rsqrt(ms + EPS) * g_ref[...]).astype(o_ref.dtype)


def rmsnorm_rows(x, g, out_dtype):
    m, d = x.shape
    tm = min(m, 256)
    return pl.pallas_call(
        _norm_kernel,
        grid=(m // tm,),
        in_specs=[pl.BlockSpec((tm, d), lambda i: (i, 0)), pl.BlockSpec((1, d), lambda i: (0, 0))],
        out_specs=pl.BlockSpec((tm, d), lambda i: (i, 0)),
        out_shape=jax.ShapeDtypeStruct((m, d), out_dtype),
        compiler_params=_cparams(("parallel",)),
    )(x, g.reshape(1, d))


def _mm_kernel(x_ref, w_ref, o_ref):
    o_ref[...] = jnp.dot(x_ref[...], w_ref[...], preferred_element_type=F32)


def _mm_res_kernel(x_ref, w_ref, r_ref, o_ref):
    o_ref[...] = r_ref[...] + jnp.dot(x_ref[...], w_ref[...], preferred_element_type=F32)


def matmul(x, w, res=None, tm=1024, tn=512):
    m, k = x.shape
    n = w.shape[1]
    tm = min(tm, m)
    tn = min(tn, n)
    assert m % tm == 0 and n % tn == 0
    in_specs = [pl.BlockSpec((tm, k), lambda i, j: (i, 0)), pl.BlockSpec((k, tn), lambda i, j: (0, j))]
    args = [x, w]
    body = _mm_kernel
    if res is not None:
        in_specs.append(pl.BlockSpec((tm, tn), lambda i, j: (i, j)))
        args.append(res)
        body = _mm_res_kernel
    return pl.pallas_call(
        body,
        grid=(m // tm, n // tn),
        in_specs=in_specs,
        out_specs=pl.BlockSpec((tm, tn), lambda i, j: (i, j)),
        out_shape=jax.ShapeDtypeStruct((m, n), F32),
        compiler_params=_cparams(("parallel", "parallel"), 48 * 1024 * 1024),
    )(*args)


def _mix_kernel(rq_ref, rk_ref, rv_ref, rg_ref, gu_ref, gv_ref, cos_ref, sin_ref, dmat_ref, cdec_ref, kdec_ref,
                gc_ref, retg_ref, gmg_ref, wtril_ref, bsm_ref, a_ref, c_ref, sfin_ref, gmv_ref, s_ref):
    c = pl.program_id(0)
    last = pl.num_programs(0) - 1

    @pl.when(c == 0)
    def _():
        s_ref[...] = jnp.zeros_like(s_ref)

    cosf = cos_ref[...]
    sinf = sin_ref[...]
    for h in range(RET_HEADS):
        q = rq_ref[:, h * RET_DK:(h + 1) * RET_DK]
        k = rk_ref[:, h * RET_DK:(h + 1) * RET_DK]
        qr = q * cosf + pltpu.roll(q, RET_DK // 2, axis=1) * sinf
        kr = (k * cosf + pltpu.roll(k, RET_DK // 2, axis=1) * sinf) * RET_DK ** -0.5
        vb = rv_ref[:, h * RET_DV:(h + 1) * RET_DV].astype(BF16)
        qb = qr.astype(BF16)
        state = s_ref[h]
        a = lax.dot_general(qb, kr.astype(BF16), NT_DIMS, preferred_element_type=F32) * dmat_ref[h]
        inner = jnp.dot(a.astype(BF16), vb, preferred_element_type=F32)
        cross = jnp.dot(qb, state.astype(BF16), preferred_element_type=F32) * cdec_ref[h]
        o = inner + cross
        kd_t = jnp.transpose(kr * kdec_ref[h]).astype(BF16)
        s_ref[h] = gc_ref[h] * state + jnp.dot(kd_t, vb, preferred_element_type=F32)
        ms = jnp.mean(o * o, axis=-1, keepdims=True)
        y = o * lax.rsqrt(ms + EPS) * retg_ref[:, h * RET_DV:(h + 1) * RET_DV]
        g = rg_ref[:, h * RET_DV:(h + 1) * RET_DV]
        a_ref[:, h * RET_DV:(h + 1) * RET_DV] = (y * (g * jax.nn.sigmoid(g))).astype(a_ref.dtype)

    u = jax.nn.gelu(gu_ref[...])
    v = jax.nn.gelu(gv_ref[...])
    ms = jnp.mean(v * v, axis=-1, keepdims=True)
    vn = v * lax.rsqrt(ms + EPS) * gmg_ref[...]
    vb = vn.astype(BF16)
    for g in range(GM_GROUPS):
        sl = slice(g * GM_GW, (g + 1) * GM_GW)
        mixed = jnp.dot(wtril_ref[g], vb[:, sl], preferred_element_type=F32) + bsm_ref[g]
        c_ref[:, sl] = (u[:, sl] * mixed).astype(c_ref.dtype)

    @pl.when(c == last)
    def _():
        sfin_ref[...] = s_ref[...]
        gmv_ref[...] = vn


def prompt_mixers(p_main, consts, retg_row, gmg_row, wtril, bsm):
    seq = p_main.shape[0]
    nc = seq // CHUNK
    cosf, sinf, dmat, cdec, kdec, gc = consts
    w512 = lambda blk: pl.BlockSpec((CHUNK, 512), lambda c: (c, blk))
    w1024 = lambda blk: pl.BlockSpec((CHUNK, 1024), lambda c: (c, blk))
    full = lambda shape: pl.BlockSpec(shape, lambda c: (0,) * len(shape))
    return pl.pallas_call(
        _mix_kernel,
        grid=(nc,),
        in_specs=[w512(0), w512(1), w1024(1), w1024(2), w512(_M_GU // 512), w512(_M_GV // 512),
                  pl.BlockSpec((CHUNK, RET_DK), lambda c: (c, 0)), pl.BlockSpec((CHUNK, RET_DK), lambda c: (c, 0)),
                  full(dmat.shape), full(cdec.shape), full(kdec.shape), full(gc.shape),
                  full(retg_row.shape), full(gmg_row.shape), full(wtril.shape), full(bsm.shape)],
        out_specs=[pl.BlockSpec((CHUNK, RET_V), lambda c: (c, 0)), pl.BlockSpec((CHUNK, GM_W), lambda c: (c, 0)),
                   full((RET_HEADS, RET_DK, RET_DV)), full((CHUNK, GM_W))],
        out_shape=[jax.ShapeDtypeStruct((seq, RET_V), BF16), jax.ShapeDtypeStruct((seq, GM_W), BF16),
                   jax.ShapeDtypeStruct((RET_HEADS, RET_DK, RET_DV), F32),
                   jax.ShapeDtypeStruct((CHUNK, GM_W), F32)],
        scratch_shapes=[pltpu.VMEM((RET_HEADS, RET_DK, RET_DV), F32)],
        compiler_params=_cparams(("arbitrary",)),
    )(p_main, p_main, p_main, p_main, p_main, p_main, cosf, sinf, dmat, cdec, kdec, gc, retg_row, gmg_row, wtril, bsm)


def _dsa_prompt_kernel(kix_ref, qit_ref, w_ref, k_ref, vt_ref, qt_ref, o_ref, keys_ref, wq_ref, acc_ref, *, topk):
    i = pl.program_id(0)
    grp = DSA_GROUP
    per_grp = grp // CHUNK
    ng = i // per_grp + 1

    def grp_rows(g):
        return pl.ds(pl.multiple_of(g * grp, grp), grp)

    for h in range(IDX_HEADS):
        wq_ref[:, h * CHUNK:(h + 1) * CHUNK] = qit_ref[h]
    w_row = jnp.concatenate([w_ref[h:h + 1, :] for h in range(IDX_HEADS)], axis=1)

    def group_keys(g):
        parts = []
        for sub in range(grp // DSA_SCORE_ROWS):
            r0 = pl.multiple_of(g * grp + sub * DSA_SCORE_ROWS, DSA_SCORE_ROWS)
            kt = kix_ref[pl.ds(r0, DSA_SCORE_ROWS), :]
            acc = jnp.zeros((DSA_SCORE_ROWS, CHUNK), F32)
            for hp in range(IDX_HEADS // 2):
                cs = slice(hp * 2 * CHUNK, (hp + 1) * 2 * CHUNK)
                r = jnp.dot(kt, wq_ref[:, cs], preferred_element_type=F32)
                r = jnp.maximum(r, 0.0) * w_row[:, cs]
                acc = acc + r[:, :CHUNK] + r[:, CHUNK:]
            parts.append(_sortable_key(acc))
        return jnp.concatenate(parts, axis=0)

    def full_group(g, carry):
        keys_ref[grp_rows(g), :] = group_keys(g)
        return carry

    lax.fori_loop(0, ng - 1, full_group, 0)
    key_pos = (ng - 1) * grp + lax.broadcasted_iota(I32, (grp, CHUNK), 0)
    q_pos = i * CHUNK + lax.broadcasted_iota(I32, (grp, CHUNK), 1)
    keys_ref[grp_rows(ng - 1), :] = jnp.where(key_pos <= q_pos, group_keys(ng - 1), INT_MIN)

    def count_ge(mid):
        def body(g, cnt):
            m = (keys_ref[grp_rows(g), :] >= mid).astype(I32)
            for t in range(per_grp):
                cnt = cnt + m[t * CHUNK:(t + 1) * CHUNK]
            return cnt

        cnt = lax.fori_loop(0, ng, body, jnp.zeros((CHUNK, CHUNK), I32))
        return jnp.sum(cnt, axis=0, keepdims=True)

    if grp % topk == 0:
        def class_max(g, cm):
            kg = keys_ref[grp_rows(g), :]
            for t in range(grp // topk):
                cm = jnp.maximum(cm, kg[t * topk:(t + 1) * topk])
            return cm

        cm = lax.fori_loop(0, ng, class_max, jnp.full((topk, CHUNK), INT_MIN, I32))
        lo0 = jnp.min(cm, axis=0, keepdims=True)
        hi0 = jnp.minimum(jnp.max(cm, axis=0, keepdims=True), INT_MAX - 1) + 1
    else:
        lo0 = jnp.full((1, CHUNK), INT_MIN, I32)
        hi0 = jnp.full((1, CHUNK), INT_MAX, I32)
    thr = jnp.maximum(_bisect_threshold(count_ge, lo0, hi0, topk), INT_MIN + 1)

    acc_ref[...] = jnp.zeros_like(acc_ref)

    def attend(g, carry):
        sel = keys_ref[grp_rows(g), :] >= thr
        out = []
        for h in range(ATT_HEADS):
            hs = slice(h * ATT_HD, (h + 1) * ATT_HD)
            m, l = carry[h]
            s = jnp.dot(k_ref[grp_rows(g), hs], qt_ref[hs, :], preferred_element_type=F32)
            s = jnp.where(sel, s, -jnp.inf)
            m_new = jnp.maximum(m, jnp.max(s, axis=0, keepdims=True))
            m_safe = jnp.where(m_new == -jnp.inf, 0.0, m_new)
            p = jnp.exp(s - m_safe)
            alpha = jnp.exp(m - m_safe)
            l = alpha * l + jnp.sum(p, axis=0, keepdims=True)
            acc_ref[h] = acc_ref[h] * alpha + jnp.dot(vt_ref[g, hs, :], p.astype(BF16), preferred_element_type=F32)
            out.append((m_new, l))
        return tuple(out)

    init = tuple((jnp.full((1, CHUNK), -jnp.inf, F32), jnp.zeros((1, CHUNK), F32)) for _ in range(ATT_HEADS))
    stats = lax.fori_loop(0, ng, attend, init)
    for h in range(ATT_HEADS):
        o_ref[h * ATT_HD:(h + 1) * ATT_HD, :] = acc_ref[h] / stats[h][1]


def dsa_prompt(aq, ak, av, iq, iw, ik):
    seq = aq.shape[0]
    assert seq % DSA_GROUP == 0
    nb = seq // CHUNK
    topk = min(TOPK_MAX, seq // 4)
    iq3 = iq.reshape(seq, IDX_HEADS, IDX_DIM)
    q_hi = iq3.astype(BF16)
    q_lo = (iq3 - q_hi.astype(F32)).astype(BF16)
    qx = jnp.concatenate([q_hi, q_lo, q_hi, q_lo], axis=-1)
    qit = qx.reshape(seq, IDX_HEADS * 4 * IDX_DIM).T.reshape(IDX_HEADS, 4 * IDX_DIM, seq)
    k_hi = ik.astype(BF16)
    k_lo = (ik - k_hi.astype(F32)).astype(BF16)
    kix = jnp.concatenate([k_hi, k_hi, k_lo, k_lo], axis=-1)
    w_t = (iw * (IDX_DIM ** -0.5 * IDX_HEADS ** -0.5)).T
    qt = (aq * ATT_HD ** -0.5).T.astype(BF16)
    kb = ak.astype(BF16)
    vt = av.reshape(seq // DSA_GROUP, DSA_GROUP, ATT_W).transpose(0, 2, 1).astype(BF16)
    resident = lambda shape: pl.BlockSpec(shape, lambda i: (0,) * len(shape))
    o_t = pl.pallas_call(
        functools.partial(_dsa_prompt_kernel, topk=topk),
        grid=(nb,),
        in_specs=[resident(kix.shape),
                  pl.BlockSpec((IDX_HEADS, 4 * IDX_DIM, CHUNK), lambda i: (0, 0, i)),
                  pl.BlockSpec((IDX_HEADS, CHUNK), lambda i: (0, i)),
                  resident(kb.shape), resident(vt.shape),
                  pl.BlockSpec((ATT_W, CHUNK), lambda i: (0, i))],
        out_specs=pl.BlockSpec((ATT_W, CHUNK), lambda i: (0, i)),
        out_shape=jax.ShapeDtypeStruct((ATT_W, seq), F32),
        scratch_shapes=[pltpu.VMEM((seq, CHUNK), I32), pltpu.VMEM((4 * IDX_DIM, IDX_HEADS * CHUNK), BF16),
                        pltpu.VMEM((ATT_HEADS, ATT_HD, CHUNK), F32)],
        compiler_params=_cparams(("arbitrary",), 56 * 1024 * 1024),
    )(kix, qit, w_t, kb, vt, qt)
    return o_t.T


def _pmem_kernel(hm_ref, wq_ref, mk_ref, mv_ref, wo_ref, x_ref, o_ref):
    q = jnp.dot(hm_ref[...], wq_ref[...], preferred_element_type=F32)
    outs = []
    for h in range(MEM_HEADS):
        hs = slice(h * MEM_HD, (h + 1) * MEM_HD)
        s = lax.dot_general(q[:, hs].astype(BF16), mk_ref[:, hs], NT_DIMS, preferred_element_type=F32) * MEM_HD ** -0.5
        e = jnp.exp(s - jnp.max(s, axis=-1, keepdims=True))
        oh = jnp.dot(e.astype(BF16), mv_ref[:, hs], preferred_element_type=F32) / jnp.sum(e, axis=-1, keepdims=True)
        outs.append(oh.astype(BF16))
    o = jnp.concatenate(outs, axis=1)
    o_ref[...] = x_ref[...] + jnp.dot(o, wo_ref[...], preferred_element_type=F32)


def prompt_mem_attend(x, hm, wq, mk, mv, wo, tm=512):
    m, d = x.shape
    n_mem = mk.shape[0]
    full = lambda shape: pl.BlockSpec(shape, lambda i: (0,) * len(shape))
    return pl.pallas_call(
        _pmem_kernel,
        grid=(m // tm,),
        in_specs=[pl.BlockSpec((tm, d), lambda i: (i, 0)), full((d, MEM_W)), full((n_mem, MEM_W)),
                  full((n_mem, MEM_W)), full((MEM_W, d)), pl.BlockSpec((tm, d), lambda i: (i, 0))],
        out_specs=pl.BlockSpec((tm, d), lambda i: (i, 0)),
        out_shape=jax.ShapeDtypeStruct((m, d), F32),
        compiler_params=_cparams(("parallel",), 48 * 1024 * 1024),
    )(hm, wq, mk, mv, wo, x)


def _conv_seq(a, i, j, part, carry_ref, cw_ref, cb_ref, row):
    tm = a.shape[0]
    prev = carry_ref[j, part]
    p1 = prev[7:8, :]
    p2 = prev[6:7, :]
    r1 = jnp.where(row == 0, p1, pltpu.roll(a, 1, axis=0))
    r2 = jnp.where(row == 0, p2, jnp.where(row == 1, p1, pltpu.roll(a, 2, axis=0)))
    carry_ref[j, part] = a[tm - 8:tm, :]
    return cb_ref[...] + cw_ref[0:1, :] * r2 + cw_ref[1:2, :] * r1 + cw_ref[2:3, :] * a


def _ffn_seq_kernel(hx_ref, wg_ref, wu_ref, cwg_ref, cwu_ref, cbg_ref, cbu_ref, wd_ref, x_ref,
                    o_ref, lg_ref, lu_ref, carry_ref):
    i = pl.program_id(0)
    j = pl.program_id(1)
    hx = hx_ref[...]
    tm = hx.shape[0]
    row = lax.broadcasted_iota(I32, (tm, 1), 0)

    @pl.when(i == 0)
    def _():
        carry_ref[j] = jnp.zeros(carry_ref.shape[1:], F32)

    ag = jnp.dot(hx, wg_ref[...], preferred_element_type=F32)
    au = jnp.dot(hx, wu_ref[...], preferred_element_type=F32)
    lg_ref[0] = ag[tm - 8:tm, :]
    lu_ref[0] = au[tm - 8:tm, :]
    cg = _conv_seq(ag, i, j, 0, carry_ref, cwg_ref, cbg_ref, row)
    cu = _conv_seq(au, i, j, 1, carry_ref, cwu_ref, cbu_ref, row)
    act = ((cg * jax.nn.sigmoid(cg)) * cu).astype(BF16)
    f = jnp.dot(act, wd_ref[...], preferred_element_type=F32)

    @pl.when(j == 0)
    def _():
        o_ref[...] = x_ref[...] + f

    @pl.when(j > 0)
    def _():
        o_ref[...] += f


def ffn_seq(x, hx, ffn_w, tm=512):
    wg, wu, cwg, cwu, cbg, cbu, wd = ffn_w
    m, d = x.shape
    tf = FFN_TF
    nf = D_FF_PAD // tf
    col = lambda r: pl.BlockSpec((r, tf), lambda i, j: (0, j))
    tail = pl.BlockSpec((1, 8, tf), lambda i, j: (i, 0, j))
    return pl.pallas_call(
        _ffn_seq_kernel,
        grid=(m // tm, nf),
        in_specs=[pl.BlockSpec((tm, d), lambda i, j: (i, 0)), col(d), col(d), col(CONV_W), col(CONV_W), col(1), col(1),
                  pl.BlockSpec((tf, d), lambda i, j: (j, 0)), pl.BlockSpec((tm, d), lambda i, j: (i, 0))],
        out_specs=[pl.BlockSpec((tm, d), lambda i, j: (i, 0)), tail, tail],
        out_shape=[jax.ShapeDtypeStruct((m, d), F32), jax.ShapeDtypeStruct((m // tm, 8, D_FF_PAD), F32),
                   jax.ShapeDtypeStruct((m // tm, 8, D_FF_PAD), F32)],
        scratch_shapes=[pltpu.VMEM((nf, 2, 8, tf), F32)],
        compiler_params=_cparams(("arbitrary", "arbitrary"), 48 * 1024 * 1024),
    )(hx, wg, wu, cwg, cwu, cbg, cbu, wd, x)


def _ffn_step_kernel(hx_ref, wg_ref, wu_ref, cwg_ref, cwu_ref, cbg_ref, cbu_ref, wd_ref, x_ref,
                     pg0_ref, pg1_ref, pu0_ref, pu1_ref, o_ref, ag_ref, au_ref):
    j = pl.program_id(0)
    hx = hx_ref[...]
    ag = jnp.dot(hx, wg_ref[...], preferred_element_type=F32)
    au = jnp.dot(hx, wu_ref[...], preferred_element_type=F32)
    ag_ref[...] = ag
    au_ref[...] = au
    cg = cbg_ref[...] + cwg_ref[0:1, :] * pg0_ref[...] + cwg_ref[1:2, :] * pg1_ref[...] + cwg_ref[2:3, :] * ag
    cu = cbu_ref[...] + cwu_ref[0:1, :] * pu0_ref[...] + cwu_ref[1:2, :] * pu1_ref[...] + cwu_ref[2:3, :] * au
    act = ((cg * jax.nn.sigmoid(cg)) * cu).astype(BF16)
    f = jnp.dot(act, wd_ref[...], preferred_element_type=F32)

    @pl.when(j == 0)
    def _():
        o_ref[...] = x_ref[...] + f

    @pl.when(j > 0)
    def _():
        o_ref[...] += f


def ffn_step(x, hx, ffn_w, prev):
    wg, wu, cwg, cwu, cbg, cbu, wd = ffn_w
    m, d = x.shape
    tf = FFN_TF
    nf = D_FF_PAD // tf
    col = lambda r: pl.BlockSpec((r, tf), lambda j: (0, j))
    row = pl.BlockSpec((m, d), lambda j: (0, 0))
    return pl.pallas_call(
        _ffn_step_kernel,
        grid=(nf,),
        in_specs=[row, col(d), col(d), col(CONV_W), col(CONV_W), col(1), col(1),
                  pl.BlockSpec((tf, d), lambda j: (j, 0)), row, col(m), col(m), col(m), col(m)],
        out_specs=[row, col(m), col(m)],
        out_shape=[jax.ShapeDtypeStruct((m, d), F32), jax.ShapeDtypeStruct((m, D_FF_PAD), F32),
                   jax.ShapeDtypeStruct((m, D_FF_PAD), F32)],
        compiler_params=_cparams(("arbitrary",), 48 * 1024 * 1024),
    )(hx, wg, wu, cwg, cwu, cbg, cbu, wd, x, *prev)


def _smix_kernel(qc_ref, kc_ref, v_ref, rg_ref, gu_ref, gv_ref, st_ref, cos_ref, sin_ref, g1_ref, retg_ref,
                 gmg_ref, wrow_ref, brow_ref, a_ref, c_ref, ns_ref, gmv_ref):
    cosc = cos_ref[...]
    sinc = sin_ref[...]
    half = RET_DK // 2

    def rot(x):
        x1, x2 = x[:half], x[half:]
        return jnp.concatenate([x1 * cosc - x2 * sinc, x1 * sinc + x2 * cosc], axis=0)

    for h in range(RET_HEADS):
        vs = slice(h * RET_DV, (h + 1) * RET_DV)
        qr = rot(qc_ref[0, h])
        kr = rot(kc_ref[0, h]) * RET_DK ** -0.5
        v = v_ref[0, :, vs]
        state = st_ref[0, h]
        g1 = g1_ref[h]
        inner = jnp.sum(qr * kr, axis=0, keepdims=True) * v
        cross = jnp.sum(qr * state, axis=0, keepdims=True) * g1
        o = inner + cross
        ns_ref[0, h] = g1 * state + kr * v
        ms = jnp.mean(o * o, axis=-1, keepdims=True)
        y = o * lax.rsqrt(ms + EPS) * retg_ref[:, vs]
        g = rg_ref[0, :, vs]
        a_ref[0, :, vs] = y * (g * jax.nn.sigmoid(g))

    u = jax.nn.gelu(gu_ref[0])
    v = jax.nn.gelu(gv_ref[0])
    ms = jnp.mean(v * v, axis=-1, keepdims=True)
    vn = v * lax.rsqrt(ms + EPS) * gmg_ref[...]
    c_ref[0] = u * (wrow_ref[...] * vn + brow_ref[...])
    gmv_ref[0] = vn


def sample_mixers(p_main, state, consts, retg_row, gmg_row, wrow, brow):
    nb = p_main.shape[0]
    cosc, sinc, g1 = consts
    qc = p_main[:, 0:RET_QK].reshape(nb, RET_HEADS, RET_DK, 1)
    kc = p_main[:, RET_QK:2 * RET_QK].reshape(nb, RET_HEADS, RET_DK, 1)
    p3 = p_main.reshape(nb, 1, N_MAIN)
    per_b = lambda width, blk: pl.BlockSpec((1, 1, width), lambda b: (b, 0, blk))
    col4 = pl.BlockSpec((1, RET_HEADS, RET_DK, 1), lambda b: (b, 0, 0, 0))
    st4 = pl.BlockSpec((1, RET_HEADS, RET_DK, RET_DV), lambda b: (b, 0, 0, 0))
    full = lambda shape: pl.BlockSpec(shape, lambda b: (0,) * len(shape))
    return pl.pallas_call(
        _smix_kernel,
        grid=(nb,),
        in_specs=[col4, col4, per_b(1024, 1), per_b(1024, 2), per_b(512, _M_GU // 512), per_b(512, _M_GV // 512), st4,
                  full(cosc.shape), full(sinc.shape), full(g1.shape), full(retg_row.shape), full(gmg_row.shape),
                  full(wrow.shape), full(brow.shape)],
        out_specs=[per_b(RET_V, 0), per_b(GM_W, 0), st4, per_b(GM_W, 0)],
        out_shape=[jax.ShapeDtypeStruct((nb, 1, RET_V), F32), jax.ShapeDtypeStruct((nb, 1, GM_W), F32),
                   jax.ShapeDtypeStruct(state.shape, F32), jax.ShapeDtypeStruct((nb, 1, GM_W), F32)],
        compiler_params=_cparams(("parallel",)),
    )(qc, kc, p3, p3, p3, p3, state, cosc, sinc, g1, retg_row, gmg_row, wrow, brow)


def _page_scores_kernel(pt_ref, qx_ref, w_ref, qbd_ref, *refs, n_pg, scale):
    ki_refs = refs[:n_pg]
    k_refs = refs[n_pg:2 * n_pg]
    isc_ref, asc_ref = refs[2 * n_pg:]
    qx = qx_ref[0]
    q_hi = qx[:IDX_HEADS]
    w = w_ref[0]
    qbd = qbd_ref[0]
    for n in range(n_pg):
        kp = ki_refs[n][0]
        k_hi = kp.astype(BF16)
        k_lo = (kp - k_hi.astype(F32)).astype(BF16)
        s2 = lax.dot_general(qx, k_hi, NT_DIMS, preferred_element_type=F32)
        s = s2[:IDX_HEADS] + s2[IDX_HEADS:] + lax.dot_general(q_hi, k_lo, NT_DIMS, preferred_element_type=F32)
        isc_ref[0, n] = jnp.sum(jnp.maximum(s, 0.0) * w, axis=0, keepdims=True)
        asc_ref[0, n] = lax.dot_general(qbd, k_refs[n][0].astype(BF16), NT_DIMS, preferred_element_type=F32) * scale


def page_scores(page_table, qx, w_col, qbd, pool_ki, pool_k, n_pg):
    nb, n_pages = page_table.shape
    groups = n_pages // n_pg
    pt = page_table.reshape(-1)

    def pool_spec(width, n):
        return pl.BlockSpec((1, PAGE_SIZE, width), lambda b, g, pt_ref: (pt_ref[b * n_pages + g * n_pg + n], 0, 0))

    per_b = lambda shape: pl.BlockSpec((1,) + shape, lambda b, g, pt_ref: (b,) + (0,) * len(shape))
    grid_spec = pltpu.PrefetchScalarGridSpec(
        num_scalar_prefetch=1,
        grid=(nb, groups),
        in_specs=[per_b((2 * IDX_HEADS, IDX_DIM)), per_b((IDX_HEADS, 1)), per_b((8, ATT_W))]
        + [pool_spec(IDX_DIM, n) for n in range(n_pg)] + [pool_spec(ATT_W, n) for n in range(n_pg)],
        out_specs=[pl.BlockSpec((1, n_pg, 1, PAGE_SIZE), lambda b, g, pt_ref: (b, g, 0, 0)),
                   pl.BlockSpec((1, n_pg, 8, PAGE_SIZE), lambda b, g, pt_ref: (b, g, 0, 0))],
    )
    return pl.pallas_call(
        functools.partial(_page_scores_kernel, n_pg=n_pg, scale=ATT_HD ** -0.5),
        grid_spec=grid_spec,
        out_shape=[jax.ShapeDtypeStruct((nb, n_pages, 1, PAGE_SIZE), F32),
                   jax.ShapeDtypeStruct((nb, n_pages, 8, PAGE_SIZE), F32)],
        compiler_params=_cparams(("arbitrary", "arbitrary")),
    )(pt, qx, w_col, qbd, *([pool_ki] * n_pg), *([pool_k] * n_pg))


def _select_softmax_kernel(isc_ref, asc_ref, p_ref, keys_ref, *, topk, n_valid):
    colid = lax.broadcasted_iota(I32, isc_ref.shape, 1)
    keys_ref[...] = jnp.where(colid < n_valid, _sortable_key(isc_ref[...]), INT_MIN)

    def count_ge(mid):
        return jnp.sum((keys_ref[...] >= mid).astype(I32), axis=1, keepdims=True)

    nb = isc_ref.shape[0]
    thr = _bisect_threshold(count_ge, jnp.full((nb, 1), INT_MIN, I32), jnp.full((nb, 1), INT_MAX, I32), topk)
    thr = jnp.maximum(thr, INT_MIN + 1)
    msk = keys_ref[...] >= thr
    for h in range(ATT_HEADS):
        s = jnp.where(msk, asc_ref[h], -jnp.inf)
        e = jnp.exp(s - jnp.max(s, axis=1, keepdims=True))
        p_ref[h] = e / jnp.sum(e, axis=1, keepdims=True)


def select_softmax(isc, asc, topk, n_valid):
    return pl.pallas_call(
        functools.partial(_select_softmax_kernel, topk=topk, n_valid=n_valid),
        out_shape=jax.ShapeDtypeStruct(asc.shape, F32),
        scratch_shapes=[pltpu.VMEM(isc.shape, I32)],
        compiler_params=_cparams(None, 48 * 1024 * 1024),
    )(isc, asc)


def _page_values_kernel(pt_ref, p_ref, *refs, n_pg):
    v_refs = refs[:n_pg]
    o_ref = refs[n_pg]
    acc = jnp.zeros(o_ref.shape[1:], F32)
    for n in range(n_pg):
        acc = acc + jnp.dot(p_ref[0, n], v_refs[n][0].astype(BF16), preferred_element_type=F32)

    @pl.when(pl.program_id(1) == 0)
    def _():
        o_ref[0] = acc

    @pl.when(pl.program_id(1) > 0)
    def _():
        o_ref[0] += acc


def page_values(page_table, p_pages, pool_v, n_pg):
    nb, n_pages = page_table.shape
    groups = n_pages // n_pg
    pt = page_table.reshape(-1)

    def pool_spec(n):
        return pl.BlockSpec((1, PAGE_SIZE, ATT_W), lambda b, g, pt_ref: (pt_ref[b * n_pages + g * n_pg + n], 0, 0))

    grid_spec = pltpu.PrefetchScalarGridSpec(
        num_scalar_prefetch=1,
        grid=(nb, groups),
        in_specs=[pl.BlockSpec((1, n_pg, 8, PAGE_SIZE), lambda b, g, pt_ref: (b, g, 0, 0))]
        + [pool_spec(n) for n in range(n_pg)],
        out_specs=pl.BlockSpec((1, 8, ATT_W), lambda b, g, pt_ref: (b, 0, 0)),
    )
    return pl.pallas_call(
        functools.partial(_page_values_kernel, n_pg=n_pg),
        grid_spec=grid_spec,
        out_shape=jax.ShapeDtypeStruct((nb, 8, ATT_W), F32),
        compiler_params=_cparams(("arbitrary", "arbitrary")),
    )(pt, p_pages, *([pool_v] * n_pg))


def _head_rows(q):
    nb = q.shape[0]
    q4 = q.reshape(nb, 1, ATT_HEADS, ATT_HD)
    eye = jnp.eye(8, ATT_HEADS, dtype=q.dtype).reshape(1, 8, ATT_HEADS, 1)
    return (q4 * eye).reshape(nb, 8, ATT_HEADS * ATT_HD).astype(BF16)


def _diag_heads(o):
    nb = o.shape[0]
    o4 = o[:, :ATT_HEADS].reshape(nb, ATT_HEADS, ATT_HEADS, ATT_HD)
    return jnp.stack([o4[:, h, h] for h in range(ATT_HEADS)], axis=1).reshape(nb, ATT_HEADS * ATT_HD)


def dsa_sample(aq, ak, av, iq, iw, ik, pool_k, pool_v, pool_ki, page_table, n_pg=8):
    nb, n_pages = page_table.shape
    past = n_pages * PAGE_SIZE
    topk = min(TOPK_MAX, (past + 1) // 4)
    iq3 = iq.reshape(nb, IDX_HEADS, IDX_DIM)
    q_hi = iq3.astype(BF16)
    q_lo = (iq3 - q_hi.astype(F32)).astype(BF16)
    qx = jnp.concatenate([q_hi, q_lo], axis=1)
    w_col = (iw * (IDX_DIM ** -0.5 * IDX_HEADS ** -0.5)).reshape(nb, IDX_HEADS, 1)
    qbd = _head_rows(aq)
    own = jnp.arange(nb, dtype=I32).reshape(nb, 1)
    pad_page = lambda a: jnp.pad(a[:, None, :], ((0, 0), (0, PAGE_SIZE - 1), (0, 0)))
    isc_p, asc_p = page_scores(page_table, qx, w_col, qbd, pool_ki, pool_k, n_pg)
    isc_n, asc_n = page_scores(own, qx, w_col, qbd, pad_page(ik), pad_page(ak), 1)
    isc = jnp.concatenate([isc_p, isc_n], axis=1).reshape(nb, past + PAGE_SIZE)
    asc = jnp.concatenate([asc_p, asc_n], axis=1)[:, :, :ATT_HEADS]
    asc = asc.transpose(2, 0, 1, 3).reshape(ATT_HEADS, nb, past + PAGE_SIZE)
    p = select_softmax(isc, asc, topk, past + 1)
    p = p.reshape(ATT_HEADS, nb, n_pages + 1, PAGE_SIZE).transpose(1, 2, 0, 3)
    p = jnp.pad(p, ((0, 0), (0, 0), (0, 8 - ATT_HEADS), (0, 0))).astype(BF16)
    o_p = page_values(page_table, p[:, :n_pages], pool_v, n_pg)
    o_n = page_values(own, p[:, n_pages:], pad_page(av), 1)
    return _diag_heads(o_p + o_n)


def _smem_kernel(qbd_ref, mk_ref, mv_ref, o_ref):
    for b in range(qbd_ref.shape[0]):
        s = lax.dot_general(qbd_ref[b], mk_ref[b].astype(BF16), NT_DIMS, preferred_element_type=F32) * MEM_HD ** -0.5
        e = jnp.exp(s - jnp.max(s, axis=-1, keepdims=True))
        p = e / jnp.sum(e, axis=-1, keepdims=True)
        o_ref[b] = jnp.dot(p.astype(BF16), mv_ref[b].astype(BF16), preferred_element_type=F32)


def sample_mem_core(q, mem_k, mem_v, bb=8):
    nb, n_mem = mem_k.shape[:2]
    qbd = _head_rows(q)
    mk = mem_k.reshape(nb, n_mem, MEM_W)
    mv = mem_v.reshape(nb, n_mem, MEM_W)
    o = pl.pallas_call(
        _smem_kernel,
        grid=(nb // bb,),
        in_specs=[pl.BlockSpec((bb, 8, MEM_W), lambda i: (i, 0, 0)), pl.BlockSpec((bb, n_mem, MEM_W), lambda i: (i, 0, 0)),
                  pl.BlockSpec((bb, n_mem, MEM_W), lambda i: (i, 0, 0))],
        out_specs=pl.BlockSpec((bb, 8, MEM_W), lambda i: (i, 0, 0)),
        out_shape=jax.ShapeDtypeStruct((nb, 8, MEM_W), F32),
        compiler_params=_cparams(("parallel",), 48 * 1024 * 1024),
    )(qbd, mk, mv)
    return _diag_heads(o)


def _prep_layer(w_in, w_out, w_mq, w_mk, w_mv, w_mo, w_up, conv_w, conv_b, w_down, gm_ws, gm_bs):
    w_main = jnp.concatenate([w_in[:, :_O_IQ_END], w_in[:, _O_GU:]], axis=1).astype(BF16)
    w_idx = jnp.concatenate([w_in[:, _O_IK:_O_GU], w_in[:, _O_IW:_O_IK],
                             jnp.zeros((D_MODEL, LANE - IDX_DIM - IDX_HEADS), w_in.dtype)], axis=1).astype(BF16)
    padc = lambda a: jnp.pad(a, ((0, 0), (0, D_FF_PAD - D_FF)))
    ffn_w = (padc(w_up[:, :D_FF]).astype(BF16), padc(w_up[:, D_FF:]).astype(BF16),
             padc(conv_w[:, :D_FF]), padc(conv_w[:, D_FF:]),
             padc(conv_b[None, :D_FF]), padc(conv_b[None, D_FF:]),
             jnp.pad(w_down, ((0, D_FF_PAD - D_FF), (0, 0))).astype(BF16))
    causal = jnp.tril(jnp.ones((CHUNK, CHUNK), dtype=bool))
    wtril = jnp.where(causal[None], gm_ws, 0.0).astype(BF16)
    bsm = jnp.broadcast_to(gm_bs[:, :, None], (GM_GROUPS, CHUNK, GM_GW))
    wrow = jnp.repeat(gm_ws[:, 0, 0], GM_GW).reshape(1, GM_W)
    brow = jnp.repeat(gm_bs[:, 0], GM_GW).reshape(1, GM_W)
    return dict(w_main=w_main, w_idx=w_idx, w_out=w_out.astype(BF16), w_mq=w_mq.astype(BF16),
                w_mk=w_mk.astype(BF16), w_mv=w_mv.astype(BF16), w_mo=w_mo.astype(BF16), ffn_w=ffn_w,
                wtril=wtril, bsm=bsm, wrow=wrow, brow=brow)


def _retention_consts(seq, past):
    log_g = jnp.log(1.0 - 2.0 ** (-5.0 - jnp.arange(RET_HEADS, dtype=F32)))
    half = RET_DK // 2
    inv = ROPE_BASE ** (-jnp.arange(half, dtype=F32) / half)
    ang = jnp.arange(seq).astype(F32)[:, None] * inv[None, :]
    cos, sin = jnp.cos(ang), jnp.sin(ang)
    cosf = jnp.concatenate([cos, cos], axis=1)
    sinf = jnp.concatenate([-sin, sin], axis=1)
    n = jnp.arange(CHUNK, dtype=F32)
    diff = n[:, None] - n[None, :]
    causal = diff >= 0
    dmat = jnp.where(causal[None], jnp.exp(log_g[:, None, None] * jnp.where(causal, diff, 0.0)[None]), 0.0)
    cdec = jnp.broadcast_to(jnp.exp(log_g[:, None] * (n[None, :] + 1.0))[:, :, None], (RET_HEADS, CHUNK, RET_DV))
    kdec = jnp.broadcast_to(jnp.exp(log_g[:, None] * (CHUNK - 1.0 - n[None, :]))[:, :, None], (RET_HEADS, CHUNK, RET_DK))
    gc = jnp.broadcast_to(jnp.exp(log_g * CHUNK)[:, None, None], (RET_HEADS, RET_DK, RET_DV))
    ang_s = jnp.full((1,), past, dtype=F32)[:, None] * inv[None, :]
    cosc = jnp.cos(ang_s).reshape(half, 1)
    sinc = jnp.sin(ang_s).reshape(half, 1)
    g1 = jnp.broadcast_to(jnp.exp(log_g * 1.0)[:, None, None], (RET_HEADS, 1, RET_DV))
    return (cosf, sinf, dmat, cdec, kdec, gc), (cosc, sinc, g1)


def _pad_ff(a):
    return jnp.pad(a, ((0, 0), (0, D_FF_PAD - D_FF)))


def kernel(x_prompt, x_sample, mem_prompt, cache_k, cache_v, cache_kidx, page_table, cache_mem_k, cache_mem_v, state_ret, state_conv, norm_mix_g, w_in, ret_norm_g, gm_norm_g, gm_ws, gm_bs, w_out, norm_mem_g, mem_in_g, w_mq, w_mk, w_mv, w_mo, norm_ffn_g, w_up, conv_w, conv_b, w_down, final_norm_g):
    depth = w_in.shape[0]
    seq = x_prompt.shape[1]
    nbd = x_sample.shape[0]
    n_mem = mem_prompt.shape[1]
    past = page_table.shape[1] * PAGE_SIZE
    p_consts, s_consts = _retention_consts(seq, past)
    n_pool = cache_k.shape[1]
    pool_k = cache_k.reshape(depth * n_pool, PAGE_SIZE, ATT_W)
    pool_v = cache_v.reshape(depth * n_pool, PAGE_SIZE, ATT_W)
    pool_ki = cache_kidx.reshape(depth * n_pool, PAGE_SIZE, IDX_DIM)

    xp = x_prompt.reshape(seq, D_MODEL)
    xs = x_sample.reshape(nbd, D_MODEL)
    mem = mem_prompt.reshape(n_mem, D_MODEL)
    outs = {k: [] for k in ("kp", "vp", "kip", "ks", "vs", "kis", "mkp", "mvp", "rsp", "rss", "csp", "css", "gvp", "gvs")}

    for l in range(depth):
        w = _prep_layer(w_in[l], w_out[l], w_mq[l], w_mk[l], w_mv[l], w_mo[l], w_up[l], conv_w[l], conv_b[l],
                        w_down[l], gm_ws[l], gm_bs[l])
        retg_row = ret_norm_g[l].reshape(1, RET_V)
        gmg_row = gm_norm_g[l].reshape(1, GM_W)

        h1 = rmsnorm_rows(xp, norm_mix_g[l], BF16)
        pm = matmul(h1, w["w_main"])
        pi = matmul(h1, w["w_idx"])
        ak, av = pm[:, _M_AK:_M_AV], pm[:, _M_AV:_M_IQ]
        ik = pi[:, :IDX_DIM]
        a_out, c_out, r_state, gm_v = prompt_mixers(pm, p_consts, retg_row, gmg_row, w["wtril"], w["bsm"])
        b_out = dsa_prompt(pm[:, _M_AQ:_M_AK], ak, av, pm[:, _M_IQ:_M_GU], pi[:, IDX_DIM:IDX_DIM + IDX_HEADS], ik)
        mixed = jnp.concatenate([a_out, b_out.astype(BF16), c_out], axis=1)
        xp = matmul(mixed, w["w_out"], res=xp)
        m_in = rmsnorm_rows(mem, mem_in_g[l], BF16)
        mk = matmul(m_in, w["w_mk"])
        mv = matmul(m_in, w["w_mv"])
        hm = rmsnorm_rows(xp, norm_mem_g[l], BF16)
        xp = prompt_mem_attend(xp, hm, w["w_mq"], mk.astype(BF16), mv.astype(BF16), w["w_mo"])
        hf = rmsnorm_rows(xp, norm_ffn_g[l], BF16)
        xp, tail_g, tail_u = ffn_seq(xp, hf, w["ffn_w"])
        last_g, last_u = tail_g[-1], tail_u[-1]
        outs["kp"].append(ak.reshape(1, seq, ATT_HEADS, ATT_HD))
        outs["vp"].append(av.reshape(1, seq, ATT_HEADS, ATT_HD))
        outs["kip"].append(ik.reshape(1, seq, IDX_DIM))
        outs["mkp"].append(mk.reshape(1, n_mem, MEM_HEADS, MEM_HD))
        outs["mvp"].append(mv.reshape(1, n_mem, MEM_HEADS, MEM_HD))
        outs["rsp"].append(r_state.reshape(1, RET_HEADS, RET_DK, RET_DV))
        outs["csp"].append(jnp.concatenate([last_g[8 - (CONV_W - 1):, :D_FF], last_u[8 - (CONV_W - 1):, :D_FF]], axis=1)[None])
        outs["gvp"].append(gm_v[None])

        h1 = rmsnorm_rows(xs, norm_mix_g[l], BF16)
        pm = matmul(h1, w["w_main"])
        pi = matmul(h1, w["w_idx"])
        ak, av = pm[:, _M_AK:_M_AV], pm[:, _M_AV:_M_IQ]
        ik = pi[:, :IDX_DIM]
        a_out, c_out, r_state, gm_v = sample_mixers(pm, state_ret[l], s_consts, retg_row, gmg_row, w["wrow"], w["brow"])
        b_out = dsa_sample(pm[:, _M_AQ:_M_AK], ak, av, pm[:, _M_IQ:_M_GU], pi[:, IDX_DIM:IDX_DIM + IDX_HEADS], ik,
                           pool_k, pool_v, pool_ki, page_table + l * n_pool)
        mixed = jnp.concatenate([a_out.reshape(nbd, RET_V), b_out, c_out.reshape(nbd, GM_W)], axis=1).astype(BF16)
        xs = matmul(mixed, w["w_out"], res=xs)
        hm = rmsnorm_rows(xs, norm_mem_g[l], BF16)
        q = matmul(hm, w["w_mq"])
        o = sample_mem_core(q, cache_mem_k[l], cache_mem_v[l])
        xs = matmul(o.astype(BF16), w["w_mo"], res=xs)
        hf = rmsnorm_rows(xs, norm_ffn_g[l], BF16)
        sc = state_conv[l]
        prev = (_pad_ff(sc[:, 0, :D_FF]), _pad_ff(sc[:, 1, :D_FF]), _pad_ff(sc[:, 0, D_FF:]), _pad_ff(sc[:, 1, D_FF:]))
        xs, a_g, a_u = ffn_step(xs, hf, w["ffn_w"], prev)
        a_new = jnp.concatenate([a_g[:, :D_FF], a_u[:, :D_FF]], axis=1)
        outs["ks"].append(ak.reshape(nbd, 1, ATT_HEADS, ATT_HD))
        outs["vs"].append(av.reshape(nbd, 1, ATT_HEADS, ATT_HD))
        outs["kis"].append(ik.reshape(nbd, 1, IDX_DIM))
        outs["rss"].append(r_state)
        outs["css"].append(jnp.stack([sc[:, 1, :], a_new], axis=1))
        outs["gvs"].append(gm_v)

    y_prompt = rmsnorm_rows(xp, final_norm_g, F32).reshape(1, seq, D_MODEL)
    y_sample = rmsnorm_rows(xs, final_norm_g, F32).reshape(nbd, 1, D_MODEL)
    st = lambda k: jnp.stack(outs[k])
    return (y_prompt, y_sample, st("kp"), st("vp"), st("kip"), st("ks"), st("vs"), st("kis"), st("mkp"), st("mvp"),
            st("rsp"), st("rss"), st("csp"), st("css"), st("gvp"), st("gvs"))
```

```python
import functools

import jax
import jax.numpy as jnp
from jax import lax
from jax.experimental import pallas as pl
from jax.experimental.pallas import tpu as pltpu

F32 = jnp.float32
BF16 = jnp.bfloat16
I32 = jnp.int32

D_MODEL = 2048
PAGE_SIZE = 128
RET_HEADS = 4
RET_DK = 128
RET_DV = 256
CHUNK = 128
ROPE_BASE = 10000.0
ATT_HEADS = 4
ATT_HD = 128
IDX_HEADS = 16
IDX_DIM = 64
TOPK_MAX = 256
GM_GROUPS = 4
GM_GW = 128
MEM_HEADS = 4
MEM_HD = 128
D_FF = 5504
CONV_W = 3
EPS = 1e-6

RET_QK = RET_HEADS * RET_DK
RET_V = RET_HEADS * RET_DV
ATT_W = ATT_HEADS * ATT_HD
IDX_Q = IDX_HEADS * IDX_DIM
GM_W = GM_GROUPS * GM_GW
MEM_W = MEM_HEADS * MEM_HD
PAGE_ROWS = PAGE_SIZE * ATT_HEADS

_O_IQ_END = 2 * RET_QK + 2 * RET_V + 3 * ATT_W + IDX_Q
_O_IW = _O_IQ_END
_O_IK = _O_IW + IDX_HEADS
_O_GU = _O_IK + IDX_DIM
_M_AQ = 2 * RET_QK + 2 * RET_V
_M_AK = _M_AQ + ATT_W
_M_AV = _M_AK + ATT_W
_M_IQ = _M_AV + ATT_W
_M_GU = _M_IQ + IDX_Q
_M_GV = _M_GU + GM_W
N_MAIN = _M_GV + GM_W
_M_IK = N_MAIN
_M_IW = _M_IK + IDX_DIM
N_PROJ = N_MAIN + 512
LANE = 128
D_FF_PAD = 5632
FFN_TF = 512
FFN_SUB = 512
DSA_GROUP = 512
DSA_SCORE_ROWS = 256
BISECT_STEPS_PER_CHECK = 4

INT_MIN = -2 ** 31
INT_MAX = 2 ** 31 - 1
V7X_VMEM_BYTES = 64 * 1024 * 1024
NT_DIMS = (((1,), (1,)), ((), ()))


def _cparams(semantics, vmem_bytes=None):
    return pltpu.CompilerParams(dimension_semantics=semantics, vmem_limit_bytes=vmem_bytes)


def _sortable_key(x):
    b = lax.bitcast_convert_type(x, I32)
    return b ^ ((b >> 31) & INT_MAX)


def _bisect_threshold(count_ge, lo, hi, topk):
    def cond(st):
        it, active, _, _ = st
        return jnp.logical_and(it < 32, active > 0)

    def body(st):
        it, _, lo, hi = st
        for _ in range(BISECT_STEPS_PER_CHECK):
            mid = (lo >> 1) + (hi >> 1) + (lo & hi & 1)
            cnt = count_ge(mid)
            ge = cnt >= topk
            lo = jnp.where(ge, mid, lo)
            hi = jnp.where(cnt == topk, mid + 1, jnp.where(ge, hi, mid))
        active = jnp.max((hi - 1 > lo).astype(I32))
        return it + BISECT_STEPS_PER_CHECK, active, lo, hi

    _, _, lo, _ = lax.while_loop(cond, body, (jnp.int32(0), jnp.int32(1), lo, hi))
    return lo


def _rmsnorm_bf16(x, g):
    ms = jnp.mean(x * x, axis=-1, keepdims=True)
    return (x * lax.rsqrt(ms + EPS) * g).astype(BF16)


def _norm_kernel(x_ref, g_ref, o_ref):
    x = x_ref[...]
    ms = jnp.mean(x * x, axis=-1, keepdims=True)
    o_ref[...] = (x * lax.rsqrt(ms + EPS) * g_ref[...]).astype(o_ref.dtype)


def rmsnorm_rows(x, g, out_dtype):
    m, d = x.shape
    tm = min(m, 256)
    return pl.pallas_call(
        _norm_kernel,
        grid=(m // tm,),
        in_specs=[pl.BlockSpec((tm, d), lambda i: (i, 0)), pl.BlockSpec((1, d), lambda i: (0, 0))],
        out_specs=pl.BlockSpec((tm, d), lambda i: (i, 0)),
        out_shape=jax.ShapeDtypeStruct((m, d), out_dtype),
        compiler_params=_cparams(("parallel",)),
    )(x, g.reshape(1, d))


def _mm_kernel(x_ref, w_ref, o_ref):
    o_ref[...] = jnp.dot(x_ref[...], w_ref[...], preferred_element_type=F32)


def _mm_res_kernel(x_ref, w_ref, r_ref, o_ref):
    o_ref[...] = r_ref[...] + jnp.dot(x_ref[...], w_ref[...], preferred_element_type=F32)


def matmul(x, w, res=None, tm=1024, tn=512):
    m, k = x.shape
    n = w.shape[1]
    tm = min(tm, m)
    tn = min(tn, n)
    assert m % tm == 0 and n % tn == 0
    in_specs = [pl.BlockSpec((tm, k), lambda i, j: (i, 0)), pl.BlockSpec((k, tn), lambda i, j: (0, j))]
    args = [x, w]
    body = _mm_kernel
    if res is not None:
        in_specs.append(pl.BlockSpec((tm, tn), lambda i, j: (i, j)))
        args.append(res)
        body = _mm_res_kernel
    return pl.pallas_call(
        body,
        grid=(m // tm, n // tn),
        in_specs=in_specs,
        out_specs=pl.BlockSpec((tm, tn), lambda i, j: (i, j)),
        out_shape=jax.ShapeDtypeStruct((m, n), F32),
        compiler_params=_cparams(("parallel", "parallel"), 48 * 1024 * 1024),
    )(*args)


def _norm_mm_kernel(x_ref, g_ref, w_ref, o_ref, h_ref):
    @pl.when(pl.program_id(1) == 0)
    def _():
        h_ref[...] = _rmsnorm_bf16(x_ref[...], g_ref[...])

    o_ref[...] = jnp.dot(h_ref[...], w_ref[...], preferred_element_type=F32)


def norm_matmul(x, g, w, tm=1024, tn=512):
    m, k = x.shape
    n = w.shape[1]
    tm = min(tm, m)
    tn = min(tn, n)
    assert m % tm == 0 and n % tn == 0
    return pl.pallas_call(
        _norm_mm_kernel,
        grid=(m // tm, n // tn),
        in_specs=[pl.BlockSpec((tm, k), lambda i, j: (i, 0)), pl.BlockSpec((1, k), lambda i, j: (0, 0)),
                  pl.BlockSpec((k, tn), lambda i, j: (0, j))],
        out_specs=pl.BlockSpec((tm, tn), lambda i, j: (i, j)),
        out_shape=jax.ShapeDtypeStruct((m, n), F32),
        scratch_shapes=[pltpu.VMEM((tm, k), BF16)],
        compiler_params=_cparams(("arbitrary", "arbitrary"), 48 * 1024 * 1024),
    )(x, g.reshape(1, k), w)


def _mix_kernel(rq_ref, rk_ref, rv_ref, rg_ref, gu_ref, gv_ref, cos_ref, sin_ref, dmat_ref, cdec_ref, kdec_ref,
                gc_ref, retg_ref, gmg_ref, wtril_ref, bsm_ref, a_ref, c_ref, sfin_ref, gmv_ref, s_ref):
    c = pl.program_id(0)
    last = pl.num_programs(0) - 1

    @pl.when(c == 0)
    def _():
        s_ref[...] = jnp.zeros_like(s_ref)

    cosf = cos_ref[...]
    sinf = sin_ref[...]
    for h in range(RET_HEADS):
        q = rq_ref[:, h * RET_DK:(h + 1) * RET_DK]
        k = rk_ref[:, h * RET_DK:(h + 1) * RET_DK]
        qr = q * cosf + pltpu.roll(q, RET_DK // 2, axis=1) * sinf
        kr = (k * cosf + pltpu.roll(k, RET_DK // 2, axis=1) * sinf) * RET_DK ** -0.5
        vb = rv_ref[:, h * RET_DV:(h + 1) * RET_DV].astype(BF16)
        qb = qr.astype(BF16)
        state = s_ref[h]
        a = lax.dot_general(qb, kr.astype(BF16), NT_DIMS, preferred_element_type=F32) * dmat_ref[h]
        inner = jnp.dot(a.astype(BF16), vb, preferred_element_type=F32)
        cross = jnp.dot(qb, state.astype(BF16), preferred_element_type=F32) * cdec_ref[h]
        o = inner + cross
        kd_t = jnp.transpose(kr * kdec_ref[h]).astype(BF16)
        s_ref[h] = gc_ref[h] * state + jnp.dot(kd_t, vb, preferred_element_type=F32)
        ms = jnp.mean(o * o, axis=-1, keepdims=True)
        y = o * lax.rsqrt(ms + EPS) * retg_ref[:, h * RET_DV:(h + 1) * RET_DV]
        g = rg_ref[:, h * RET_DV:(h + 1) * RET_DV]
        a_ref[:, h * RET_DV:(h + 1) * RET_DV] = (y * (g * jax.nn.sigmoid(g))).astype(a_ref.dtype)

    u = jax.nn.gelu(gu_ref[...])
    v = jax.nn.gelu(gv_ref[...])
    ms = jnp.mean(v * v, axis=-1, keepdims=True)
    vn = v * lax.rsqrt(ms + EPS) * gmg_ref[...]
    vb = vn.astype(BF16)
    for g in range(GM_GROUPS):
        sl = slice(g * GM_GW, (g + 1) * GM_GW)
        mixed = jnp.dot(wtril_ref[g], vb[:, sl], preferred_element_type=F32) + bsm_ref[g]
        c_ref[:, sl] = (u[:, sl] * mixed).astype(c_ref.dtype)

    @pl.when(c == last)
    def _():
        sfin_ref[...] = s_ref[...]
        gmv_ref[...] = vn


def prompt_mixers(p_main, consts, retg_row, gmg_row, wtril, bsm):
    seq = p_main.shape[0]
    nc = seq // CHUNK
    cosf, sinf, dmat, cdec, kdec, gc = consts
    w512 = lambda blk: pl.BlockSpec((CHUNK, 512), lambda c: (c, blk))
    w1024 = lambda blk: pl.BlockSpec((CHUNK, 1024), lambda c: (c, blk))
    full = lambda shape: pl.BlockSpec(shape, lambda c: (0,) * len(shape))
    return pl.pallas_call(
        _mix_kernel,
        grid=(nc,),
        in_specs=[w512(0), w512(1), w1024(1), w1024(2), w512(_M_GU // 512), w512(_M_GV // 512),
                  pl.BlockSpec((CHUNK, RET_DK), lambda c: (c, 0)), pl.BlockSpec((CHUNK, RET_DK), lambda c: (c, 0)),
                  full(dmat.shape), full(cdec.shape), full(kdec.shape), full(gc.shape),
                  full(retg_row.shape), full(gmg_row.shape), full(wtril.shape), full(bsm.shape)],
        out_specs=[pl.BlockSpec((CHUNK, RET_V), lambda c: (c, 0)), pl.BlockSpec((CHUNK, GM_W), lambda c: (c, 0)),
                   full((RET_HEADS, RET_DK, RET_DV)), full((CHUNK, GM_W))],
        out_shape=[jax.ShapeDtypeStruct((seq, RET_V), BF16), jax.ShapeDtypeStruct((seq, GM_W), BF16),
                   jax.ShapeDtypeStruct((RET_HEADS, RET_DK, RET_DV), F32),
                   jax.ShapeDtypeStruct((CHUNK, GM_W), F32)],
        scratch_shapes=[pltpu.VMEM((RET_HEADS, RET_DK, RET_DV), F32)],
        compiler_params=_cparams(("arbitrary",)),
    )(p_main, p_main, p_main, p_main, p_main, p_main, cosf, sinf, dmat, cdec, kdec, gc, retg_row, gmg_row, wtril, bsm)


def _dsa_prompt_kernel(kix_ref, qit_ref, w_ref, k_ref, vt_ref, qt_ref, o_ref, keys_ref, wq_ref, acc_ref, *, topk):
    i = pl.program_id(0)
    grp = DSA_GROUP
    per_grp = grp // CHUNK
    ng = i // per_grp + 1

    def grp_rows(g):
        return pl.ds(pl.multiple_of(g * grp, grp), grp)

    for h in range(IDX_HEADS):
        wq_ref[:, h * CHUNK:(h + 1) * CHUNK] = qit_ref[h]
    w_row = jnp.concatenate([w_ref[h:h + 1, :] for h in range(IDX_HEADS)], axis=1)

    def group_keys(g):
        parts = []
        for sub in range(grp // DSA_SCORE_ROWS):
            r0 = pl.multiple_of(g * grp + sub * DSA_SCORE_ROWS, DSA_SCORE_ROWS)
            kt = kix_ref[pl.ds(r0, DSA_SCORE_ROWS), :]
            acc = jnp.zeros((DSA_SCORE_ROWS, CHUNK), F32)
            for hp in range(IDX_HEADS // 2):
                cs = slice(hp * 2 * CHUNK, (hp + 1) * 2 * CHUNK)
                r = jnp.dot(kt, wq_ref[:, cs], preferred_element_type=F32)
                r = jnp.maximum(r, 0.0) * w_row[:, cs]
                acc = acc + r[:, :CHUNK] + r[:, CHUNK:]
            parts.append(_sortable_key(acc))
        return jnp.concatenate(parts, axis=0)

    def full_group(g, carry):
        keys_ref[grp_rows(g), :] = group_keys(g)
        return carry

    lax.fori_loop(0, ng - 1, full_group, 0)
    key_pos = (ng - 1) * grp + lax.broadcasted_iota(I32, (grp, CHUNK), 0)
    q_pos = i * CHUNK + lax.broadcasted_iota(I32, (grp, CHUNK), 1)
    keys_ref[grp_rows(ng - 1), :] = jnp.where(key_pos <= q_pos, group_keys(ng - 1), INT_MIN)

    def count_ge(mid):
        def body(g, cnt):
            m = (keys_ref[grp_rows(g), :] >= mid).astype(I32)
            for t in range(per_grp):
                cnt = cnt + m[t * CHUNK:(t + 1) * CHUNK]
            return cnt

        cnt = lax.fori_loop(0, ng, body, jnp.zeros((CHUNK, CHUNK), I32))
        return jnp.sum(cnt, axis=0, keepdims=True)

    if grp % topk == 0:
        def class_max(g, cm):
            kg = keys_ref[grp_rows(g), :]
            for t in range(grp // topk):
                cm = jnp.maximum(cm, kg[t * topk:(t + 1) * topk])
            return cm

        cm = lax.fori_loop(0, ng, class_max, jnp.full((topk, CHUNK), INT_MIN, I32))
        lo0 = jnp.min(cm, axis=0, keepdims=True)
        hi0 = jnp.minimum(jnp.max(cm, axis=0, keepdims=True), INT_MAX - 1) + 1
    else:
        lo0 = jnp.full((1, CHUNK), INT_MIN, I32)
        hi0 = jnp.full((1, CHUNK), INT_MAX, I32)
    thr = jnp.maximum(_bisect_threshold(count_ge, lo0, hi0, topk), INT_MIN + 1)

    acc_ref[...] = jnp.zeros_like(acc_ref)

    def attend(g, carry):
        sel = keys_ref[grp_rows(g), :] >= thr
        out = []
        for h in range(ATT_HEADS):
            hs = slice(h * ATT_HD, (h + 1) * ATT_HD)
            m, l = carry[h]
            s = jnp.dot(k_ref[grp_rows(g), hs], qt_ref[hs, :], preferred_element_type=F32)
            s = jnp.where(sel, s, -jnp.inf)
            m_new = jnp.maximum(m, jnp.max(s, axis=0, keepdims=True))
            m_safe = jnp.where(m_new == -jnp.inf, 0.0, m_new)
            p = jnp.exp(s - m_safe)
            alpha = jnp.exp(m - m_safe)
            l = alpha * l + jnp.sum(p, axis=0, keepdims=True)
            acc_ref[h] = acc_ref[h] * alpha + jnp.dot(vt_ref[g, hs, :], p.astype(BF16), preferred_element_type=F32)
            out.append((m_new, l))
        return tuple(out)

    init = tuple((jnp.full((1, CHUNK), -jnp.inf, F32), jnp.zeros((1, CHUNK), F32)) for _ in range(ATT_HEADS))
    stats = lax.fori_loop(0, ng, attend, init)
    for h in range(ATT_HEADS):
        o_ref[h * ATT_HD:(h + 1) * ATT_HD, :] = acc_ref[h] / stats[h][1]


def dsa_prompt(aq, ak, av, iq, iw, ik):
    seq = aq.shape[0]
    assert seq % DSA_GROUP == 0
    nb = seq // CHUNK
    topk = min(TOPK_MAX, seq // 4)
    iq3 = iq.reshape(seq, IDX_HEADS, IDX_DIM)
    q_hi = iq3.astype(BF16)
    q_lo = (iq3 - q_hi.astype(F32)).astype(BF16)
    qx = jnp.concatenate([q_hi, q_lo, q_hi, q_lo], axis=-1)
    qit = qx.reshape(seq, IDX_HEADS * 4 * IDX_DIM).T.reshape(IDX_HEADS, 4 * IDX_DIM, seq)
    k_hi = ik.astype(BF16)
    k_lo = (ik - k_hi.astype(F32)).astype(BF16)
    kix = jnp.concatenate([k_hi, k_hi, k_lo, k_lo], axis=-1)
    w_t = (iw * (IDX_DIM ** -0.5 * IDX_HEADS ** -0.5)).T
    qt = (aq * ATT_HD ** -0.5).T.astype(BF16)
    kb = ak.astype(BF16)
    vt = av.reshape(seq // DSA_GROUP, DSA_GROUP, ATT_W).transpose(0, 2, 1).astype(BF16)
    resident = lambda shape: pl.BlockSpec(shape, lambda i: (0,) * len(shape))
    o_t = pl.pallas_call(
        functools.partial(_dsa_prompt_kernel, topk=topk),
        grid=(nb,),
        in_specs=[resident(kix.shape),
                  pl.BlockSpec((IDX_HEADS, 4 * IDX_DIM, CHUNK), lambda i: (0, 0, i)),
                  pl.BlockSpec((IDX_HEADS, CHUNK), lambda i: (0, i)),
                  resident(kb.shape), resident(vt.shape),
                  pl.BlockSpec((ATT_W, CHUNK), lambda i: (0, i))],
        out_specs=pl.BlockSpec((ATT_W, CHUNK), lambda i: (0, i)),
        out_shape=jax.ShapeDtypeStruct((ATT_W, seq), F32),
        scratch_shapes=[pltpu.VMEM((seq, CHUNK), I32), pltpu.VMEM((4 * IDX_DIM, IDX_HEADS * CHUNK), BF16),
                        pltpu.VMEM((ATT_HEADS, ATT_HD, CHUNK), F32)],
        compiler_params=_cparams(("arbitrary",), 56 * 1024 * 1024),
    )(kix, qit, w_t, kb, vt, qt)
    return o_t.T


def _pmem_kernel(x_ref, g_ref, wq_ref, mk_ref, mv_ref, wo_ref, o_ref):
    q = jnp.dot(_rmsnorm_bf16(x_ref[...], g_ref[...]), wq_ref[...], preferred_element_type=F32)
    outs = []
    for h in range(MEM_HEADS):
        hs = slice(h * MEM_HD, (h + 1) * MEM_HD)
        s = lax.dot_general(q[:, hs].astype(BF16), mk_ref[:, hs], NT_DIMS, preferred_element_type=F32) * MEM_HD ** -0.5
        e = jnp.exp(s - jnp.max(s, axis=-1, keepdims=True))
        oh = jnp.dot(e.astype(BF16), mv_ref[:, hs], preferred_element_type=F32) / jnp.sum(e, axis=-1, keepdims=True)
        outs.append(oh.astype(BF16))
    o = jnp.concatenate(outs, axis=1)
    o_ref[...] = x_ref[...] + jnp.dot(o, wo_ref[...], preferred_element_type=F32)


def prompt_mem_attend(x, g, wq, mk, mv, wo, tm=512):
    m, d = x.shape
    n_mem = mk.shape[0]
    full = lambda shape: pl.BlockSpec(shape, lambda i: (0,) * len(shape))
    return pl.pallas_call(
        _pmem_kernel,
        grid=(m // tm,),
        in_specs=[pl.BlockSpec((tm, d), lambda i: (i, 0)), full((1, d)), full((d, MEM_W)), full((n_mem, MEM_W)),
                  full((n_mem, MEM_W)), full((MEM_W, d))],
        out_specs=pl.BlockSpec((tm, d), lambda i: (i, 0)),
        out_shape=jax.ShapeDtypeStruct((m, d), F32),
        compiler_params=_cparams(("parallel",), 48 * 1024 * 1024),
    )(x, g.reshape(1, d), wq, mk, mv, wo)


def _conv_seq(a, prev, cw, cb, row):
    p1 = prev[7:8, :]
    p2 = prev[6:7, :]
    r1 = jnp.where(row == 0, p1, pltpu.roll(a, 1, axis=0))
    r2 = jnp.where(row == 0, p2, jnp.where(row == 1, p1, pltpu.roll(a, 2, axis=0)))
    return cb + cw[0:1, :] * r2 + cw[1:2, :] * r1 + cw[2:3, :] * a


def _ffn_seq_kernel(x_ref, g_ref, wg_ref, wu_ref, cwg_ref, cwu_ref, cbg_ref, cbu_ref, wd_ref,
                    o_ref, lg_ref, lu_ref, carry_ref, h_ref):
    i = pl.program_id(0)
    j = pl.program_id(1)
    tm = x_ref.shape[0]
    row = lax.broadcasted_iota(I32, (tm, 1), 0)

    @pl.when(j == 0)
    def _():
        h_ref[...] = _rmsnorm_bf16(x_ref[...], g_ref[...])
        o_ref[...] = x_ref[...]

    @pl.when(i == 0)
    def _():
        carry_ref[j] = jnp.zeros(carry_ref.shape[1:], F32)

    hx = h_ref[...]
    f = None
    for s in range(FFN_TF // FFN_SUB):
        cs = slice(s * FFN_SUB, (s + 1) * FFN_SUB)
        ag = jnp.dot(hx, wg_ref[:, cs], preferred_element_type=F32)
        au = jnp.dot(hx, wu_ref[:, cs], preferred_element_type=F32)
        lg_ref[0, :, cs] = ag[tm - 8:tm, :]
        lu_ref[0, :, cs] = au[tm - 8:tm, :]
        cg = _conv_seq(ag, carry_ref[j, 0, :, cs], cwg_ref[:, cs], cbg_ref[:, cs], row)
        cu = _conv_seq(au, carry_ref[j, 1, :, cs], cwu_ref[:, cs], cbu_ref[:, cs], row)
        carry_ref[j, 0, :, cs] = ag[tm - 8:tm, :]
        carry_ref[j, 1, :, cs] = au[tm - 8:tm, :]
        act = ((cg * jax.nn.sigmoid(cg)) * cu).astype(BF16)
        fs = jnp.dot(act, wd_ref[cs, :], preferred_element_type=F32)
        f = fs if f is None else f + fs
    o_ref[...] += f


def ffn_seq(x, g, ffn_w, tm=512):
    wg, wu, cwg, cwu, cbg, cbu, wd = ffn_w
    m, d = x.shape
    tf = FFN_TF
    nf = D_FF_PAD // tf
    col = lambda r: pl.BlockSpec((r, tf), lambda i, j: (0, j))
    tail = pl.BlockSpec((1, 8, tf), lambda i, j: (i, 0, j))
    return pl.pallas_call(
        _ffn_seq_kernel,
        grid=(m // tm, nf),
        in_specs=[pl.BlockSpec((tm, d), lambda i, j: (i, 0)), pl.BlockSpec((1, d), lambda i, j: (0, 0)),
                  col(d), col(d), col(CONV_W), col(CONV_W), col(1), col(1),
                  pl.BlockSpec((tf, d), lambda i, j: (j, 0))],
        out_specs=[pl.BlockSpec((tm, d), lambda i, j: (i, 0)), tail, tail],
        out_shape=[jax.ShapeDtypeStruct((m, d), F32), jax.ShapeDtypeStruct((m // tm, 8, D_FF_PAD), F32),
                   jax.ShapeDtypeStruct((m // tm, 8, D_FF_PAD), F32)],
        scratch_shapes=[pltpu.VMEM((nf, 2, 8, tf), F32), pltpu.VMEM((tm, d), BF16)],
        compiler_params=_cparams(("arbitrary", "arbitrary"), 48 * 1024 * 1024),
    )(x, g.reshape(1, d), wg, wu, cwg, cwu, cbg, cbu, wd)


def _ffn_step_kernel(x_ref, g_ref, wg_ref, wu_ref, cwg_ref, cwu_ref, cbg_ref, cbu_ref, wd_ref,
                     pg0_ref, pg1_ref, pu0_ref, pu1_ref, o_ref, ag_ref, au_ref, h_ref):
    j = pl.program_id(0)

    @pl.when(j == 0)
    def _():
        h_ref[...] = _rmsnorm_bf16(x_ref[...], g_ref[...])
        o_ref[...] = x_ref[...]

    hx = h_ref[...]
    ag = jnp.dot(hx, wg_ref[...], preferred_element_type=F32)
    au = jnp.dot(hx, wu_ref[...], preferred_element_type=F32)
    ag_ref[...] = ag
    au_ref[...] = au
    cg = cbg_ref[...] + cwg_ref[0:1, :] * pg0_ref[...] + cwg_ref[1:2, :] * pg1_ref[...] + cwg_ref[2:3, :] * ag
    cu = cbu_ref[...] + cwu_ref[0:1, :] * pu0_ref[...] + cwu_ref[1:2, :] * pu1_ref[...] + cwu_ref[2:3, :] * au
    act = ((cg * jax.nn.sigmoid(cg)) * cu).astype(BF16)
    o_ref[...] += jnp.dot(act, wd_ref[...], preferred_element_type=F32)


def ffn_step(x, g, ffn_w, prev):
    wg, wu, cwg, cwu, cbg, cbu, wd = ffn_w
    m, d = x.shape
    tf = FFN_TF
    nf = D_FF_PAD // tf
    col = lambda r: pl.BlockSpec((r, tf), lambda j: (0, j))
    row = pl.BlockSpec((m, d), lambda j: (0, 0))
    return pl.pallas_call(
        _ffn_step_kernel,
        grid=(nf,),
        in_specs=[row, pl.BlockSpec((1, d), lambda j: (0, 0)), col(d), col(d), col(CONV_W), col(CONV_W), col(1), col(1),
                  pl.BlockSpec((tf, d), lambda j: (j, 0)), col(m), col(m), col(m), col(m)],
        out_specs=[row, col(m), col(m)],
        out_shape=[jax.ShapeDtypeStruct((m, d), F32), jax.ShapeDtypeStruct((m, D_FF_PAD), F32),
                   jax.ShapeDtypeStruct((m, D_FF_PAD), F32)],
        scratch_shapes=[pltpu.VMEM((m, d), BF16)],
        compiler_params=_cparams(("arbitrary",), 48 * 1024 * 1024),
    )(x, g.reshape(1, d), wg, wu, cwg, cwu, cbg, cbu, wd, *prev)


def _smix_kernel(qc_ref, kc_ref, v_ref, rg_ref, gu_ref, gv_ref, st_ref, cos_ref, sin_ref, g1_ref, retg_ref,
                 gmg_ref, wrow_ref, brow_ref, a_ref, c_ref, ns_ref, gmv_ref):
    cosc = cos_ref[...]
    sinc = sin_ref[...]
    half = RET_DK // 2

    def rot(x):
        x1, x2 = x[:half], x[half:]
        return jnp.concatenate([x1 * cosc - x2 * sinc, x1 * sinc + x2 * cosc], axis=0)

    for h in range(RET_HEADS):
        vs = slice(h * RET_DV, (h + 1) * RET_DV)
        qr = rot(qc_ref[0, h])
        kr = rot(kc_ref[0, h]) * RET_DK ** -0.5
        v = v_ref[0, :, vs]
        state = st_ref[0, h]
        g1 = g1_ref[h]
        inner = jnp.sum(qr * kr, axis=0, keepdims=True) * v
        cross = jnp.sum(qr * state, axis=0, keepdims=True) * g1
        o = inner + cross
        ns_ref[0, h] = g1 * state + kr * v
        ms = jnp.mean(o * o, axis=-1, keepdims=True)
        y = o * lax.rsqrt(ms + EPS) * retg_ref[:, vs]
        g = rg_ref[0, :, vs]
        a_ref[0, :, vs] = y * (g * jax.nn.sigmoid(g))

    u = jax.nn.gelu(gu_ref[0])
    v = jax.nn.gelu(gv_ref[0])
    ms = jnp.mean(v * v, axis=-1, keepdims=True)
    vn = v * lax.rsqrt(ms + EPS) * gmg_ref[...]
    c_ref[0] = u * (wrow_ref[...] * vn + brow_ref[...])
    gmv_ref[0] = vn


def sample_mixers(p_main, state, consts, retg_row, gmg_row, wrow, brow):
    nb = p_main.shape[0]
    cosc, sinc, g1 = consts
    qc = p_main[:, 0:RET_QK].reshape(nb, RET_HEADS, RET_DK, 1)
    kc = p_main[:, RET_QK:2 * RET_QK].reshape(nb, RET_HEADS, RET_DK, 1)
    p3 = p_main.reshape(nb, 1, p_main.shape[1])
    per_b = lambda width, blk: pl.BlockSpec((1, 1, width), lambda b: (b, 0, blk))
    col4 = pl.BlockSpec((1, RET_HEADS, RET_DK, 1), lambda b: (b, 0, 0, 0))
    st4 = pl.BlockSpec((1, RET_HEADS, RET_DK, RET_DV), lambda b: (b, 0, 0, 0))
    full = lambda shape: pl.BlockSpec(shape, lambda b: (0,) * len(shape))
    return pl.pallas_call(
        _smix_kernel,
        grid=(nb,),
        in_specs=[col4, col4, per_b(1024, 1), per_b(1024, 2), per_b(512, _M_GU // 512), per_b(512, _M_GV // 512), st4,
                  full(cosc.shape), full(sinc.shape), full(g1.shape), full(retg_row.shape), full(gmg_row.shape),
                  full(wrow.shape), full(brow.shape)],
        out_specs=[per_b(RET_V, 0), per_b(GM_W, 0), st4, per_b(GM_W, 0)],
        out_shape=[jax.ShapeDtypeStruct((nb, 1, RET_V), F32), jax.ShapeDtypeStruct((nb, 1, GM_W), F32),
                   jax.ShapeDtypeStruct(state.shape, F32), jax.ShapeDtypeStruct((nb, 1, GM_W), F32)],
        compiler_params=_cparams(("parallel",)),
    )(qc, kc, p3, p3, p3, p3, state, cosc, sinc, g1, retg_row, gmg_row, wrow, brow)


def _page_scores_kernel(pt_ref, qx_ref, w_ref, q8_ref, *refs, n_pg):
    ki_refs = refs[:n_pg]
    k_refs = refs[n_pg:2 * n_pg]
    isc_ref, asc_ref = refs[2 * n_pg:]
    qx = qx_ref[0]
    q_hi = qx[:IDX_HEADS]
    w = w_ref[0]
    q8 = q8_ref[0]
    for n in range(n_pg):
        kp = ki_refs[n][0]
        k_hi = kp.astype(BF16)
        k_lo = (kp - k_hi.astype(F32)).astype(BF16)
        s2 = lax.dot_general(qx, k_hi, NT_DIMS, preferred_element_type=F32)
        s = s2[:IDX_HEADS] + s2[IDX_HEADS:] + lax.dot_general(q_hi, k_lo, NT_DIMS, preferred_element_type=F32)
        isc_ref[0, n] = jnp.sum(jnp.maximum(s, 0.0) * w, axis=0, keepdims=True)
        asc_ref[0, n] = lax.dot_general(q8, k_refs[n][0].astype(BF16), NT_DIMS, preferred_element_type=F32)


def page_scores(page_table, qx, w_col, q8, pool_ki, pool_k, n_pg):
    nb, n_pages = page_table.shape
    groups = n_pages // n_pg
    pt = page_table.reshape(-1)

    def pool_spec(shape, n):
        return pl.BlockSpec((1,) + shape, lambda b, g, pt_ref: (pt_ref[b * n_pages + g * n_pg + n], 0, 0))

    per_b = lambda shape: pl.BlockSpec((1,) + shape, lambda b, g, pt_ref: (b,) + (0,) * len(shape))
    grid_spec = pltpu.PrefetchScalarGridSpec(
        num_scalar_prefetch=1,
        grid=(nb, groups),
        in_specs=[per_b((2 * IDX_HEADS, IDX_DIM)), per_b((IDX_HEADS, 1)), per_b((8, ATT_HD))]
        + [pool_spec((PAGE_SIZE, IDX_DIM), n) for n in range(n_pg)]
        + [pool_spec((PAGE_ROWS, ATT_HD), n) for n in range(n_pg)],
        out_specs=[pl.BlockSpec((1, n_pg, 1, PAGE_SIZE), lambda b, g, pt_ref: (b, g, 0, 0)),
                   pl.BlockSpec((1, n_pg, 8, PAGE_ROWS), lambda b, g, pt_ref: (b, g, 0, 0))],
    )
    return pl.pallas_call(
        functools.partial(_page_scores_kernel, n_pg=n_pg),
        grid_spec=grid_spec,
        out_shape=[jax.ShapeDtypeStruct((nb, n_pages, 1, PAGE_SIZE), F32),
                   jax.ShapeDtypeStruct((nb, n_pages, 8, PAGE_ROWS), F32)],
        compiler_params=_cparams(("arbitrary", "arbitrary")),
    )(pt, qx, w_col, q8, *([pool_ki] * n_pg), *([pool_k] * n_pg))


def _select_softmax_kernel(isc_ref, isce_ref, asc_ref, p_ref, keys_ref, thr_ref, *, topk, n_valid):
    h = pl.program_id(0)
    nb = isc_ref.shape[0]

    @pl.when(h == 0)
    def _():
        colid = lax.broadcasted_iota(I32, isc_ref.shape, 1)
        keys_ref[...] = jnp.where(colid < n_valid, _sortable_key(isc_ref[...]), INT_MIN)

        def count_ge(mid):
            return jnp.sum((keys_ref[...] >= mid).astype(I32), axis=1, keepdims=True)

        thr = _bisect_threshold(count_ge, jnp.full((nb, 1), INT_MIN, I32), jnp.full((nb, 1), INT_MAX, I32), topk)
        thr_ref[...] = jnp.broadcast_to(jnp.maximum(thr, INT_MIN + 1), thr_ref.shape)

    ce = lax.broadcasted_iota(I32, isce_ref.shape, 1)
    s = jnp.where(_sortable_key(isce_ref[...]) >= thr_ref[:, 0:1], asc_ref[0], -jnp.inf)
    s = jnp.where(ce < ATT_HEADS * n_valid, s, -jnp.inf)
    s = jnp.where((ce & (ATT_HEADS - 1)) == h, s, -jnp.inf)
    e = jnp.exp(s - jnp.max(s, axis=1, keepdims=True))
    p_ref[0] = e / jnp.sum(e, axis=1, keepdims=True)


def select_softmax(isc, isc_e, asc, topk, n_valid):
    nb = isc.shape[0]
    head_blk = pl.BlockSpec((1,) + asc.shape[1:], lambda h: (h, 0, 0))
    whole = lambda a: pl.BlockSpec(a.shape, lambda h: (0, 0))
    return pl.pallas_call(
        functools.partial(_select_softmax_kernel, topk=topk, n_valid=n_valid),
        grid=(ATT_HEADS,),
        in_specs=[whole(isc), whole(isc_e), head_blk],
        out_specs=head_blk,
        out_shape=jax.ShapeDtypeStruct(asc.shape, F32),
        scratch_shapes=[pltpu.VMEM(isc.shape, I32), pltpu.VMEM((nb, LANE), I32)],
        compiler_params=_cparams(("arbitrary",), 48 * 1024 * 1024),
    )(isc, isc_e, asc)


def _page_values_kernel(pt_ref, p_ref, *refs, n_pg):
    v_refs = refs[:n_pg]
    o_ref = refs[n_pg]
    acc = jnp.zeros(o_ref.shape[1:], F32)
    for n in range(n_pg):
        acc = acc + jnp.dot(p_ref[0, n], v_refs[n][0].astype(BF16), preferred_element_type=F32)

    @pl.when(pl.program_id(1) == 0)
    def _():
        o_ref[0] = acc

    @pl.when(pl.program_id(1) > 0)
    def _():
        o_ref[0] += acc


def page_values(page_table, p_pages, pool_v, n_pg):
    nb, n_pages = page_table.shape
    groups = n_pages // n_pg
    pt = page_table.reshape(-1)

    def pool_spec(n):
        return pl.BlockSpec((1, PAGE_ROWS, ATT_HD), lambda b, g, pt_ref: (pt_ref[b * n_pages + g * n_pg + n], 0, 0))

    grid_spec = pltpu.PrefetchScalarGridSpec(
        num_scalar_prefetch=1,
        grid=(nb, groups),
        in_specs=[pl.BlockSpec((1, n_pg, 8, PAGE_ROWS), lambda b, g, pt_ref: (b, g, 0, 0))]
        + [pool_spec(n) for n in range(n_pg)],
        out_specs=pl.BlockSpec((1, 8, ATT_HD), lambda b, g, pt_ref: (b, 0, 0)),
    )
    return pl.pallas_call(
        functools.partial(_page_values_kernel, n_pg=n_pg),
        grid_spec=grid_spec,
        out_shape=jax.ShapeDtypeStruct((nb, 8, ATT_HD), F32),
        compiler_params=_cparams(("arbitrary", "arbitrary")),
    )(pt, p_pages, *([pool_v] * n_pg))


def _head_rows(q, scale):
    nb = q.shape[0]
    q3 = (q * scale).reshape(nb, ATT_HEADS, ATT_HD)
    return jnp.pad(q3, ((0, 0), (0, 8 - ATT_HEADS), (0, 0))).astype(BF16)


def dsa_sample(aq, ak, av, iq, iw, ik, pool_k, pool_v, pool_ki, page_table, n_pg=8):
    nb, n_pages = page_table.shape
    past = n_pages * PAGE_SIZE
    n_keys = past + PAGE_SIZE
    topk = min(TOPK_MAX, (past + 1) // 4)
    iq3 = iq.reshape(nb, IDX_HEADS, IDX_DIM)
    q_hi = iq3.astype(BF16)
    q_lo = (iq3 - q_hi.astype(F32)).astype(BF16)
    qx = jnp.concatenate([q_hi, q_lo], axis=1)
    w_col = (iw * (IDX_DIM ** -0.5 * IDX_HEADS ** -0.5)).reshape(nb, IDX_HEADS, 1)
    q8 = _head_rows(aq, ATT_HD ** -0.5)
    own = jnp.arange(nb, dtype=I32).reshape(nb, 1)
    pad_rows = lambda a, rows: jnp.pad(a, ((0, 0), (0, rows - a.shape[1]), (0, 0)))
    own_ki = pad_rows(ik[:, None, :], PAGE_SIZE)
    own_k = pad_rows(ak.reshape(nb, ATT_HEADS, ATT_HD), PAGE_ROWS)
    own_v = pad_rows(av.reshape(nb, ATT_HEADS, ATT_HD), PAGE_ROWS)
    isc_p, asc_p = page_scores(page_table, qx, w_col, q8, pool_ki, pool_k, n_pg)
    isc_n, asc_n = page_scores(own, qx, w_col, q8, own_ki, own_k, 1)
    isc = jnp.concatenate([isc_p, isc_n], axis=1).reshape(nb, n_keys)
    asc = jnp.concatenate([asc_p, asc_n], axis=1)[:, :, :ATT_HEADS]
    asc = asc.transpose(2, 0, 1, 3).reshape(ATT_HEADS, nb, n_keys * ATT_HEADS)
    p = select_softmax(isc, jnp.repeat(isc, ATT_HEADS, axis=1), asc, topk, past + 1)
    p = p.reshape(ATT_HEADS, nb, n_pages + 1, PAGE_ROWS).transpose(1, 2, 0, 3)
    p = jnp.pad(p, ((0, 0), (0, 0), (0, 8 - ATT_HEADS), (0, 0))).astype(BF16)
    o_p = page_values(page_table, p[:, :n_pages], pool_v, n_pg)
    o_n = page_values(own, p[:, n_pages:], own_v, 1)
    return (o_p + o_n)[:, :ATT_HEADS].reshape(nb, ATT_W)


def _smem_kernel(q8_ref, mk_ref, mv_ref, o_ref):
    rows = mk_ref.shape[1]
    head_of = lambda axis: lax.broadcasted_iota(I32, (8, rows), axis) & (MEM_HEADS - 1)
    own_head = head_of(1) == head_of(0)
    for b in range(q8_ref.shape[0]):
        s = lax.dot_general(q8_ref[b], mk_ref[b].astype(BF16), NT_DIMS, preferred_element_type=F32)
        s = jnp.where(own_head, s, -jnp.inf)
        e = jnp.exp(s - jnp.max(s, axis=-1, keepdims=True))
        p = e / jnp.sum(e, axis=-1, keepdims=True)
        o_ref[b] = jnp.dot(p.astype(BF16), mv_ref[b].astype(BF16), preferred_element_type=F32)


def sample_mem_core(q, mem_k, mem_v, layer, bb=8):
    nb = q.shape[0]
    rows = mem_k.shape[1]
    q8 = _head_rows(q, MEM_HD ** -0.5)
    first = layer * (nb // bb)
    o = pl.pallas_call(
        _smem_kernel,
        grid=(nb // bb,),
        in_specs=[pl.BlockSpec((bb, 8, MEM_HD), lambda i: (i, 0, 0)),
                  pl.BlockSpec((bb, rows, MEM_HD), lambda i: (first + i, 0, 0)),
                  pl.BlockSpec((bb, rows, MEM_HD), lambda i: (first + i, 0, 0))],
        out_specs=pl.BlockSpec((bb, 8, MEM_HD), lambda i: (i, 0, 0)),
        out_shape=jax.ShapeDtypeStruct((nb, 8, MEM_HD), F32),
        compiler_params=_cparams(("parallel",), 48 * 1024 * 1024),
    )(q8, mem_k, mem_v)
    return o[:, :MEM_HEADS].reshape(nb, MEM_W)


def _prep_layer(w_in, w_out, w_mq, w_mk, w_mv, w_mo, w_up, conv_w, conv_b, w_down, gm_ws, gm_bs):
    w_proj = jnp.concatenate([w_in[:, :_O_IQ_END], w_in[:, _O_GU:], w_in[:, _O_IK:_O_GU], w_in[:, _O_IW:_O_IK],
                              jnp.zeros((D_MODEL, N_PROJ - N_MAIN - IDX_DIM - IDX_HEADS), w_in.dtype)],
                             axis=1).astype(BF16)
    padc = lambda a: jnp.pad(a, ((0, 0), (0, D_FF_PAD - D_FF)))
    ffn_w = (padc(w_up[:, :D_FF]).astype(BF16), padc(w_up[:, D_FF:]).astype(BF16),
             padc(conv_w[:, :D_FF]), padc(conv_w[:, D_FF:]),
             padc(conv_b[None, :D_FF]), padc(conv_b[None, D_FF:]),
             jnp.pad(w_down, ((0, D_FF_PAD - D_FF), (0, 0))).astype(BF16))
    causal = jnp.tril(jnp.ones((CHUNK, CHUNK), dtype=bool))
    wtril = jnp.where(causal[None], gm_ws, 0.0).astype(BF16)
    bsm = jnp.broadcast_to(gm_bs[:, :, None], (GM_GROUPS, CHUNK, GM_GW))
    wrow = jnp.repeat(gm_ws[:, 0, 0], GM_GW).reshape(1, GM_W)
    brow = jnp.repeat(gm_bs[:, 0], GM_GW).reshape(1, GM_W)
    return dict(w_proj=w_proj, w_out=w_out.astype(BF16), w_mq=w_mq.astype(BF16),
                w_mkv=jnp.concatenate([w_mk, w_mv], axis=1).astype(BF16), w_mo=w_mo.astype(BF16), ffn_w=ffn_w,
                wtril=wtril, bsm=bsm, wrow=wrow, brow=brow)


def _retention_consts(seq, past):
    log_g = jnp.log(1.0 - 2.0 ** (-5.0 - jnp.arange(RET_HEADS, dtype=F32)))
    half = RET_DK // 2
    inv = ROPE_BASE ** (-jnp.arange(half, dtype=F32) / half)
    ang = jnp.arange(seq).astype(F32)[:, None] * inv[None, :]
    cos, sin = jnp.cos(ang), jnp.sin(ang)
    cosf = jnp.concatenate([cos, cos], axis=1)
    sinf = jnp.concatenate([-sin, sin], axis=1)
    n = jnp.arange(CHUNK, dtype=F32)
    diff = n[:, None] - n[None, :]
    causal = diff >= 0
    dmat = jnp.where(causal[None], jnp.exp(log_g[:, None, None] * jnp.where(causal, diff, 0.0)[None]), 0.0)
    cdec = jnp.broadcast_to(jnp.exp(log_g[:, None] * (n[None, :] + 1.0))[:, :, None], (RET_HEADS, CHUNK, RET_DV))
    kdec = jnp.broadcast_to(jnp.exp(log_g[:, None] * (CHUNK - 1.0 - n[None, :]))[:, :, None], (RET_HEADS, CHUNK, RET_DK))
    gc = jnp.broadcast_to(jnp.exp(log_g * CHUNK)[:, None, None], (RET_HEADS, RET_DK, RET_DV))
    ang_s = jnp.full((1,), past, dtype=F32)[:, None] * inv[None, :]
    cosc = jnp.cos(ang_s).reshape(half, 1)
    sinc = jnp.sin(ang_s).reshape(half, 1)
    g1 = jnp.broadcast_to(jnp.exp(log_g * 1.0)[:, None, None], (RET_HEADS, 1, RET_DV))
    return (cosf, sinf, dmat, cdec, kdec, gc), (cosc, sinc, g1)


def _pad_ff(a):
    return jnp.pad(a, ((0, 0), (0, D_FF_PAD - D_FF)))


def kernel(x_prompt, x_sample, mem_prompt, cache_k, cache_v, cache_kidx, page_table, cache_mem_k, cache_mem_v, state_ret, state_conv, norm_mix_g, w_in, ret_norm_g, gm_norm_g, gm_ws, gm_bs, w_out, norm_mem_g, mem_in_g, w_mq, w_mk, w_mv, w_mo, norm_ffn_g, w_up, conv_w, conv_b, w_down, final_norm_g):
    depth = w_in.shape[0]
    seq = x_prompt.shape[1]
    nbd = x_sample.shape[0]
    n_mem = mem_prompt.shape[1]
    past = page_table.shape[1] * PAGE_SIZE
    p_consts, s_consts = _retention_consts(seq, past)
    n_pool = cache_k.shape[1]
    pool_k = cache_k.reshape(depth * n_pool, PAGE_ROWS, ATT_HD)
    pool_v = cache_v.reshape(depth * n_pool, PAGE_ROWS, ATT_HD)
    pool_ki = cache_kidx.reshape(depth * n_pool, PAGE_SIZE, IDX_DIM)
    mem_k_all = cache_mem_k.reshape(depth * nbd, n_mem * MEM_HEADS, MEM_HD)
    mem_v_all = cache_mem_v.reshape(depth * nbd, n_mem * MEM_HEADS, MEM_HD)

    xp = x_prompt.reshape(seq, D_MODEL)
    xs = x_sample.reshape(nbd, D_MODEL)
    mem = mem_prompt.reshape(n_mem, D_MODEL)
    outs = {k: [] for k in ("kp", "vp", "kip", "ks", "vs", "kis", "mkp", "mvp", "rsp", "rss", "csp", "css", "gvp", "gvs")}

    for l in range(depth):
        w = _prep_layer(w_in[l], w_out[l], w_mq[l], w_mk[l], w_mv[l], w_mo[l], w_up[l], conv_w[l], conv_b[l],
                        w_down[l], gm_ws[l], gm_bs[l])
        retg_row = ret_norm_g[l].reshape(1, RET_V)
        gmg_row = gm_norm_g[l].reshape(1, GM_W)

        pm = norm_matmul(xp, norm_mix_g[l], w["w_proj"])
        ak, av = pm[:, _M_AK:_M_AV], pm[:, _M_AV:_M_IQ]
        ik = pm[:, _M_IK:_M_IW]
        a_out, c_out, r_state, gm_v = prompt_mixers(pm, p_consts, retg_row, gmg_row, w["wtril"], w["bsm"])
        b_out = dsa_prompt(pm[:, _M_AQ:_M_AK], ak, av, pm[:, _M_IQ:_M_GU], pm[:, _M_IW:_M_IW + IDX_HEADS], ik)
        mixed = jnp.concatenate([a_out, b_out.astype(BF16), c_out], axis=1)
        xp = matmul(mixed, w["w_out"], res=xp)
        mkv = norm_matmul(mem, mem_in_g[l], w["w_mkv"])
        mk, mv = mkv[:, :MEM_W], mkv[:, MEM_W:]
        xp = prompt_mem_attend(xp, norm_mem_g[l], w["w_mq"], mk.astype(BF16), mv.astype(BF16), w["w_mo"])
        xp, tail_g, tail_u = ffn_seq(xp, norm_ffn_g[l], w["ffn_w"])
        last_g, last_u = tail_g[-1], tail_u[-1]
        outs["kp"].append(ak.reshape(1, seq, ATT_HEADS, ATT_HD))
        outs["vp"].append(av.reshape(1, seq, ATT_HEADS, ATT_HD))
        outs["kip"].append(ik.reshape(1, seq, IDX_DIM))
        outs["mkp"].append(mk.reshape(1, n_mem, MEM_HEADS, MEM_HD))
        outs["mvp"].append(mv.reshape(1, n_mem, MEM_HEADS, MEM_HD))
        outs["rsp"].append(r_state.reshape(1, RET_HEADS, RET_DK, RET_DV))
        outs["csp"].append(jnp.concatenate([last_g[8 - (CONV_W - 1):, :D_FF], last_u[8 - (CONV_W - 1):, :D_FF]], axis=1)[None])
        outs["gvp"].append(gm_v[None])

        pm = norm_matmul(xs, norm_mix_g[l], w["w_proj"])
        ak, av = pm[:, _M_AK:_M_AV], pm[:, _M_AV:_M_IQ]
        ik = pm[:, _M_IK:_M_IW]
        a_out, c_out, r_state, gm_v = sample_mixers(pm, state_ret[l], s_consts, retg_row, gmg_row, w["wrow"], w["brow"])
        b_out = dsa_sample(pm[:, _M_AQ:_M_AK], ak, av, pm[:, _M_IQ:_M_GU], pm[:, _M_IW:_M_IW + IDX_HEADS], ik,
                           pool_k, pool_v, pool_ki, page_table + l * n_pool)
        mixed = jnp.concatenate([a_out.reshape(nbd, RET_V), b_out, c_out.reshape(nbd, GM_W)], axis=1).astype(BF16)
        xs = matmul(mixed, w["w_out"], res=xs)
        q = norm_matmul(xs, norm_mem_g[l], w["w_mq"])
        o = sample_mem_core(q, mem_k_all, mem_v_all, l)
        xs = matmul(o.astype(BF16), w["w_mo"], res=xs)
        sc = state_conv[l]
        prev = (_pad_ff(sc[:, 0, :D_FF]), _pad_ff(sc[:, 1, :D_FF]), _pad_ff(sc[:, 0, D_FF:]), _pad_ff(sc[:, 1, D_FF:]))
        xs, a_g, a_u = ffn_step(xs, norm_ffn_g[l], w["ffn_w"], prev)
        a_new = jnp.concatenate([a_g[:, :D_FF], a_u[:, :D_FF]], axis=1)
        outs["ks"].append(ak.reshape(nbd, 1, ATT_HEADS, ATT_HD))
        outs["vs"].append(av.reshape(nbd, 1, ATT_HEADS, ATT_HD))
        outs["kis"].append(ik.reshape(nbd, 1, IDX_DIM))
        outs["rss"].append(r_state)
        outs["css"].append(jnp.stack([sc[:, 1, :], a_new], axis=1))
        outs["gvs"].append(gm_v)

    y_prompt = rmsnorm_rows(xp, final_norm_g, F32).reshape(1, seq, D_MODEL)
    y_sample = rmsnorm_rows(xs, final_norm_g, F32).reshape(nbd, 1, D_MODEL)
    st = lambda k: jnp.stack(outs[k])
    return (y_prompt, y_sample, st("kp"), st("vp"), st("kip"), st("ks"), st("vs"), st("kis"), st("mkp"), st("mvp"),
            st("rsp"), st("rss"), st("csp"), st("css"), st("gvp"), st("gvs"))
```

```python
import functools

import jax
import jax.numpy as jnp
from jax import lax
from jax.experimental import pallas as pl
from jax.experimental.pallas import tpu as pltpu

F32 = jnp.float32
BF16 = jnp.bfloat16
I32 = jnp.int32

D_MODEL = 2048
PAGE_SIZE = 128
RET_HEADS = 4
RET_DK = 128
RET_DV = 256
CHUNK = 128
ROPE_BASE = 10000.0
ATT_HEADS = 4
ATT_HD = 128
IDX_HEADS = 16
IDX_DIM = 64
TOPK_MAX = 256
GM_GROUPS = 4
GM_GW = 128
MEM_HEADS = 4
MEM_HD = 128
D_FF = 5504
CONV_W = 3
EPS = 1e-6

RET_QK = RET_HEADS * RET_DK
RET_V = RET_HEADS * RET_DV
ATT_W = ATT_HEADS * ATT_HD
IDX_Q = IDX_HEADS * IDX_DIM
GM_W = GM_GROUPS * GM_GW
MEM_W = MEM_HEADS * MEM_HD
PAGE_ROWS = PAGE_SIZE * ATT_HEADS

_O_IQ_END = 2 * RET_QK + 2 * RET_V + 3 * ATT_W + IDX_Q
_O_IW = _O_IQ_END
_O_IK = _O_IW + IDX_HEADS
_O_GU = _O_IK + IDX_DIM
_M_AQ = 2 * RET_QK + 2 * RET_V
_M_AK = _M_AQ + ATT_W
_M_AV = _M_AK + ATT_W
_M_IQ = _M_AV + ATT_W
_M_GU = _M_IQ + IDX_Q
_M_GV = _M_GU + GM_W
N_MAIN = _M_GV + GM_W
_M_IK = N_MAIN
_M_IW = _M_IK + IDX_DIM
N_PROJ = N_MAIN + 512
LANE = 128
D_FF_PAD = 5632
FFN_TF = 512
FFN_SUB = 512
DSA_GROUP = 512
DSA_SCORE_ROWS = 256
BISECT_STEPS_PER_CHECK = 4

INT_MIN = -2 ** 31
INT_MAX = 2 ** 31 - 1
V7X_VMEM_BYTES = 64 * 1024 * 1024
NT_DIMS = (((1,), (1,)), ((), ()))


def _cparams(semantics, vmem_bytes=None):
    return pltpu.CompilerParams(dimension_semantics=semantics, vmem_limit_bytes=vmem_bytes)


def _sortable_key(x):
    b = lax.bitcast_convert_type(x, I32)
    return jnp.where(x == 0.0, 0, b ^ ((b >> 31) & INT_MAX))


def _bisect_threshold(count_ge, lo, hi, topk):
    def cond(st):
        it, active, _, _ = st
        return jnp.logical_and(it < 32, active > 0)

    def body(st):
        it, _, lo, hi = st
        for _ in range(BISECT_STEPS_PER_CHECK):
            mid = (lo >> 1) + (hi >> 1) + (lo & hi & 1)
            cnt = count_ge(mid)
            ge = cnt >= topk
            lo = jnp.where(ge, mid, lo)
            hi = jnp.where(cnt == topk, mid + 1, jnp.where(ge, hi, mid))
        active = jnp.max((hi - 1 > lo).astype(I32))
        return it + BISECT_STEPS_PER_CHECK, active, lo, hi

    _, _, lo, _ = lax.while_loop(cond, body, (jnp.int32(0), jnp.int32(1), lo, hi))
    return lo


def _rmsnorm_bf16(x, g):
    ms = jnp.mean(x * x, axis=-1, keepdims=True)
    return (x * lax.rsqrt(ms + EPS) * g).astype(BF16)


def _norm_kernel(x_ref, g_ref, o_ref):
    x = x_ref[...]
    ms = jnp.mean(x * x, axis=-1, keepdims=True)
    o_ref[...] = (x * lax.rsqrt(ms + EPS) * g_ref[...]).astype(o_ref.dtype)


def rmsnorm_rows(x, g, out_dtype):
    m, d = x.shape
    tm = min(m, 256)
    return pl.pallas_call(
        _norm_kernel,
        grid=(m // tm,),
        in_specs=[pl.BlockSpec((tm, d), lambda i: (i, 0)), pl.BlockSpec((1, d), lambda i: (0, 0))],
        out_specs=pl.BlockSpec((tm, d), lambda i: (i, 0)),
        out_shape=jax.ShapeDtypeStruct((m, d), out_dtype),
        compiler_params=_cparams(("parallel",)),
    )(x, g.reshape(1, d))


def _mm_kernel(x_ref, w_ref, o_ref):
    o_ref[...] = jnp.dot(x_ref[...], w_ref[...], preferred_element_type=F32)


def _mm_res_kernel(x_ref, w_ref, r_ref, o_ref):
    o_ref[...] = r_ref[...] + jnp.dot(x_ref[...], w_ref[...], preferred_element_type=F32)


def matmul(x, w, res=None, tm=1024, tn=512):
    m, k = x.shape
    n = w.shape[1]
    tm = min(tm, m)
    tn = min(tn, n)
    assert m % tm == 0 and n % tn == 0
    in_specs = [pl.BlockSpec((tm, k), lambda i, j: (i, 0)), pl.BlockSpec((k, tn), lambda i, j: (0, j))]
    args = [x, w]
    body = _mm_kernel
    if res is not None:
        in_specs.append(pl.BlockSpec((tm, tn), lambda i, j: (i, j)))
        args.append(res)
        body = _mm_res_kernel
    return pl.pallas_call(
        body,
        grid=(m // tm, n // tn),
        in_specs=in_specs,
        out_specs=pl.BlockSpec((tm, tn), lambda i, j: (i, j)),
        out_shape=jax.ShapeDtypeStruct((m, n), F32),
        compiler_params=_cparams(("parallel", "parallel"), 48 * 1024 * 1024),
    )(*args)


def _out_proj_kernel(a_ref, b_ref, c_ref, wa_ref, wb_ref, wc_ref, r_ref, o_ref):
    acc = jnp.dot(a_ref[...].astype(BF16), wa_ref[...], preferred_element_type=F32)
    acc = acc + jnp.dot(b_ref[...].astype(BF16), wb_ref[...], preferred_element_type=F32)
    acc = acc + jnp.dot(c_ref[...].astype(BF16), wc_ref[...], preferred_element_type=F32)
    o_ref[...] = r_ref[...] + acc


def out_projection(a, b, c, w_out, res, tm=1024, tn=512):
    m = a.shape[0]
    n = w_out.shape[1]
    tm = min(tm, m)
    ka, kb, kc = a.shape[1], b.shape[1], c.shape[1]
    assert m % tm == 0 and n % tn == 0 and ka % kb == 0 and kb == kc
    rows = lambda k: pl.BlockSpec((tm, k), lambda i, j: (i, 0))
    return pl.pallas_call(
        _out_proj_kernel,
        grid=(m // tm, n // tn),
        in_specs=[rows(ka), rows(kb), rows(kc),
                  pl.BlockSpec((ka, tn), lambda i, j: (0, j)),
                  pl.BlockSpec((kb, tn), lambda i, j: (ka // kb, j)),
                  pl.BlockSpec((kc, tn), lambda i, j: (ka // kb + 1, j)),
                  pl.BlockSpec((tm, tn), lambda i, j: (i, j))],
        out_specs=pl.BlockSpec((tm, tn), lambda i, j: (i, j)),
        out_shape=jax.ShapeDtypeStruct((m, n), F32),
        compiler_params=_cparams(("parallel", "parallel"), 48 * 1024 * 1024),
    )(a, b, c, w_out, w_out, w_out, res)


def _norm_mm_kernel(x_ref, g_ref, w_ref, o_ref, h_ref):
    @pl.when(pl.program_id(1) == 0)
    def _():
        h_ref[...] = _rmsnorm_bf16(x_ref[...], g_ref[...])

    o_ref[...] = jnp.dot(h_ref[...], w_ref[...], preferred_element_type=F32)


def norm_matmul(x, g, w, tm=1024, tn=512):
    m, k = x.shape
    n = w.shape[1]
    tm = min(tm, m)
    tn = min(tn, n)
    assert m % tm == 0 and n % tn == 0
    return pl.pallas_call(
        _norm_mm_kernel,
        grid=(m // tm, n // tn),
        in_specs=[pl.BlockSpec((tm, k), lambda i, j: (i, 0)), pl.BlockSpec((1, k), lambda i, j: (0, 0)),
                  pl.BlockSpec((k, tn), lambda i, j: (0, j))],
        out_specs=pl.BlockSpec((tm, tn), lambda i, j: (i, j)),
        out_shape=jax.ShapeDtypeStruct((m, n), F32),
        scratch_shapes=[pltpu.VMEM((tm, k), BF16)],
        compiler_params=_cparams(("arbitrary", "arbitrary"), 48 * 1024 * 1024),
    )(x, g.reshape(1, k), w)


def _mix_kernel(rq_ref, rk_ref, rv_ref, rg_ref, gu_ref, gv_ref, cos_ref, sin_ref, dmat_ref, cdec_ref, kdec_ref,
                gc_ref, retg_ref, gmg_ref, wtril_ref, bsm_ref, a_ref, c_ref, sfin_ref, gmv_ref, s_ref):
    c = pl.program_id(0)
    last = pl.num_programs(0) - 1

    @pl.when(c == 0)
    def _():
        s_ref[...] = jnp.zeros_like(s_ref)

    cosf = cos_ref[...]
    sinf = sin_ref[...]
    for h in range(RET_HEADS):
        q = rq_ref[:, h * RET_DK:(h + 1) * RET_DK]
        k = rk_ref[:, h * RET_DK:(h + 1) * RET_DK]
        qr = q * cosf + pltpu.roll(q, RET_DK // 2, axis=1) * sinf
        kr = (k * cosf + pltpu.roll(k, RET_DK // 2, axis=1) * sinf) * RET_DK ** -0.5
        vb = rv_ref[:, h * RET_DV:(h + 1) * RET_DV].astype(BF16)
        qb = qr.astype(BF16)
        state = s_ref[h]
        a = lax.dot_general(qb, kr.astype(BF16), NT_DIMS, preferred_element_type=F32) * dmat_ref[h]
        inner = jnp.dot(a.astype(BF16), vb, preferred_element_type=F32)
        cross = jnp.dot(qb, state.astype(BF16), preferred_element_type=F32) * cdec_ref[h]
        o = inner + cross
        kd_t = jnp.transpose(kr * kdec_ref[h]).astype(BF16)
        s_ref[h] = gc_ref[h] * state + jnp.dot(kd_t, vb, preferred_element_type=F32)
        ms = jnp.mean(o * o, axis=-1, keepdims=True)
        y = o * lax.rsqrt(ms + EPS) * retg_ref[:, h * RET_DV:(h + 1) * RET_DV]
        g = rg_ref[:, h * RET_DV:(h + 1) * RET_DV]
        a_ref[:, h * RET_DV:(h + 1) * RET_DV] = (y * (g * jax.nn.sigmoid(g))).astype(a_ref.dtype)

    u = jax.nn.gelu(gu_ref[...])
    v = jax.nn.gelu(gv_ref[...])
    ms = jnp.mean(v * v, axis=-1, keepdims=True)
    vn = v * lax.rsqrt(ms + EPS) * gmg_ref[...]
    vb = vn.astype(BF16)
    for g in range(GM_GROUPS):
        sl = slice(g * GM_GW, (g + 1) * GM_GW)
        mixed = jnp.dot(wtril_ref[g], vb[:, sl], preferred_element_type=F32) + bsm_ref[g]
        c_ref[:, sl] = (u[:, sl] * mixed).astype(c_ref.dtype)

    @pl.when(c == last)
    def _():
        sfin_ref[...] = s_ref[...]
        gmv_ref[...] = vn


def prompt_mixers(p_main, consts, retg_row, gmg_row, wtril, bsm):
    seq = p_main.shape[0]
    nc = seq // CHUNK
    cosf, sinf, dmat, cdec, kdec, gc = consts
    w512 = lambda blk: pl.BlockSpec((CHUNK, 512), lambda c: (c, blk))
    w1024 = lambda blk: pl.BlockSpec((CHUNK, 1024), lambda c: (c, blk))
    full = lambda shape: pl.BlockSpec(shape, lambda c: (0,) * len(shape))
    return pl.pallas_call(
        _mix_kernel,
        grid=(nc,),
        in_specs=[w512(0), w512(1), w1024(1), w1024(2), w512(_M_GU // 512), w512(_M_GV // 512),
                  pl.BlockSpec((CHUNK, RET_DK), lambda c: (c, 0)), pl.BlockSpec((CHUNK, RET_DK), lambda c: (c, 0)),
                  full(dmat.shape), full(cdec.shape), full(kdec.shape), full(gc.shape),
                  full(retg_row.shape), full(gmg_row.shape), full(wtril.shape), full(bsm.shape)],
        out_specs=[pl.BlockSpec((CHUNK, RET_V), lambda c: (c, 0)), pl.BlockSpec((CHUNK, GM_W), lambda c: (c, 0)),
                   full((RET_HEADS, RET_DK, RET_DV)), full((CHUNK, GM_W))],
        out_shape=[jax.ShapeDtypeStruct((seq, RET_V), BF16), jax.ShapeDtypeStruct((seq, GM_W), BF16),
                   jax.ShapeDtypeStruct((RET_HEADS, RET_DK, RET_DV), F32),
                   jax.ShapeDtypeStruct((CHUNK, GM_W), F32)],
        scratch_shapes=[pltpu.VMEM((RET_HEADS, RET_DK, RET_DV), F32)],
        compiler_params=_cparams(("arbitrary",)),
    )(p_main, p_main, p_main, p_main, p_main, p_main, cosf, sinf, dmat, cdec, kdec, gc, retg_row, gmg_row, wtril, bsm)


def _dsa_prompt_kernel(kix_ref, qit_ref, w_ref, k_ref, vt_ref, qt_ref, o_ref, keys_ref, wq_ref, acc_ref, *, topk):
    i = pl.program_id(0)
    grp = DSA_GROUP
    per_grp = grp // CHUNK
    ng = i // per_grp + 1

    def grp_rows(g):
        return pl.ds(pl.multiple_of(g * grp, grp), grp)

    for h in range(IDX_HEADS):
        wq_ref[:, h * CHUNK:(h + 1) * CHUNK] = qit_ref[h]
    w_row = jnp.concatenate([w_ref[h:h + 1, :] for h in range(IDX_HEADS)], axis=1)

    def group_keys(g):
        parts = []
        for sub in range(grp // DSA_SCORE_ROWS):
            r0 = pl.multiple_of(g * grp + sub * DSA_SCORE_ROWS, DSA_SCORE_ROWS)
            kt = kix_ref[pl.ds(r0, DSA_SCORE_ROWS), :]
            acc = jnp.zeros((DSA_SCORE_ROWS, CHUNK), F32)
            for hp in range(IDX_HEADS // 2):
                cs = slice(hp * 2 * CHUNK, (hp + 1) * 2 * CHUNK)
                r = jnp.dot(kt, wq_ref[:, cs], preferred_element_type=F32)
                r = jnp.maximum(r, 0.0) * w_row[:, cs]
                acc = acc + r[:, :CHUNK] + r[:, CHUNK:]
            parts.append(_sortable_key(acc))
        return jnp.concatenate(parts, axis=0)

    def full_group(g, carry):
        keys_ref[grp_rows(g), :] = group_keys(g)
        return carry

    lax.fori_loop(0, ng - 1, full_group, 0)
    key_pos = (ng - 1) * grp + lax.broadcasted_iota(I32, (grp, CHUNK), 0)
    q_pos = i * CHUNK + lax.broadcasted_iota(I32, (grp, CHUNK), 1)
    keys_ref[grp_rows(ng - 1), :] = jnp.where(key_pos <= q_pos, group_keys(ng - 1), INT_MIN)

    def count_ge(mid):
        def body(g, cnt):
            m = (keys_ref[grp_rows(g), :] >= mid).astype(I32)
            for t in range(per_grp):
                cnt = cnt + m[t * CHUNK:(t + 1) * CHUNK]
            return cnt

        cnt = lax.fori_loop(0, ng, body, jnp.zeros((CHUNK, CHUNK), I32))
        return jnp.sum(cnt, axis=0, keepdims=True)

    if grp % topk == 0:
        def class_max(g, cm):
            kg = keys_ref[grp_rows(g), :]
            for t in range(grp // topk):
                cm = jnp.maximum(cm, kg[t * topk:(t + 1) * topk])
            return cm

        cm = lax.fori_loop(0, ng, class_max, jnp.full((topk, CHUNK), INT_MIN, I32))
        lo0 = jnp.min(cm, axis=0, keepdims=True)
        hi0 = jnp.minimum(jnp.max(cm, axis=0, keepdims=True), INT_MAX - 1) + 1
    else:
        lo0 = jnp.full((1, CHUNK), INT_MIN, I32)
        hi0 = jnp.full((1, CHUNK), INT_MAX, I32)
    t_k = _bisect_threshold(count_ge, lo0, hi0, topk)

    over = jnp.logical_and(count_ge(t_k) > topk, t_k > INT_MIN)

    @pl.when(jnp.max(over.astype(I32)) > 0)
    def _():
        need = topk - count_ge(t_k + 1)

        def pos_of(g):
            return g * grp + lax.broadcasted_iota(I32, (grp, CHUNK), 0)

        def count_tied_upto(j):
            def body(g, cnt):
                m = jnp.where(keys_ref[grp_rows(g), :] == t_k, jnp.where(pos_of(g) <= j, 1, 0), 0)
                for t in range(per_grp):
                    cnt = cnt + m[t * CHUNK:(t + 1) * CHUNK]
                return cnt

            cnt = lax.fori_loop(0, ng, body, jnp.zeros((CHUNK, CHUNK), I32))
            return jnp.sum(cnt, axis=0, keepdims=True)

        def step(_, lh):
            lo_j, hi_j = lh
            mid = (lo_j + hi_j) >> 1
            ok = count_tied_upto(mid) >= need
            return jnp.where(ok, lo_j, mid), jnp.where(ok, mid, hi_j)

        n_keys = keys_ref.shape[0]
        _, j_keep = lax.fori_loop(0, n_keys.bit_length(), step,
                                  (jnp.full((1, CHUNK), -1, I32), jnp.full((1, CHUNK), n_keys - 1, I32)))

        def drop_rest(g, carry):
            k = keys_ref[grp_rows(g), :]
            dropped = jnp.where(pos_of(g) > j_keep, jnp.where(over, INT_MIN, k), k)
            keys_ref[grp_rows(g), :] = jnp.where(k == t_k, dropped, k)
            return carry

        lax.fori_loop(0, ng, drop_rest, 0)

    thr = jnp.maximum(t_k, INT_MIN + 1)

    acc_ref[...] = jnp.zeros_like(acc_ref)

    def attend(g, carry):
        sel = keys_ref[grp_rows(g), :] >= thr
        out = []
        for h in range(ATT_HEADS):
            hs = slice(h * ATT_HD, (h + 1) * ATT_HD)
            m, l = carry[h]
            s = jnp.dot(k_ref[grp_rows(g), hs], qt_ref[hs, :], preferred_element_type=F32)
            s = jnp.where(sel, s, -jnp.inf)
            m_new = jnp.maximum(m, jnp.max(s, axis=0, keepdims=True))
            m_safe = jnp.where(m_new == -jnp.inf, 0.0, m_new)
            p = jnp.exp(s - m_safe)
            alpha = jnp.exp(m - m_safe)
            l = alpha * l + jnp.sum(p, axis=0, keepdims=True)
            acc_ref[h] = acc_ref[h] * alpha + jnp.dot(vt_ref[g, hs, :], p.astype(BF16), preferred_element_type=F32)
            out.append((m_new, l))
        return tuple(out)

    init = tuple((jnp.full((1, CHUNK), -jnp.inf, F32), jnp.zeros((1, CHUNK), F32)) for _ in range(ATT_HEADS))
    stats = lax.fori_loop(0, ng, attend, init)
    for h in range(ATT_HEADS):
        o_ref[h * ATT_HD:(h + 1) * ATT_HD, :] = acc_ref[h] / stats[h][1]


def dsa_prompt(aq, ak, av, iq, iw, ik):
    seq = aq.shape[0]
    assert seq % DSA_GROUP == 0
    nb = seq // CHUNK
    topk = min(TOPK_MAX, seq // 4)
    iq3 = iq.reshape(seq, IDX_HEADS, IDX_DIM)
    q_hi = iq3.astype(BF16)
    q_lo = (iq3 - q_hi.astype(F32)).astype(BF16)
    qx = jnp.concatenate([q_hi, q_lo, q_hi, q_lo], axis=-1)
    qit = qx.reshape(seq, IDX_HEADS * 4 * IDX_DIM).T.reshape(IDX_HEADS, 4 * IDX_DIM, seq)
    k_hi = ik.astype(BF16)
    k_lo = (ik - k_hi.astype(F32)).astype(BF16)
    kix = jnp.concatenate([k_hi, k_hi, k_lo, k_lo], axis=-1)
    w_t = (iw * (IDX_DIM ** -0.5 * IDX_HEADS ** -0.5)).T
    qt = (aq * ATT_HD ** -0.5).T.astype(BF16)
    kb = ak.astype(BF16)
    vt = av.reshape(seq // DSA_GROUP, DSA_GROUP, ATT_W).transpose(0, 2, 1).astype(BF16)
    resident = lambda shape: pl.BlockSpec(shape, lambda i: (0,) * len(shape))
    o_t = pl.pallas_call(
        functools.partial(_dsa_prompt_kernel, topk=topk),
        grid=(nb,),
        in_specs=[resident(kix.shape),
                  pl.BlockSpec((IDX_HEADS, 4 * IDX_DIM, CHUNK), lambda i: (0, 0, i)),
                  pl.BlockSpec((IDX_HEADS, CHUNK), lambda i: (0, i)),
                  resident(kb.shape), resident(vt.shape),
                  pl.BlockSpec((ATT_W, CHUNK), lambda i: (0, i))],
        out_specs=pl.BlockSpec((ATT_W, CHUNK), lambda i: (0, i)),
        out_shape=jax.ShapeDtypeStruct((ATT_W, seq), F32),
        scratch_shapes=[pltpu.VMEM((seq, CHUNK), I32), pltpu.VMEM((4 * IDX_DIM, IDX_HEADS * CHUNK), BF16),
                        pltpu.VMEM((ATT_HEADS, ATT_HD, CHUNK), F32)],
        compiler_params=_cparams(("arbitrary",), 56 * 1024 * 1024),
    )(kix, qit, w_t, kb, vt, qt)
    return o_t.T


def _pmem_kernel(x_ref, g_ref, wq_ref, mk_ref, mv_ref, wo_ref, o_ref):
    q = jnp.dot(_rmsnorm_bf16(x_ref[...], g_ref[...]), wq_ref[...], preferred_element_type=F32)
    outs = []
    for h in range(MEM_HEADS):
        hs = slice(h * MEM_HD, (h + 1) * MEM_HD)
        s = lax.dot_general(q[:, hs].astype(BF16), mk_ref[:, hs], NT_DIMS, preferred_element_type=F32) * MEM_HD ** -0.5
        e = jnp.exp(s - jnp.max(s, axis=-1, keepdims=True))
        oh = jnp.dot(e.astype(BF16), mv_ref[:, hs], preferred_element_type=F32) / jnp.sum(e, axis=-1, keepdims=True)
        outs.append(oh.astype(BF16))
    o = jnp.concatenate(outs, axis=1)
    o_ref[...] = x_ref[...] + jnp.dot(o, wo_ref[...], preferred_element_type=F32)


def prompt_mem_attend(x, g, wq, mk, mv, wo, tm=512):
    m, d = x.shape
    n_mem = mk.shape[0]
    full = lambda shape: pl.BlockSpec(shape, lambda i: (0,) * len(shape))
    return pl.pallas_call(
        _pmem_kernel,
        grid=(m // tm,),
        in_specs=[pl.BlockSpec((tm, d), lambda i: (i, 0)), full((1, d)), full((d, MEM_W)), full((n_mem, MEM_W)),
                  full((n_mem, MEM_W)), full((MEM_W, d))],
        out_specs=pl.BlockSpec((tm, d), lambda i: (i, 0)),
        out_shape=jax.ShapeDtypeStruct((m, d), F32),
        compiler_params=_cparams(("parallel",), 48 * 1024 * 1024),
    )(x, g.reshape(1, d), wq, mk, mv, wo)


def _conv_seq(a, prev, cw, cb, row):
    p1 = prev[7:8, :]
    p2 = prev[6:7, :]
    r1 = jnp.where(row == 0, p1, pltpu.roll(a, 1, axis=0))
    r2 = jnp.where(row == 0, p2, jnp.where(row == 1, p1, pltpu.roll(a, 2, axis=0)))
    return cb + cw[0:1, :] * r2 + cw[1:2, :] * r1 + cw[2:3, :] * a


def _ffn_seq_kernel(x_ref, g_ref, wg_ref, wu_ref, cwg_ref, cwu_ref, cbg_ref, cbu_ref, wd_ref,
                    o_ref, lg_ref, lu_ref, carry_ref, h_ref):
    i = pl.program_id(0)
    j = pl.program_id(1)
    tm = x_ref.shape[0]
    row = lax.broadcasted_iota(I32, (tm, 1), 0)

    @pl.when(j == 0)
    def _():
        h_ref[...] = _rmsnorm_bf16(x_ref[...], g_ref[...])
        o_ref[...] = x_ref[...]

    @pl.when(i == 0)
    def _():
        carry_ref[j] = jnp.zeros(carry_ref.shape[1:], F32)

    hx = h_ref[...]
    f = None
    for s in range(FFN_TF // FFN_SUB):
        cs = slice(s * FFN_SUB, (s + 1) * FFN_SUB)
        ag = jnp.dot(hx, wg_ref[:, cs], preferred_element_type=F32)
        au = jnp.dot(hx, wu_ref[:, cs], preferred_element_type=F32)
        lg_ref[0, :, cs] = ag[tm - 8:tm, :]
        lu_ref[0, :, cs] = au[tm - 8:tm, :]
        cg = _conv_seq(ag, carry_ref[j, 0, :, cs], cwg_ref[:, cs], cbg_ref[:, cs], row)
        cu = _conv_seq(au, carry_ref[j, 1, :, cs], cwu_ref[:, cs], cbu_ref[:, cs], row)
        carry_ref[j, 0, :, cs] = ag[tm - 8:tm, :]
        carry_ref[j, 1, :, cs] = au[tm - 8:tm, :]
        act = ((cg * jax.nn.sigmoid(cg)) * cu).astype(BF16)
        fs = jnp.dot(act, wd_ref[cs, :], preferred_element_type=F32)
        f = fs if f is None else f + fs
    o_ref[...] += f


def ffn_seq(x, g, ffn_w, tm=512):
    wg, wu, cwg, cwu, cbg, cbu, wd = ffn_w
    m, d = x.shape
    tf = FFN_TF
    nf = D_FF_PAD // tf
    col = lambda r: pl.BlockSpec((r, tf), lambda i, j: (0, j))
    tail = pl.BlockSpec((1, 8, tf), lambda i, j: (i, 0, j))
    return pl.pallas_call(
        _ffn_seq_kernel,
        grid=(m // tm, nf),
        in_specs=[pl.BlockSpec((tm, d), lambda i, j: (i, 0)), pl.BlockSpec((1, d), lambda i, j: (0, 0)),
                  col(d), col(d), col(CONV_W), col(CONV_W), col(1), col(1),
                  pl.BlockSpec((tf, d), lambda i, j: (j, 0))],
        out_specs=[pl.BlockSpec((tm, d), lambda i, j: (i, 0)), tail, tail],
        out_shape=[jax.ShapeDtypeStruct((m, d), F32), jax.ShapeDtypeStruct((m // tm, 8, D_FF_PAD), F32),
                   jax.ShapeDtypeStruct((m // tm, 8, D_FF_PAD), F32)],
        scratch_shapes=[pltpu.VMEM((nf, 2, 8, tf), F32), pltpu.VMEM((tm, d), BF16)],
        compiler_params=_cparams(("arbitrary", "arbitrary"), 48 * 1024 * 1024),
    )(x, g.reshape(1, d), wg, wu, cwg, cwu, cbg, cbu, wd)


def _ffn_step_kernel(x_ref, g_ref, wg_ref, wu_ref, cwg_ref, cwu_ref, cbg_ref, cbu_ref, wd_ref,
                     pg0_ref, pg1_ref, pu0_ref, pu1_ref, o_ref, ag_ref, au_ref, h_ref):
    j = pl.program_id(0)

    @pl.when(j == 0)
    def _():
        h_ref[...] = _rmsnorm_bf16(x_ref[...], g_ref[...])
        o_ref[...] = x_ref[...]

    hx = h_ref[...]
    ag = jnp.dot(hx, wg_ref[...], preferred_element_type=F32)
    au = jnp.dot(hx, wu_ref[...], preferred_element_type=F32)
    ag_ref[...] = ag
    au_ref[...] = au
    cg = cbg_ref[...] + cwg_ref[0:1, :] * pg0_ref[...] + cwg_ref[1:2, :] * pg1_ref[...] + cwg_ref[2:3, :] * ag
    cu = cbu_ref[...] + cwu_ref[0:1, :] * pu0_ref[...] + cwu_ref[1:2, :] * pu1_ref[...] + cwu_ref[2:3, :] * au
    act = ((cg * jax.nn.sigmoid(cg)) * cu).astype(BF16)
    o_ref[...] += jnp.dot(act, wd_ref[...], preferred_element_type=F32)


def ffn_step(x, g, ffn_w, prev):
    wg, wu, cwg, cwu, cbg, cbu, wd = ffn_w
    m, d = x.shape
    tf = FFN_TF
    nf = D_FF_PAD // tf
    col = lambda r: pl.BlockSpec((r, tf), lambda j: (0, j))
    row = pl.BlockSpec((m, d), lambda j: (0, 0))
    return pl.pallas_call(
        _ffn_step_kernel,
        grid=(nf,),
        in_specs=[row, pl.BlockSpec((1, d), lambda j: (0, 0)), col(d), col(d), col(CONV_W), col(CONV_W), col(1), col(1),
                  pl.BlockSpec((tf, d), lambda j: (j, 0)), col(m), col(m), col(m), col(m)],
        out_specs=[row, col(m), col(m)],
        out_shape=[jax.ShapeDtypeStruct((m, d), F32), jax.ShapeDtypeStruct((m, D_FF_PAD), F32),
                   jax.ShapeDtypeStruct((m, D_FF_PAD), F32)],
        scratch_shapes=[pltpu.VMEM((m, d), BF16)],
        compiler_params=_cparams(("arbitrary",), 48 * 1024 * 1024),
    )(x, g.reshape(1, d), wg, wu, cwg, cwu, cbg, cbu, wd, *prev)


def _smix_kernel(qc_ref, kc_ref, v_ref, rg_ref, gu_ref, gv_ref, st_ref, cos_ref, sin_ref, g1_ref, retg_ref,
                 gmg_ref, wrow_ref, brow_ref, a_ref, c_ref, ns_ref, gmv_ref):
    cosc = cos_ref[...]
    sinc = sin_ref[...]
    half = RET_DK // 2

    def rot(x):
        x1, x2 = x[:half], x[half:]
        return jnp.concatenate([x1 * cosc - x2 * sinc, x1 * sinc + x2 * cosc], axis=0)

    for h in range(RET_HEADS):
        vs = slice(h * RET_DV, (h + 1) * RET_DV)
        qr = rot(qc_ref[0, h])
        kr = rot(kc_ref[0, h]) * RET_DK ** -0.5
        v = v_ref[0, :, vs]
        state = st_ref[0, h]
        g1 = g1_ref[h]
        inner = jnp.sum(qr * kr, axis=0, keepdims=True) * v
        cross = jnp.sum(qr * state, axis=0, keepdims=True) * g1
        o = inner + cross
        ns_ref[0, h] = g1 * state + kr * v
        ms = jnp.mean(o * o, axis=-1, keepdims=True)
        y = o * lax.rsqrt(ms + EPS) * retg_ref[:, vs]
        g = rg_ref[0, :, vs]
        a_ref[0, :, vs] = y * (g * jax.nn.sigmoid(g))

    u = jax.nn.gelu(gu_ref[0])
    v = jax.nn.gelu(gv_ref[0])
    ms = jnp.mean(v * v, axis=-1, keepdims=True)
    vn = v * lax.rsqrt(ms + EPS) * gmg_ref[...]
    c_ref[0] = u * (wrow_ref[...] * vn + brow_ref[...])
    gmv_ref[0] = vn


def sample_mixers(p_main, state, consts, retg_row, gmg_row, wrow, brow):
    nb = p_main.shape[0]
    cosc, sinc, g1 = consts
    qc = p_main[:, 0:RET_QK].reshape(nb, RET_HEADS, RET_DK, 1)
    kc = p_main[:, RET_QK:2 * RET_QK].reshape(nb, RET_HEADS, RET_DK, 1)
    p3 = p_main.reshape(nb, 1, p_main.shape[1])
    per_b = lambda width, blk: pl.BlockSpec((1, 1, width), lambda b: (b, 0, blk))
    col4 = pl.BlockSpec((1, RET_HEADS, RET_DK, 1), lambda b: (b, 0, 0, 0))
    st4 = pl.BlockSpec((1, RET_HEADS, RET_DK, RET_DV), lambda b: (b, 0, 0, 0))
    full = lambda shape: pl.BlockSpec(shape, lambda b: (0,) * len(shape))
    return pl.pallas_call(
        _smix_kernel,
        grid=(nb,),
        in_specs=[col4, col4, per_b(1024, 1), per_b(1024, 2), per_b(512, _M_GU // 512), per_b(512, _M_GV // 512), st4,
                  full(cosc.shape), full(sinc.shape), full(g1.shape), full(retg_row.shape), full(gmg_row.shape),
                  full(wrow.shape), full(brow.shape)],
        out_specs=[per_b(RET_V, 0), per_b(GM_W, 0), st4, per_b(GM_W, 0)],
        out_shape=[jax.ShapeDtypeStruct((nb, 1, RET_V), F32), jax.ShapeDtypeStruct((nb, 1, GM_W), F32),
                   jax.ShapeDtypeStruct(state.shape, F32), jax.ShapeDtypeStruct((nb, 1, GM_W), F32)],
        compiler_params=_cparams(("parallel",)),
    )(qc, kc, p3, p3, p3, p3, state, cosc, sinc, g1, retg_row, gmg_row, wrow, brow)


def _page_scores_kernel(pt_ref, qx_ref, w_ref, q8_ref, *refs, n_pg):
    ki_refs = refs[:n_pg]
    k_refs = refs[n_pg:2 * n_pg]
    isc_ref, asc_ref = refs[2 * n_pg:]
    qx = qx_ref[0]
    q_hi = qx[:IDX_HEADS]
    w = w_ref[0]
    q8 = q8_ref[0]
    for n in range(n_pg):
        kp = ki_refs[n][0]
        k_hi = kp.astype(BF16)
        k_lo = (kp - k_hi.astype(F32)).astype(BF16)
        s2 = lax.dot_general(qx, k_hi, NT_DIMS, preferred_element_type=F32)
        s = s2[:IDX_HEADS] + s2[IDX_HEADS:] + lax.dot_general(q_hi, k_lo, NT_DIMS, preferred_element_type=F32)
        isc_ref[0, n] = jnp.sum(jnp.maximum(s, 0.0) * w, axis=0, keepdims=True)
        asc_ref[0, n] = lax.dot_general(q8, k_refs[n][0].astype(BF16), NT_DIMS, preferred_element_type=F32)


def page_scores(page_table, qx, w_col, q8, pool_ki, pool_k, n_pg):
    nb, n_pages = page_table.shape
    groups = n_pages // n_pg
    pt = page_table.reshape(-1)

    def pool_spec(shape, n):
        return pl.BlockSpec((1,) + shape, lambda b, g, pt_ref: (pt_ref[b * n_pages + g * n_pg + n], 0, 0))

    per_b = lambda shape: pl.BlockSpec((1,) + shape, lambda b, g, pt_ref: (b,) + (0,) * len(shape))
    grid_spec = pltpu.PrefetchScalarGridSpec(
        num_scalar_prefetch=1,
        grid=(nb, groups),
        in_specs=[per_b((2 * IDX_HEADS, IDX_DIM)), per_b((IDX_HEADS, 1)), per_b((8, ATT_HD))]
        + [pool_spec((PAGE_SIZE, IDX_DIM), n) for n in range(n_pg)]
        + [pool_spec((PAGE_ROWS, ATT_HD), n) for n in range(n_pg)],
        out_specs=[pl.BlockSpec((1, n_pg, 1, PAGE_SIZE), lambda b, g, pt_ref: (b, g, 0, 0)),
                   pl.BlockSpec((1, n_pg, 8, PAGE_ROWS), lambda b, g, pt_ref: (b, g, 0, 0))],
    )
    return pl.pallas_call(
        functools.partial(_page_scores_kernel, n_pg=n_pg),
        grid_spec=grid_spec,
        out_shape=[jax.ShapeDtypeStruct((nb, n_pages, 1, PAGE_SIZE), F32),
                   jax.ShapeDtypeStruct((nb, n_pages, 8, PAGE_ROWS), F32)],
        compiler_params=_cparams(("arbitrary", "arbitrary")),
    )(pt, qx, w_col, q8, *([pool_ki] * n_pg), *([pool_k] * n_pg))


def _select_softmax_kernel(isc_ref, isce_ref, asc_ref, p_ref, keys_ref, thr_ref, keep_ref, *, topk, n_valid):
    h = pl.program_id(0)
    nb = isc_ref.shape[0]

    @pl.when(h == 0)
    def _():
        colid = lax.broadcasted_iota(I32, isc_ref.shape, 1)
        keys_ref[...] = jnp.where(colid < n_valid, _sortable_key(isc_ref[...]), INT_MIN)

        def count_ge(mid):
            return jnp.sum((keys_ref[...] >= mid).astype(I32), axis=1, keepdims=True)

        t_k = _bisect_threshold(count_ge, jnp.full((nb, 1), INT_MIN, I32), jnp.full((nb, 1), INT_MAX, I32), topk)
        thr_ref[...] = jnp.broadcast_to(jnp.maximum(t_k, INT_MIN + 1), thr_ref.shape)
        keep_ref[...] = jnp.full(keep_ref.shape, INT_MAX, I32)

        over = jnp.logical_and(count_ge(t_k) > topk, t_k > INT_MIN)

        @pl.when(jnp.max(over.astype(I32)) > 0)
        def _():
            need = topk - count_ge(t_k + 1)

            def count_tied_upto(j):
                tied = jnp.where(keys_ref[...] == t_k, jnp.where(colid <= j, 1, 0), 0)
                return jnp.sum(tied, axis=1, keepdims=True)

            def step(_, lh):
                lo_j, hi_j = lh
                mid = (lo_j + hi_j) >> 1
                ok = count_tied_upto(mid) >= need
                return jnp.where(ok, lo_j, mid), jnp.where(ok, mid, hi_j)

            n_keys = isc_ref.shape[1]
            _, j_keep = lax.fori_loop(0, n_keys.bit_length(), step,
                                      (jnp.full((nb, 1), -1, I32), jnp.full((nb, 1), n_keys - 1, I32)))
            keep_ref[...] = jnp.broadcast_to(jnp.where(over, j_keep, INT_MAX), keep_ref.shape)

    ce = lax.broadcasted_iota(I32, isce_ref.shape, 1)
    ke = _sortable_key(isce_ref[...])
    thr = thr_ref[:, 0:1]
    key_pos = ce >> (ATT_HEADS.bit_length() - 1)
    tied_ok = jnp.where(key_pos <= keep_ref[:, 0:1], asc_ref[0], -jnp.inf)
    s = jnp.where(ke > thr, asc_ref[0], jnp.where(ke == thr, tied_ok, -jnp.inf))
    s = jnp.where(ce < ATT_HEADS * n_valid, s, -jnp.inf)
    s = jnp.where((ce & (ATT_HEADS - 1)) == h, s, -jnp.inf)
    e = jnp.exp(s - jnp.max(s, axis=1, keepdims=True))
    p_ref[0] = e / jnp.sum(e, axis=1, keepdims=True)


def select_softmax(isc, isc_e, asc, topk, n_valid):
    nb = isc.shape[0]
    head_blk = pl.BlockSpec((1,) + asc.shape[1:], lambda h: (h, 0, 0))
    whole = lambda a: pl.BlockSpec(a.shape, lambda h: (0, 0))
    return pl.pallas_call(
        functools.partial(_select_softmax_kernel, topk=topk, n_valid=n_valid),
        grid=(ATT_HEADS,),
        in_specs=[whole(isc), whole(isc_e), head_blk],
        out_specs=head_blk,
        out_shape=jax.ShapeDtypeStruct(asc.shape, F32),
        scratch_shapes=[pltpu.VMEM(isc.shape, I32), pltpu.VMEM((nb, LANE), I32), pltpu.VMEM((nb, LANE), I32)],
        compiler_params=_cparams(("arbitrary",), 48 * 1024 * 1024),
    )(isc, isc_e, asc)


def _page_values_kernel(pt_ref, p_ref, *refs, n_pg):
    v_refs = refs[:n_pg]
    o_ref = refs[n_pg]
    acc = jnp.zeros(o_ref.shape[1:], F32)
    for n in range(n_pg):
        acc = acc + jnp.dot(p_ref[0, n], v_refs[n][0].astype(BF16), preferred_element_type=F32)

    @pl.when(pl.program_id(1) == 0)
    def _():
        o_ref[0] = acc

    @pl.when(pl.program_id(1) > 0)
    def _():
        o_ref[0] += acc


def page_values(page_table, p_pages, pool_v, n_pg):
    nb, n_pages = page_table.shape
    groups = n_pages // n_pg
    pt = page_table.reshape(-1)

    def pool_spec(n):
        return pl.BlockSpec((1, PAGE_ROWS, ATT_HD), lambda b, g, pt_ref: (pt_ref[b * n_pages + g * n_pg + n], 0, 0))

    grid_spec = pltpu.PrefetchScalarGridSpec(
        num_scalar_prefetch=1,
        grid=(nb, groups),
        in_specs=[pl.BlockSpec((1, n_pg, 8, PAGE_ROWS), lambda b, g, pt_ref: (b, g, 0, 0))]
        + [pool_spec(n) for n in range(n_pg)],
        out_specs=pl.BlockSpec((1, 8, ATT_HD), lambda b, g, pt_ref: (b, 0, 0)),
    )
    return pl.pallas_call(
        functools.partial(_page_values_kernel, n_pg=n_pg),
        grid_spec=grid_spec,
        out_shape=jax.ShapeDtypeStruct((nb, 8, ATT_HD), F32),
        compiler_params=_cparams(("arbitrary", "arbitrary")),
    )(pt, p_pages, *([pool_v] * n_pg))


def _head_rows(q, scale):
    nb = q.shape[0]
    q3 = (q * scale).reshape(nb, ATT_HEADS, ATT_HD)
    return jnp.pad(q3, ((0, 0), (0, 8 - ATT_HEADS), (0, 0))).astype(BF16)


def dsa_sample(aq, ak, av, iq, iw, ik, pool_k, pool_v, pool_ki, page_table, n_pg=16):
    nb, n_pages = page_table.shape
    past = n_pages * PAGE_SIZE
    n_keys = past + PAGE_SIZE
    topk = min(TOPK_MAX, (past + 1) // 4)
    iq3 = iq.reshape(nb, IDX_HEADS, IDX_DIM)
    q_hi = iq3.astype(BF16)
    q_lo = (iq3 - q_hi.astype(F32)).astype(BF16)
    qx = jnp.concatenate([q_hi, q_lo], axis=1)
    w_col = (iw * (IDX_DIM ** -0.5 * IDX_HEADS ** -0.5)).reshape(nb, IDX_HEADS, 1)
    q8 = _head_rows(aq, ATT_HD ** -0.5)
    own = jnp.arange(nb, dtype=I32).reshape(nb, 1)
    pad_rows = lambda a, rows: jnp.pad(a, ((0, 0), (0, rows - a.shape[1]), (0, 0)))
    own_ki = pad_rows(ik[:, None, :], PAGE_SIZE)
    own_k = pad_rows(ak.reshape(nb, ATT_HEADS, ATT_HD), PAGE_ROWS)
    own_v = pad_rows(av.reshape(nb, ATT_HEADS, ATT_HD), PAGE_ROWS)
    isc_p, asc_p = page_scores(page_table, qx, w_col, q8, pool_ki, pool_k, n_pg)
    isc_n, asc_n = page_scores(own, qx, w_col, q8, own_ki, own_k, 1)
    isc = jnp.concatenate([isc_p, isc_n], axis=1).reshape(nb, n_keys)
    asc = jnp.concatenate([asc_p, asc_n], axis=1)[:, :, :ATT_HEADS]
    asc = asc.transpose(2, 0, 1, 3).reshape(ATT_HEADS, nb, n_keys * ATT_HEADS)
    p = select_softmax(isc, jnp.repeat(isc, ATT_HEADS, axis=1), asc, topk, past + 1)
    p = p.reshape(ATT_HEADS, nb, n_pages + 1, PAGE_ROWS).transpose(1, 2, 0, 3)
    p = jnp.pad(p, ((0, 0), (0, 0), (0, 8 - ATT_HEADS), (0, 0))).astype(BF16)
    o_p = page_values(page_table, p[:, :n_pages], pool_v, n_pg)
    o_n = page_values(own, p[:, n_pages:], own_v, 1)
    return (o_p + o_n)[:, :ATT_HEADS].reshape(nb, ATT_W)


def _smem_kernel(q8_ref, mk_ref, mv_ref, o_ref):
    rows = mk_ref.shape[1]
    head_of = lambda axis: lax.broadcasted_iota(I32, (8, rows), axis) & (MEM_HEADS - 1)
    own_head = head_of(1) == head_of(0)
    for b in range(q8_ref.shape[0]):
        s = lax.dot_general(q8_ref[b], mk_ref[b].astype(BF16), NT_DIMS, preferred_element_type=F32)
        s = jnp.where(own_head, s, -jnp.inf)
        e = jnp.exp(s - jnp.max(s, axis=-1, keepdims=True))
        p = e / jnp.sum(e, axis=-1, keepdims=True)
        o_ref[b] = jnp.dot(p.astype(BF16), mv_ref[b].astype(BF16), preferred_element_type=F32)


def sample_mem_core(q, mem_k, mem_v, layer, bb=8):
    nb = q.shape[0]
    rows = mem_k.shape[1]
    q8 = _head_rows(q, MEM_HD ** -0.5)
    first = layer * (nb // bb)
    o = pl.pallas_call(
        _smem_kernel,
        grid=(nb // bb,),
        in_specs=[pl.BlockSpec((bb, 8, MEM_HD), lambda i: (i, 0, 0)),
                  pl.BlockSpec((bb, rows, MEM_HD), lambda i: (first + i, 0, 0)),
                  pl.BlockSpec((bb, rows, MEM_HD), lambda i: (first + i, 0, 0))],
        out_specs=pl.BlockSpec((bb, 8, MEM_HD), lambda i: (i, 0, 0)),
        out_shape=jax.ShapeDtypeStruct((nb, 8, MEM_HD), F32),
        compiler_params=_cparams(("parallel",), 48 * 1024 * 1024),
    )(q8, mem_k, mem_v)
    return o[:, :MEM_HEADS].reshape(nb, MEM_W)


def _prep_layer(w_in, w_out, w_mq, w_mk, w_mv, w_mo, w_up, conv_w, conv_b, w_down, gm_ws, gm_bs):
    w_proj = jnp.concatenate([w_in[:, :_O_IQ_END], w_in[:, _O_GU:], w_in[:, _O_IK:_O_GU], w_in[:, _O_IW:_O_IK],
                              jnp.zeros((D_MODEL, N_PROJ - N_MAIN - IDX_DIM - IDX_HEADS), w_in.dtype)],
                             axis=1).astype(BF16)
    padc = lambda a: jnp.pad(a, ((0, 0), (0, D_FF_PAD - D_FF)))
    ffn_w = (padc(w_up[:, :D_FF]).astype(BF16), padc(w_up[:, D_FF:]).astype(BF16),
             padc(conv_w[:, :D_FF]), padc(conv_w[:, D_FF:]),
             padc(conv_b[None, :D_FF]), padc(conv_b[None, D_FF:]),
             jnp.pad(w_down, ((0, D_FF_PAD - D_FF), (0, 0))).astype(BF16))
    causal = jnp.tril(jnp.ones((CHUNK, CHUNK), dtype=bool))
    wtril = jnp.where(causal[None], gm_ws, 0.0).astype(BF16)
    bsm = jnp.broadcast_to(gm_bs[:, :, None], (GM_GROUPS, CHUNK, GM_GW))
    wrow = jnp.repeat(gm_ws[:, 0, 0], GM_GW).reshape(1, GM_W)
    brow = jnp.repeat(gm_bs[:, 0], GM_GW).reshape(1, GM_W)
    return dict(w_proj=w_proj, w_out=w_out.astype(BF16), w_mq=w_mq.astype(BF16),
                w_mkv=jnp.concatenate([w_mk, w_mv], axis=1).astype(BF16), w_mo=w_mo.astype(BF16), ffn_w=ffn_w,
                wtril=wtril, bsm=bsm, wrow=wrow, brow=brow)


def _retention_consts(seq, past):
    log_g = jnp.log(1.0 - 2.0 ** (-5.0 - jnp.arange(RET_HEADS, dtype=F32)))
    half = RET_DK // 2
    inv = ROPE_BASE ** (-jnp.arange(half, dtype=F32) / half)
    ang = jnp.arange(seq).astype(F32)[:, None] * inv[None, :]
    cos, sin = jnp.cos(ang), jnp.sin(ang)
    cosf = jnp.concatenate([cos, cos], axis=1)
    sinf = jnp.concatenate([-sin, sin], axis=1)
    n = jnp.arange(CHUNK, dtype=F32)
    diff = n[:, None] - n[None, :]
    causal = diff >= 0
    dmat = jnp.where(causal[None], jnp.exp(log_g[:, None, None] * jnp.where(causal, diff, 0.0)[None]), 0.0)
    cdec = jnp.broadcast_to(jnp.exp(log_g[:, None] * (n[None, :] + 1.0))[:, :, None], (RET_HEADS, CHUNK, RET_DV))
    kdec = jnp.broadcast_to(jnp.exp(log_g[:, None] * (CHUNK - 1.0 - n[None, :]))[:, :, None], (RET_HEADS, CHUNK, RET_DK))
    gc = jnp.broadcast_to(jnp.exp(log_g * CHUNK)[:, None, None], (RET_HEADS, RET_DK, RET_DV))
    ang_s = jnp.full((1,), past, dtype=F32)[:, None] * inv[None, :]
    cosc = jnp.cos(ang_s).reshape(half, 1)
    sinc = jnp.sin(ang_s).reshape(half, 1)
    g1 = jnp.broadcast_to(jnp.exp(log_g * 1.0)[:, None, None], (RET_HEADS, 1, RET_DV))
    return (cosf, sinf, dmat, cdec, kdec, gc), (cosc, sinc, g1)


def _pad_ff(a):
    return jnp.pad(a, ((0, 0), (0, D_FF_PAD - D_FF)))


def kernel(x_prompt, x_sample, mem_prompt, cache_k, cache_v, cache_kidx, page_table, cache_mem_k, cache_mem_v, state_ret, state_conv, norm_mix_g, w_in, ret_norm_g, gm_norm_g, gm_ws, gm_bs, w_out, norm_mem_g, mem_in_g, w_mq, w_mk, w_mv, w_mo, norm_ffn_g, w_up, conv_w, conv_b, w_down, final_norm_g):
    depth = w_in.shape[0]
    seq = x_prompt.shape[1]
    nbd = x_sample.shape[0]
    n_mem = mem_prompt.shape[1]
    past = page_table.shape[1] * PAGE_SIZE
    p_consts, s_consts = _retention_consts(seq, past)
    n_pool = cache_k.shape[1]
    pool_k = cache_k.reshape(depth * n_pool, PAGE_ROWS, ATT_HD)
    pool_v = cache_v.reshape(depth * n_pool, PAGE_ROWS, ATT_HD)
    pool_ki = cache_kidx.reshape(depth * n_pool, PAGE_SIZE, IDX_DIM)
    mem_k_all = cache_mem_k.reshape(depth * nbd, n_mem * MEM_HEADS, MEM_HD)
    mem_v_all = cache_mem_v.reshape(depth * nbd, n_mem * MEM_HEADS, MEM_HD)

    xp = x_prompt.reshape(seq, D_MODEL)
    xs = x_sample.reshape(nbd, D_MODEL)
    mem = mem_prompt.reshape(n_mem, D_MODEL)
    outs = {k: [] for k in ("kp", "vp", "kip", "ks", "vs", "kis", "mkp", "mvp", "rsp", "rss", "csp", "css", "gvp", "gvs")}

    for l in range(depth):
        w = _prep_layer(w_in[l], w_out[l], w_mq[l], w_mk[l], w_mv[l], w_mo[l], w_up[l], conv_w[l], conv_b[l],
                        w_down[l], gm_ws[l], gm_bs[l])
        retg_row = ret_norm_g[l].reshape(1, RET_V)
        gmg_row = gm_norm_g[l].reshape(1, GM_W)

        pm = norm_matmul(xp, norm_mix_g[l], w["w_proj"])
        ak, av = pm[:, _M_AK:_M_AV], pm[:, _M_AV:_M_IQ]
        ik = pm[:, _M_IK:_M_IW]
        a_out, c_out, r_state, gm_v = prompt_mixers(pm, p_consts, retg_row, gmg_row, w["wtril"], w["bsm"])
        b_out = dsa_prompt(pm[:, _M_AQ:_M_AK], ak, av, pm[:, _M_IQ:_M_GU], pm[:, _M_IW:_M_IW + IDX_HEADS], ik)
        xp = out_projection(a_out, b_out, c_out, w["w_out"], xp)
        mkv = norm_matmul(mem, mem_in_g[l], w["w_mkv"])
        mk, mv = mkv[:, :MEM_W], mkv[:, MEM_W:]
        xp = prompt_mem_attend(xp, norm_mem_g[l], w["w_mq"], mk.astype(BF16), mv.astype(BF16), w["w_mo"])
        xp, tail_g, tail_u = ffn_seq(xp, norm_ffn_g[l], w["ffn_w"])
        last_g, last_u = tail_g[-1], tail_u[-1]
        outs["kp"].append(ak.reshape(1, seq, ATT_HEADS, ATT_HD))
        outs["vp"].append(av.reshape(1, seq, ATT_HEADS, ATT_HD))
        outs["kip"].append(ik.reshape(1, seq, IDX_DIM))
        outs["mkp"].append(mk.reshape(1, n_mem, MEM_HEADS, MEM_HD))
        outs["mvp"].append(mv.reshape(1, n_mem, MEM_HEADS, MEM_HD))
        outs["rsp"].append(r_state.reshape(1, RET_HEADS, RET_DK, RET_DV))
        outs["csp"].append(jnp.concatenate([last_g[8 - (CONV_W - 1):, :D_FF], last_u[8 - (CONV_W - 1):, :D_FF]], axis=1)[None])
        outs["gvp"].append(gm_v[None])

        pm = norm_matmul(xs, norm_mix_g[l], w["w_proj"])
        ak, av = pm[:, _M_AK:_M_AV], pm[:, _M_AV:_M_IQ]
        ik = pm[:, _M_IK:_M_IW]
        a_out, c_out, r_state, gm_v = sample_mixers(pm, state_ret[l], s_consts, retg_row, gmg_row, w["wrow"], w["brow"])
        b_out = dsa_sample(pm[:, _M_AQ:_M_AK], ak, av, pm[:, _M_IQ:_M_GU], pm[:, _M_IW:_M_IW + IDX_HEADS], ik,
                           pool_k, pool_v, pool_ki, page_table + l * n_pool)
        xs = out_projection(a_out.reshape(nbd, RET_V), b_out, c_out.reshape(nbd, GM_W), w["w_out"], xs)
        q = norm_matmul(xs, norm_mem_g[l], w["w_mq"])
        o = sample_mem_core(q, mem_k_all, mem_v_all, l)
        xs = matmul(o.astype(BF16), w["w_mo"], res=xs)
        sc = state_conv[l]
        prev = (_pad_ff(sc[:, 0, :D_FF]), _pad_ff(sc[:, 1, :D_FF]), _pad_ff(sc[:, 0, D_FF:]), _pad_ff(sc[:, 1, D_FF:]))
        xs, a_g, a_u = ffn_step(xs, norm_ffn_g[l], w["ffn_w"], prev)
        a_new = jnp.concatenate([a_g[:, :D_FF], a_u[:, :D_FF]], axis=1)
        outs["ks"].append(ak.reshape(nbd, 1, ATT_HEADS, ATT_HD))
        outs["vs"].append(av.reshape(nbd, 1, ATT_HEADS, ATT_HD))
        outs["kis"].append(ik.reshape(nbd, 1, IDX_DIM))
        outs["rss"].append(r_state)
        outs["css"].append(jnp.stack([sc[:, 1, :], a_new], axis=1))
        outs["gvs"].append(gm_v)

    y_prompt = rmsnorm_rows(xp, final_norm_g, F32).reshape(1, seq, D_MODEL)
    y_sample = rmsnorm_rows(xs, final_norm_g, F32).reshape(nbd, 1, D_MODEL)
    st = lambda k: jnp.stack(outs[k])
    return (y_prompt, y_sample, st("kp"), st("vp"), st("kip"), st("ks"), st("vs"), st("kis"), st("mkp"), st("mvp"),
            st("rsp"), st("rss"), st("csp"), st("css"), st("gvp"), st("gvs"))
```

```python
import functools

import jax
import jax.numpy as jnp
from jax import lax
from jax.experimental import pallas as pl
from jax.experimental.pallas import tpu as pltpu

F32 = jnp.float32
BF16 = jnp.bfloat16
I32 = jnp.int32

D_MODEL = 2048
PAGE_SIZE = 128
RET_HEADS = 4
RET_DK = 128
RET_DV = 256
CHUNK = 128
ROPE_BASE = 10000.0
ATT_HEADS = 4
ATT_HD = 128
IDX_HEADS = 16
IDX_DIM = 64
TOPK_MAX = 256
GM_GROUPS = 4
GM_GW = 128
MEM_HEADS = 4
MEM_HD = 128
D_FF = 5504
CONV_W = 3
EPS = 1e-6

RET_QK = RET_HEADS * RET_DK
RET_V = RET_HEADS * RET_DV
ATT_W = ATT_HEADS * ATT_HD
IDX_Q = IDX_HEADS * IDX_DIM
GM_W = GM_GROUPS * GM_GW
MEM_W = MEM_HEADS * MEM_HD
PAGE_ROWS = PAGE_SIZE * ATT_HEADS

_O_IQ_END = 2 * RET_QK + 2 * RET_V + 3 * ATT_W + IDX_Q
_O_IW = _O_IQ_END
_O_IK = _O_IW + IDX_HEADS
_O_GU = _O_IK + IDX_DIM
_M_AQ = 2 * RET_QK + 2 * RET_V
_M_AK = _M_AQ + ATT_W
_M_AV = _M_AK + ATT_W
_M_IQ = _M_AV + ATT_W
_M_GU = _M_IQ + IDX_Q
_M_GV = _M_GU + GM_W
N_MAIN = _M_GV + GM_W
_M_IK = N_MAIN
_M_IW = _M_IK + IDX_DIM
N_PROJ = N_MAIN + 512
LANE = 128
D_FF_PAD = 5632
FFN_TF = 512
FFN_SUB = 512
DSA_GROUP = 512
DSA_SCORE_ROWS = 256
BISECT_STEPS_PER_CHECK = 4

INT_MIN = -2 ** 31
INT_MAX = 2 ** 31 - 1
V7X_VMEM_BYTES = 64 * 1024 * 1024
NT_DIMS = (((1,), (1,)), ((), ()))


def _cparams(semantics, vmem_bytes=None):
    return pltpu.CompilerParams(dimension_semantics=semantics, vmem_limit_bytes=vmem_bytes)


def _sortable_key(x):
    b = lax.bitcast_convert_type(x, I32)
    return jnp.where(x == 0.0, 0, b ^ ((b >> 31) & INT_MAX))


def _bisect_threshold(count_ge, lo, hi, topk):
    def cond(st):
        it, active, _, _ = st
        return jnp.logical_and(it < 32, active > 0)

    def body(st):
        it, _, lo, hi = st
        for _ in range(BISECT_STEPS_PER_CHECK):
            mid = (lo >> 1) + (hi >> 1) + (lo & hi & 1)
            cnt = count_ge(mid)
            ge = cnt >= topk
            lo = jnp.where(ge, mid, lo)
            hi = jnp.where(cnt == topk, mid + 1, jnp.where(ge, hi, mid))
        active = jnp.max((hi - 1 > lo).astype(I32))
        return it + BISECT_STEPS_PER_CHECK, active, lo, hi

    _, _, lo, _ = lax.while_loop(cond, body, (jnp.int32(0), jnp.int32(1), lo, hi))
    return lo


def _rmsnorm_bf16(x, g):
    ms = jnp.mean(x * x, axis=-1, keepdims=True)
    return (x * lax.rsqrt(ms + EPS) * g).astype(BF16)


def _norm_kernel(x_ref, g_ref, o_ref):
    x = x_ref[...]
    ms = jnp.mean(x * x, axis=-1, keepdims=True)
    o_ref[...] = (x * lax.rsqrt(ms + EPS) * g_ref[...]).astype(o_ref.dtype)


def rmsnorm_rows(x, g, out_dtype):
    m, d = x.shape
    tm = min(m, 256)
    return pl.pallas_call(
        _norm_kernel,
        grid=(m // tm,),
        in_specs=[pl.BlockSpec((tm, d), lambda i: (i, 0)), pl.BlockSpec((1, d), lambda i: (0, 0))],
        out_specs=pl.BlockSpec((tm, d), lambda i: (i, 0)),
        out_shape=jax.ShapeDtypeStruct((m, d), out_dtype),
        compiler_params=_cparams(("parallel",)),
    )(x, g.reshape(1, d))


def _mm_kernel(x_ref, w_ref, o_ref):
    o_ref[...] = jnp.dot(x_ref[...], w_ref[...], preferred_element_type=F32)


def _mm_res_kernel(x_ref, w_ref, r_ref, o_ref):
    o_ref[...] = r_ref[...] + jnp.dot(x_ref[...], w_ref[...], preferred_element_type=F32)


def matmul(x, w, res=None, tm=1024, tn=512):
    m, k = x.shape
    n = w.shape[1]
    tm = min(tm, m)
    tn = min(tn, n)
    assert m % tm == 0 and n % tn == 0
    in_specs = [pl.BlockSpec((tm, k), lambda i, j: (i, 0)), pl.BlockSpec((k, tn), lambda i, j: (0, j))]
    args = [x, w]
    body = _mm_kernel
    if res is not None:
        in_specs.append(pl.BlockSpec((tm, tn), lambda i, j: (i, j)))
        args.append(res)
        body = _mm_res_kernel
    return pl.pallas_call(
        body,
        grid=(m // tm, n // tn),
        in_specs=in_specs,
        out_specs=pl.BlockSpec((tm, tn), lambda i, j: (i, j)),
        out_shape=jax.ShapeDtypeStruct((m, n), F32),
        compiler_params=_cparams(("parallel", "parallel"), 48 * 1024 * 1024),
    )(*args)


def _out_proj_kernel(a_ref, b_ref, c_ref, wa_ref, wb_ref, wc_ref, r_ref, o_ref):
    acc = jnp.dot(a_ref[...].astype(BF16), wa_ref[...], preferred_element_type=F32)
    acc = acc + jnp.dot(b_ref[...].astype(BF16), wb_ref[...], preferred_element_type=F32)
    acc = acc + jnp.dot(c_ref[...].astype(BF16), wc_ref[...], preferred_element_type=F32)
    o_ref[...] = r_ref[...] + acc


def out_projection(a, b, c, w_out, res, tm=1024, tn=512):
    m = a.shape[0]
    n = w_out.shape[1]
    tm = min(tm, m)
    ka, kb, kc = a.shape[1], b.shape[1], c.shape[1]
    assert m % tm == 0 and n % tn == 0 and ka % kb == 0 and kb == kc
    rows = lambda k: pl.BlockSpec((tm, k), lambda i, j: (i, 0))
    return pl.pallas_call(
        _out_proj_kernel,
        grid=(m // tm, n // tn),
        in_specs=[rows(ka), rows(kb), rows(kc),
                  pl.BlockSpec((ka, tn), lambda i, j: (0, j)),
                  pl.BlockSpec((kb, tn), lambda i, j: (ka // kb, j)),
                  pl.BlockSpec((kc, tn), lambda i, j: (ka // kb + 1, j)),
                  pl.BlockSpec((tm, tn), lambda i, j: (i, j))],
        out_specs=pl.BlockSpec((tm, tn), lambda i, j: (i, j)),
        out_shape=jax.ShapeDtypeStruct((m, n), F32),
        compiler_params=_cparams(("parallel", "parallel"), 48 * 1024 * 1024),
    )(a, b, c, w_out, w_out, w_out, res)


def _norm_mm_kernel(x_ref, g_ref, w_ref, o_ref, h_ref):
    @pl.when(pl.program_id(1) == 0)
    def _():
        h_ref[...] = _rmsnorm_bf16(x_ref[...], g_ref[...])

    o_ref[...] = jnp.dot(h_ref[...], w_ref[...], preferred_element_type=F32)


def norm_matmul(x, g, w, tm=1024, tn=512):
    m, k = x.shape
    n = w.shape[1]
    tm = min(tm, m)
    tn = min(tn, n)
    assert m % tm == 0 and n % tn == 0
    return pl.pallas_call(
        _norm_mm_kernel,
        grid=(m // tm, n // tn),
        in_specs=[pl.BlockSpec((tm, k), lambda i, j: (i, 0)), pl.BlockSpec((1, k), lambda i, j: (0, 0)),
                  pl.BlockSpec((k, tn), lambda i, j: (0, j))],
        out_specs=pl.BlockSpec((tm, tn), lambda i, j: (i, j)),
        out_shape=jax.ShapeDtypeStruct((m, n), F32),
        scratch_shapes=[pltpu.VMEM((tm, k), BF16)],
        compiler_params=_cparams(("arbitrary", "arbitrary"), 48 * 1024 * 1024),
    )(x, g.reshape(1, k), w)


def _mix_kernel(rq_ref, rk_ref, rv_ref, rg_ref, gu_ref, gv_ref, cos_ref, sin_ref, dmat_ref, cdec_ref, kdec_ref,
                gc_ref, retg_ref, gmg_ref, wtril_ref, bsm_ref, a_ref, c_ref, sfin_ref, gmv_ref, s_ref):
    c = pl.program_id(0)
    last = pl.num_programs(0) - 1

    @pl.when(c == 0)
    def _():
        s_ref[...] = jnp.zeros_like(s_ref)

    cosf = cos_ref[...]
    sinf = sin_ref[...]
    for h in range(RET_HEADS):
        q = rq_ref[:, h * RET_DK:(h + 1) * RET_DK]
        k = rk_ref[:, h * RET_DK:(h + 1) * RET_DK]
        qr = q * cosf + pltpu.roll(q, RET_DK // 2, axis=1) * sinf
        kr = (k * cosf + pltpu.roll(k, RET_DK // 2, axis=1) * sinf) * RET_DK ** -0.5
        vb = rv_ref[:, h * RET_DV:(h + 1) * RET_DV].astype(BF16)
        qb = qr.astype(BF16)
        state = s_ref[h]
        a = lax.dot_general(qb, kr.astype(BF16), NT_DIMS, preferred_element_type=F32) * dmat_ref[h]
        inner = jnp.dot(a.astype(BF16), vb, preferred_element_type=F32)
        cross = jnp.dot(qb, state.astype(BF16), preferred_element_type=F32) * cdec_ref[h]
        o = inner + cross
        kd_t = jnp.transpose(kr * kdec_ref[h]).astype(BF16)
        s_ref[h] = gc_ref[h] * state + jnp.dot(kd_t, vb, preferred_element_type=F32)
        ms = jnp.mean(o * o, axis=-1, keepdims=True)
        y = o * lax.rsqrt(ms + EPS) * retg_ref[:, h * RET_DV:(h + 1) * RET_DV]
        g = rg_ref[:, h * RET_DV:(h + 1) * RET_DV]
        a_ref[:, h * RET_DV:(h + 1) * RET_DV] = (y * (g * jax.nn.sigmoid(g))).astype(a_ref.dtype)

    u = jax.nn.gelu(gu_ref[...])
    v = jax.nn.gelu(gv_ref[...])
    ms = jnp.mean(v * v, axis=-1, keepdims=True)
    vn = v * lax.rsqrt(ms + EPS) * gmg_ref[...]
    vb = vn.astype(BF16)
    for g in range(GM_GROUPS):
        sl = slice(g * GM_GW, (g + 1) * GM_GW)
        mixed = jnp.dot(wtril_ref[g], vb[:, sl], preferred_element_type=F32) + bsm_ref[g]
        c_ref[:, sl] = (u[:, sl] * mixed).astype(c_ref.dtype)

    @pl.when(c == last)
    def _():
        sfin_ref[...] = s_ref[...]
        gmv_ref[...] = vn


def prompt_mixers(p_main, consts, retg_row, gmg_row, wtril, bsm):
    seq = p_main.shape[0]
    nc = seq // CHUNK
    cosf, sinf, dmat, cdec, kdec, gc = consts
    w512 = lambda blk: pl.BlockSpec((CHUNK, 512), lambda c: (c, blk))
    w1024 = lambda blk: pl.BlockSpec((CHUNK, 1024), lambda c: (c, blk))
    full = lambda shape: pl.BlockSpec(shape, lambda c: (0,) * len(shape))
    return pl.pallas_call(
        _mix_kernel,
        grid=(nc,),
        in_specs=[w512(0), w512(1), w1024(1), w1024(2), w512(_M_GU // 512), w512(_M_GV // 512),
                  pl.BlockSpec((CHUNK, RET_DK), lambda c: (c, 0)), pl.BlockSpec((CHUNK, RET_DK), lambda c: (c, 0)),
                  full(dmat.shape), full(cdec.shape), full(kdec.shape), full(gc.shape),
                  full(retg_row.shape), full(gmg_row.shape), full(wtril.shape), full(bsm.shape)],
        out_specs=[pl.BlockSpec((CHUNK, RET_V), lambda c: (c, 0)), pl.BlockSpec((CHUNK, GM_W), lambda c: (c, 0)),
                   full((RET_HEADS, RET_DK, RET_DV)), full((CHUNK, GM_W))],
        out_shape=[jax.ShapeDtypeStruct((seq, RET_V), BF16), jax.ShapeDtypeStruct((seq, GM_W), BF16),
                   jax.ShapeDtypeStruct((RET_HEADS, RET_DK, RET_DV), F32),
                   jax.ShapeDtypeStruct((CHUNK, GM_W), F32)],
        scratch_shapes=[pltpu.VMEM((RET_HEADS, RET_DK, RET_DV), F32)],
        compiler_params=_cparams(("arbitrary",)),
    )(p_main, p_main, p_main, p_main, p_main, p_main, cosf, sinf, dmat, cdec, kdec, gc, retg_row, gmg_row, wtril, bsm)


def _dsa_prompt_kernel(kix_ref, qih_ref, qil_ref, w_ref, k_ref, vt_ref, qt_ref, o_ref, keys_ref, wq_ref, acc_ref, *,
                       topk):
    i = pl.program_id(0)
    grp = DSA_GROUP
    per_grp = grp // CHUNK
    ng = i // per_grp + 1

    def grp_rows(g):
        return pl.ds(pl.multiple_of(g * grp, grp), grp)

    for h in range(IDX_HEADS):
        cs = slice(h * CHUNK, (h + 1) * CHUNK)
        for part, src in enumerate((qih_ref, qil_ref, qih_ref, qil_ref)):
            wq_ref[part * IDX_DIM:(part + 1) * IDX_DIM, cs] = src[h]
    w_row = jnp.concatenate([w_ref[h:h + 1, :] for h in range(IDX_HEADS)], axis=1)

    def group_keys(g):
        parts = []
        for sub in range(grp // DSA_SCORE_ROWS):
            r0 = pl.multiple_of(g * grp + sub * DSA_SCORE_ROWS, DSA_SCORE_ROWS)
            kt = kix_ref[pl.ds(r0, DSA_SCORE_ROWS), :]
            acc = jnp.zeros((DSA_SCORE_ROWS, CHUNK), F32)
            for hp in range(IDX_HEADS // 2):
                cs = slice(hp * 2 * CHUNK, (hp + 1) * 2 * CHUNK)
                r = jnp.dot(kt, wq_ref[:, cs], preferred_element_type=F32)
                r = jnp.maximum(r, 0.0) * w_row[:, cs]
                acc = acc + r[:, :CHUNK] + r[:, CHUNK:]
            parts.append(_sortable_key(acc))
        return jnp.concatenate(parts, axis=0)

    def full_group(g, carry):
        keys_ref[grp_rows(g), :] = group_keys(g)
        return carry

    lax.fori_loop(0, ng - 1, full_group, 0)
    key_pos = (ng - 1) * grp + lax.broadcasted_iota(I32, (grp, CHUNK), 0)
    q_pos = i * CHUNK + lax.broadcasted_iota(I32, (grp, CHUNK), 1)
    keys_ref[grp_rows(ng - 1), :] = jnp.where(key_pos <= q_pos, group_keys(ng - 1), INT_MIN)

    def count_ge(mid):
        def body(g, cnt):
            m = (keys_ref[grp_rows(g), :] >= mid).astype(I32)
            for t in range(per_grp):
                cnt = cnt + m[t * CHUNK:(t + 1) * CHUNK]
            return cnt

        cnt = lax.fori_loop(0, ng, body, jnp.zeros((CHUNK, CHUNK), I32))
        return jnp.sum(cnt, axis=0, keepdims=True)

    if grp % topk == 0:
        def class_max(g, cm):
            kg = keys_ref[grp_rows(g), :]
            for t in range(grp // topk):
                cm = jnp.maximum(cm, kg[t * topk:(t + 1) * topk])
            return cm

        cm = lax.fori_loop(0, ng, class_max, jnp.full((topk, CHUNK), INT_MIN, I32))
        lo0 = jnp.min(cm, axis=0, keepdims=True)
        hi0 = jnp.minimum(jnp.max(cm, axis=0, keepdims=True), INT_MAX - 1) + 1
    else:
        lo0 = jnp.full((1, CHUNK), INT_MIN, I32)
        hi0 = jnp.full((1, CHUNK), INT_MAX, I32)
    t_k = _bisect_threshold(count_ge, lo0, hi0, topk)

    over = jnp.logical_and(count_ge(t_k) > topk, t_k > INT_MIN)

    @pl.when(jnp.max(over.astype(I32)) > 0)
    def _():
        need = topk - count_ge(t_k + 1)

        def pos_of(g):
            return g * grp + lax.broadcasted_iota(I32, (grp, CHUNK), 0)

        def count_tied_upto(j):
            def body(g, cnt):
                m = jnp.where(keys_ref[grp_rows(g), :] == t_k, jnp.where(pos_of(g) <= j, 1, 0), 0)
                for t in range(per_grp):
                    cnt = cnt + m[t * CHUNK:(t + 1) * CHUNK]
                return cnt

            cnt = lax.fori_loop(0, ng, body, jnp.zeros((CHUNK, CHUNK), I32))
            return jnp.sum(cnt, axis=0, keepdims=True)

        def step(_, lh):
            lo_j, hi_j = lh
            mid = (lo_j + hi_j) >> 1
            ok = count_tied_upto(mid) >= need
            return jnp.where(ok, lo_j, mid), jnp.where(ok, mid, hi_j)

        n_keys = keys_ref.shape[0]
        _, j_keep = lax.fori_loop(0, n_keys.bit_length(), step,
                                  (jnp.full((1, CHUNK), -1, I32), jnp.full((1, CHUNK), n_keys - 1, I32)))

        def drop_rest(g, carry):
            k = keys_ref[grp_rows(g), :]
            dropped = jnp.where(pos_of(g) > j_keep, jnp.where(over, INT_MIN, k), k)
            keys_ref[grp_rows(g), :] = jnp.where(k == t_k, dropped, k)
            return carry

        lax.fori_loop(0, ng, drop_rest, 0)

    thr = jnp.maximum(t_k, INT_MIN + 1)

    acc_ref[...] = jnp.zeros_like(acc_ref)

    def attend(g, carry):
        sel = keys_ref[grp_rows(g), :] >= thr
        out = []
        for h in range(ATT_HEADS):
            hs = slice(h * ATT_HD, (h + 1) * ATT_HD)
            m, l = carry[h]
            s = jnp.dot(k_ref[grp_rows(g), hs], qt_ref[hs, :], preferred_element_type=F32)
            s = jnp.where(sel, s, -jnp.inf)
            m_new = jnp.maximum(m, jnp.max(s, axis=0, keepdims=True))
            m_safe = jnp.where(m_new == -jnp.inf, 0.0, m_new)
            p = jnp.exp(s - m_safe)
            alpha = jnp.exp(m - m_safe)
            l = alpha * l + jnp.sum(p, axis=0, keepdims=True)
            acc_ref[h] = acc_ref[h] * alpha + jnp.dot(vt_ref[g, hs, :], p.astype(BF16), preferred_element_type=F32)
            out.append((m_new, l))
        return tuple(out)

    init = tuple((jnp.full((1, CHUNK), -jnp.inf, F32), jnp.zeros((1, CHUNK), F32)) for _ in range(ATT_HEADS))
    stats = lax.fori_loop(0, ng, attend, init)
    for h in range(ATT_HEADS):
        o_ref[h * ATT_HD:(h + 1) * ATT_HD, :] = acc_ref[h] / stats[h][1]


def dsa_prompt(aq, ak, av, iq, iw, ik):
    seq = aq.shape[0]
    assert seq % DSA_GROUP == 0
    nb = seq // CHUNK
    topk = min(TOPK_MAX, seq // 4)
    q_hi = iq.astype(BF16)
    q_lo = (iq - q_hi.astype(F32)).astype(BF16)
    qit_hi = q_hi.T.reshape(IDX_HEADS, IDX_DIM, seq)
    qit_lo = q_lo.T.reshape(IDX_HEADS, IDX_DIM, seq)
    k_hi = ik.astype(BF16)
    k_lo = (ik - k_hi.astype(F32)).astype(BF16)
    kix = jnp.concatenate([k_hi, k_hi, k_lo, k_lo], axis=-1)
    w_t = (iw * (IDX_DIM ** -0.5 * IDX_HEADS ** -0.5)).T
    qt = (aq * ATT_HD ** -0.5).T.astype(BF16)
    kb = ak.astype(BF16)
    vt = av.reshape(seq // DSA_GROUP, DSA_GROUP, ATT_W).transpose(0, 2, 1).astype(BF16)
    resident = lambda shape: pl.BlockSpec(shape, lambda i: (0,) * len(shape))
    o_t = pl.pallas_call(
        functools.partial(_dsa_prompt_kernel, topk=topk),
        grid=(nb,),
        in_specs=[resident(kix.shape),
                  pl.BlockSpec((IDX_HEADS, IDX_DIM, CHUNK), lambda i: (0, 0, i)),
                  pl.BlockSpec((IDX_HEADS, IDX_DIM, CHUNK), lambda i: (0, 0, i)),
                  pl.BlockSpec((IDX_HEADS, CHUNK), lambda i: (0, i)),
                  resident(kb.shape), resident(vt.shape),
                  pl.BlockSpec((ATT_W, CHUNK), lambda i: (0, i))],
        out_specs=pl.BlockSpec((ATT_W, CHUNK), lambda i: (0, i)),
        out_shape=jax.ShapeDtypeStruct((ATT_W, seq), F32),
        scratch_shapes=[pltpu.VMEM((seq, CHUNK), I32), pltpu.VMEM((4 * IDX_DIM, IDX_HEADS * CHUNK), BF16),
                        pltpu.VMEM((ATT_HEADS, ATT_HD, CHUNK), F32)],
        compiler_params=_cparams(("arbitrary",), 56 * 1024 * 1024),
    )(kix, qit_hi, qit_lo, w_t, kb, vt, qt)
    return o_t.T


def _pmem_kernel(x_ref, g_ref, wq_ref, mk_ref, mv_ref, wo_ref, o_ref):
    q = jnp.dot(_rmsnorm_bf16(x_ref[...], g_ref[...]), wq_ref[...], preferred_element_type=F32)
    outs = []
    for h in range(MEM_HEADS):
        hs = slice(h * MEM_HD, (h + 1) * MEM_HD)
        s = lax.dot_general(q[:, hs].astype(BF16), mk_ref[:, hs], NT_DIMS, preferred_element_type=F32) * MEM_HD ** -0.5
        e = jnp.exp(s - jnp.max(s, axis=-1, keepdims=True))
        oh = jnp.dot(e.astype(BF16), mv_ref[:, hs], preferred_element_type=F32) / jnp.sum(e, axis=-1, keepdims=True)
        outs.append(oh.astype(BF16))
    o = jnp.concatenate(outs, axis=1)
    o_ref[...] = x_ref[...] + jnp.dot(o, wo_ref[...], preferred_element_type=F32)


def prompt_mem_attend(x, g, wq, mk, mv, wo, tm=512):
    m, d = x.shape
    n_mem = mk.shape[0]
    full = lambda shape: pl.BlockSpec(shape, lambda i: (0,) * len(shape))
    return pl.pallas_call(
        _pmem_kernel,
        grid=(m // tm,),
        in_specs=[pl.BlockSpec((tm, d), lambda i: (i, 0)), full((1, d)), full((d, MEM_W)), full((n_mem, MEM_W)),
                  full((n_mem, MEM_W)), full((MEM_W, d))],
        out_specs=pl.BlockSpec((tm, d), lambda i: (i, 0)),
        out_shape=jax.ShapeDtypeStruct((m, d), F32),
        compiler_params=_cparams(("parallel",), 48 * 1024 * 1024),
    )(x, g.reshape(1, d), wq, mk, mv, wo)


def _conv_seq(a, prev, cw, cb, row):
    p1 = prev[7:8, :]
    p2 = prev[6:7, :]
    r1 = jnp.where(row == 0, p1, pltpu.roll(a, 1, axis=0))
    r2 = jnp.where(row == 0, p2, jnp.where(row == 1, p1, pltpu.roll(a, 2, axis=0)))
    return cb + cw[0:1, :] * r2 + cw[1:2, :] * r1 + cw[2:3, :] * a


def _ffn_seq_kernel(x_ref, g_ref, wg_ref, wu_ref, cwg_ref, cwu_ref, cbg_ref, cbu_ref, wd_ref,
                    o_ref, lg_ref, lu_ref, carry_ref, h_ref):
    i = pl.program_id(0)
    j = pl.program_id(1)
    tm = x_ref.shape[0]
    row = lax.broadcasted_iota(I32, (tm, 1), 0)

    @pl.when(j == 0)
    def _():
        h_ref[...] = _rmsnorm_bf16(x_ref[...], g_ref[...])
        o_ref[...] = x_ref[...]

    @pl.when(i == 0)
    def _():
        carry_ref[j] = jnp.zeros(carry_ref.shape[1:], F32)

    hx = h_ref[...]
    f = None
    for s in range(FFN_TF // FFN_SUB):
        cs = slice(s * FFN_SUB, (s + 1) * FFN_SUB)
        ag = jnp.dot(hx, wg_ref[:, cs], preferred_element_type=F32)
        au = jnp.dot(hx, wu_ref[:, cs], preferred_element_type=F32)
        lg_ref[0, :, cs] = ag[tm - 8:tm, :]
        lu_ref[0, :, cs] = au[tm - 8:tm, :]
        cg = _conv_seq(ag, carry_ref[j, 0, :, cs], cwg_ref[:, cs], cbg_ref[:, cs], row)
        cu = _conv_seq(au, carry_ref[j, 1, :, cs], cwu_ref[:, cs], cbu_ref[:, cs], row)
        carry_ref[j, 0, :, cs] = ag[tm - 8:tm, :]
        carry_ref[j, 1, :, cs] = au[tm - 8:tm, :]
        act = ((cg * jax.nn.sigmoid(cg)) * cu).astype(BF16)
        fs = jnp.dot(act, wd_ref[cs, :], preferred_element_type=F32)
        f = fs if f is None else f + fs
    o_ref[...] += f


def ffn_seq(x, g, ffn_w, tm=512):
    wg, wu, cwg, cwu, cbg, cbu, wd = ffn_w
    m, d = x.shape
    tf = FFN_TF
    nf = D_FF_PAD // tf
    col = lambda r: pl.BlockSpec((r, tf), lambda i, j: (0, j))
    tail = pl.BlockSpec((1, 8, tf), lambda i, j: (i, 0, j))
    return pl.pallas_call(
        _ffn_seq_kernel,
        grid=(m // tm, nf),
        in_specs=[pl.BlockSpec((tm, d), lambda i, j: (i, 0)), pl.BlockSpec((1, d), lambda i, j: (0, 0)),
                  col(d), col(d), col(CONV_W), col(CONV_W), col(1), col(1),
                  pl.BlockSpec((tf, d), lambda i, j: (j, 0))],
        out_specs=[pl.BlockSpec((tm, d), lambda i, j: (i, 0)), tail, tail],
        out_shape=[jax.ShapeDtypeStruct((m, d), F32), jax.ShapeDtypeStruct((m // tm, 8, D_FF_PAD), F32),
                   jax.ShapeDtypeStruct((m // tm, 8, D_FF_PAD), F32)],
        scratch_shapes=[pltpu.VMEM((nf, 2, 8, tf), F32), pltpu.VMEM((tm, d), BF16)],
        compiler_params=_cparams(("arbitrary", "arbitrary"), 48 * 1024 * 1024),
    )(x, g.reshape(1, d), wg, wu, cwg, cwu, cbg, cbu, wd)


def _ffn_step_kernel(x_ref, g_ref, wg_ref, wu_ref, cwg_ref, cwu_ref, cbg_ref, cbu_ref, wd_ref,
                     pg0_ref, pg1_ref, pu0_ref, pu1_ref, o_ref, ag_ref, au_ref, h_ref):
    j = pl.program_id(0)

    @pl.when(j == 0)
    def _():
        h_ref[...] = _rmsnorm_bf16(x_ref[...], g_ref[...])
        o_ref[...] = x_ref[...]

    hx = h_ref[...]
    ag = jnp.dot(hx, wg_ref[...], preferred_element_type=F32)
    au = jnp.dot(hx, wu_ref[...], preferred_element_type=F32)
    ag_ref[...] = ag
    au_ref[...] = au
    cg = cbg_ref[...] + cwg_ref[0:1, :] * pg0_ref[...] + cwg_ref[1:2, :] * pg1_ref[...] + cwg_ref[2:3, :] * ag
    cu = cbu_ref[...] + cwu_ref[0:1, :] * pu0_ref[...] + cwu_ref[1:2, :] * pu1_ref[...] + cwu_ref[2:3, :] * au
    act = ((cg * jax.nn.sigmoid(cg)) * cu).astype(BF16)
    o_ref[...] += jnp.dot(act, wd_ref[...], preferred_element_type=F32)


def ffn_step(x, g, ffn_w, prev):
    wg, wu, cwg, cwu, cbg, cbu, wd = ffn_w
    m, d = x.shape
    tf = FFN_TF
    nf = D_FF_PAD // tf
    col = lambda r: pl.BlockSpec((r, tf), lambda j: (0, j))
    row = pl.BlockSpec((m, d), lambda j: (0, 0))
    return pl.pallas_call(
        _ffn_step_kernel,
        grid=(nf,),
        in_specs=[row, pl.BlockSpec((1, d), lambda j: (0, 0)), col(d), col(d), col(CONV_W), col(CONV_W), col(1), col(1),
                  pl.BlockSpec((tf, d), lambda j: (j, 0)), col(m), col(m), col(m), col(m)],
        out_specs=[row, col(m), col(m)],
        out_shape=[jax.ShapeDtypeStruct((m, d), F32), jax.ShapeDtypeStruct((m, D_FF_PAD), F32),
                   jax.ShapeDtypeStruct((m, D_FF_PAD), F32)],
        scratch_shapes=[pltpu.VMEM((m, d), BF16)],
        compiler_params=_cparams(("arbitrary",), 48 * 1024 * 1024),
    )(x, g.reshape(1, d), wg, wu, cwg, cwu, cbg, cbu, wd, *prev)


def _smix_kernel(qc_ref, kc_ref, v_ref, rg_ref, gu_ref, gv_ref, st_ref, cos_ref, sin_ref, g1_ref, retg_ref,
                 gmg_ref, wrow_ref, brow_ref, a_ref, c_ref, ns_ref, gmv_ref):
    cosc = cos_ref[...]
    sinc = sin_ref[...]
    half = RET_DK // 2

    def rot(x):
        x1, x2 = x[:half], x[half:]
        return jnp.concatenate([x1 * cosc - x2 * sinc, x1 * sinc + x2 * cosc], axis=0)

    for h in range(RET_HEADS):
        vs = slice(h * RET_DV, (h + 1) * RET_DV)
        qr = rot(qc_ref[0, h])
        kr = rot(kc_ref[0, h]) * RET_DK ** -0.5
        v = v_ref[0, :, vs]
        state = st_ref[0, h]
        g1 = g1_ref[h]
        inner = jnp.sum(qr * kr, axis=0, keepdims=True) * v
        cross = jnp.sum(qr * state, axis=0, keepdims=True) * g1
        o = inner + cross
        ns_ref[0, h] = g1 * state + kr * v
        ms = jnp.mean(o * o, axis=-1, keepdims=True)
        y = o * lax.rsqrt(ms + EPS) * retg_ref[:, vs]
        g = rg_ref[0, :, vs]
        a_ref[0, :, vs] = y * (g * jax.nn.sigmoid(g))

    u = jax.nn.gelu(gu_ref[0])
    v = jax.nn.gelu(gv_ref[0])
    ms = jnp.mean(v * v, axis=-1, keepdims=True)
    vn = v * lax.rsqrt(ms + EPS) * gmg_ref[...]
    c_ref[0] = u * (wrow_ref[...] * vn + brow_ref[...])
    gmv_ref[0] = vn


def sample_mixers(p_main, state, consts, retg_row, gmg_row, wrow, brow):
    nb = p_main.shape[0]
    cosc, sinc, g1 = consts
    qc = p_main[:, 0:RET_QK].reshape(nb, RET_HEADS, RET_DK, 1)
    kc = p_main[:, RET_QK:2 * RET_QK].reshape(nb, RET_HEADS, RET_DK, 1)
    p3 = p_main.reshape(nb, 1, p_main.shape[1])
    per_b = lambda width, blk: pl.BlockSpec((1, 1, width), lambda b: (b, 0, blk))
    col4 = pl.BlockSpec((1, RET_HEADS, RET_DK, 1), lambda b: (b, 0, 0, 0))
    st4 = pl.BlockSpec((1, RET_HEADS, RET_DK, RET_DV), lambda b: (b, 0, 0, 0))
    full = lambda shape: pl.BlockSpec(shape, lambda b: (0,) * len(shape))
    return pl.pallas_call(
        _smix_kernel,
        grid=(nb,),
        in_specs=[col4, col4, per_b(1024, 1), per_b(1024, 2), per_b(512, _M_GU // 512), per_b(512, _M_GV // 512), st4,
                  full(cosc.shape), full(sinc.shape), full(g1.shape), full(retg_row.shape), full(gmg_row.shape),
                  full(wrow.shape), full(brow.shape)],
        out_specs=[per_b(RET_V, 0), per_b(GM_W, 0), st4, per_b(GM_W, 0)],
        out_shape=[jax.ShapeDtypeStruct((nb, 1, RET_V), F32), jax.ShapeDtypeStruct((nb, 1, GM_W), F32),
                   jax.ShapeDtypeStruct(state.shape, F32), jax.ShapeDtypeStruct((nb, 1, GM_W), F32)],
        compiler_params=_cparams(("parallel",)),
    )(qc, kc, p3, p3, p3, p3, state, cosc, sinc, g1, retg_row, gmg_row, wrow, brow)


def _page_scores_kernel(pt_ref, qx_ref, w_ref, q8_ref, *refs, n_pg):
    ki_refs = refs[:n_pg]
    k_refs = refs[n_pg:2 * n_pg]
    isc_ref, asc_ref = refs[2 * n_pg:]
    qx = qx_ref[0]
    q_hi = qx[:IDX_HEADS]
    w = w_ref[0]
    q8 = q8_ref[0]
    for n in range(n_pg):
        kp = ki_refs[n][0]
        k_hi = kp.astype(BF16)
        k_lo = (kp - k_hi.astype(F32)).astype(BF16)
        s2 = lax.dot_general(qx, k_hi, NT_DIMS, preferred_element_type=F32)
        s = s2[:IDX_HEADS] + s2[IDX_HEADS:] + lax.dot_general(q_hi, k_lo, NT_DIMS, preferred_element_type=F32)
        isc_ref[0, n] = jnp.sum(jnp.maximum(s, 0.0) * w, axis=0, keepdims=True)
        asc_ref[0, n] = lax.dot_general(q8, k_refs[n][0].astype(BF16), NT_DIMS, preferred_element_type=F32)


def page_scores(page_table, qx, w_col, q8, pool_ki, pool_k, n_pg):
    nb, n_pages = page_table.shape
    groups = n_pages // n_pg
    pt = page_table.reshape(-1)

    def pool_spec(shape, n):
        return pl.BlockSpec((1,) + shape, lambda b, g, pt_ref: (pt_ref[b * n_pages + g * n_pg + n], 0, 0))

    per_b = lambda shape: pl.BlockSpec((1,) + shape, lambda b, g, pt_ref: (b,) + (0,) * len(shape))
    grid_spec = pltpu.PrefetchScalarGridSpec(
        num_scalar_prefetch=1,
        grid=(nb, groups),
        in_specs=[per_b((2 * IDX_HEADS, IDX_DIM)), per_b((IDX_HEADS, 1)), per_b((8, ATT_HD))]
        + [pool_spec((PAGE_SIZE, IDX_DIM), n) for n in range(n_pg)]
        + [pool_spec((PAGE_ROWS, ATT_HD), n) for n in range(n_pg)],
        out_specs=[pl.BlockSpec((1, n_pg, 1, PAGE_SIZE), lambda b, g, pt_ref: (b, g, 0, 0)),
                   pl.BlockSpec((1, n_pg, 8, PAGE_ROWS), lambda b, g, pt_ref: (b, g, 0, 0))],
    )
    return pl.pallas_call(
        functools.partial(_page_scores_kernel, n_pg=n_pg),
        grid_spec=grid_spec,
        out_shape=[jax.ShapeDtypeStruct((nb, n_pages, 1, PAGE_SIZE), F32),
                   jax.ShapeDtypeStruct((nb, n_pages, 8, PAGE_ROWS), F32)],
        compiler_params=_cparams(("arbitrary", "arbitrary"), 48 * 1024 * 1024),
    )(pt, qx, w_col, q8, *([pool_ki] * n_pg), *([pool_k] * n_pg))


def _select_softmax_kernel(isc_ref, isce_ref, asc_ref, p_ref, keys_ref, thr_ref, keep_ref, *, topk, n_valid):
    h = pl.program_id(0)
    nb = isc_ref.shape[0]

    @pl.when(h == 0)
    def _():
        colid = lax.broadcasted_iota(I32, isc_ref.shape, 1)
        keys_ref[...] = jnp.where(colid < n_valid, _sortable_key(isc_ref[...]), INT_MIN)

        def count_ge(mid):
            return jnp.sum((keys_ref[...] >= mid).astype(I32), axis=1, keepdims=True)

        t_k = _bisect_threshold(count_ge, jnp.full((nb, 1), INT_MIN, I32), jnp.full((nb, 1), INT_MAX, I32), topk)
        thr_ref[...] = jnp.broadcast_to(jnp.maximum(t_k, INT_MIN + 1), thr_ref.shape)
        keep_ref[...] = jnp.full(keep_ref.shape, INT_MAX, I32)

        over = jnp.logical_and(count_ge(t_k) > topk, t_k > INT_MIN)

        @pl.when(jnp.max(over.astype(I32)) > 0)
        def _():
            need = topk - count_ge(t_k + 1)

            def count_tied_upto(j):
                tied = jnp.where(keys_ref[...] == t_k, jnp.where(colid <= j, 1, 0), 0)
                return jnp.sum(tied, axis=1, keepdims=True)

            def step(_, lh):
                lo_j, hi_j = lh
                mid = (lo_j + hi_j) >> 1
                ok = count_tied_upto(mid) >= need
                return jnp.where(ok, lo_j, mid), jnp.where(ok, mid, hi_j)

            n_keys = isc_ref.shape[1]
            _, j_keep = lax.fori_loop(0, n_keys.bit_length(), step,
                                      (jnp.full((nb, 1), -1, I32), jnp.full((nb, 1), n_keys - 1, I32)))
            keep_ref[...] = jnp.broadcast_to(jnp.where(over, j_keep, INT_MAX), keep_ref.shape)

    ce = lax.broadcasted_iota(I32, isce_ref.shape, 1)
    ke = _sortable_key(isce_ref[...])
    thr = thr_ref[:, 0:1]
    key_pos = ce >> (ATT_HEADS.bit_length() - 1)
    tied_ok = jnp.where(key_pos <= keep_ref[:, 0:1], asc_ref[0], -jnp.inf)
    s = jnp.where(ke > thr, asc_ref[0], jnp.where(ke == thr, tied_ok, -jnp.inf))
    s = jnp.where(ce < ATT_HEADS * n_valid, s, -jnp.inf)
    s = jnp.where((ce & (ATT_HEADS - 1)) == h, s, -jnp.inf)
    e = jnp.exp(s - jnp.max(s, axis=1, keepdims=True))
    p_ref[0] = e / jnp.sum(e, axis=1, keepdims=True)


def select_softmax(isc, isc_e, asc, topk, n_valid):
    nb = isc.shape[0]
    head_blk = pl.BlockSpec((1,) + asc.shape[1:], lambda h: (h, 0, 0))
    whole = lambda a: pl.BlockSpec(a.shape, lambda h: (0, 0))
    return pl.pallas_call(
        functools.partial(_select_softmax_kernel, topk=topk, n_valid=n_valid),
        grid=(ATT_HEADS,),
        in_specs=[whole(isc), whole(isc_e), head_blk],
        out_specs=head_blk,
        out_shape=jax.ShapeDtypeStruct(asc.shape, F32),
        scratch_shapes=[pltpu.VMEM(isc.shape, I32), pltpu.VMEM((nb, LANE), I32), pltpu.VMEM((nb, LANE), I32)],
        compiler_params=_cparams(("arbitrary",), 48 * 1024 * 1024),
    )(isc, isc_e, asc)


def _page_values_kernel(pt_ref, p_ref, *refs, n_pg):
    v_refs = refs[:n_pg]
    o_ref = refs[n_pg]
    acc = jnp.zeros(o_ref.shape[1:], F32)
    for n in range(n_pg):
        acc = acc + jnp.dot(p_ref[0, n], v_refs[n][0].astype(BF16), preferred_element_type=F32)

    @pl.when(pl.program_id(1) == 0)
    def _():
        o_ref[0] = acc

    @pl.when(pl.program_id(1) > 0)
    def _():
        o_ref[0] += acc


def page_values(page_table, p_pages, pool_v, n_pg):
    nb, n_pages = page_table.shape
    groups = n_pages // n_pg
    pt = page_table.reshape(-1)

    def pool_spec(n):
        return pl.BlockSpec((1, PAGE_ROWS, ATT_HD), lambda b, g, pt_ref: (pt_ref[b * n_pages + g * n_pg + n], 0, 0))

    grid_spec = pltpu.PrefetchScalarGridSpec(
        num_scalar_prefetch=1,
        grid=(nb, groups),
        in_specs=[pl.BlockSpec((1, n_pg, 8, PAGE_ROWS), lambda b, g, pt_ref: (b, g, 0, 0))]
        + [pool_spec(n) for n in range(n_pg)],
        out_specs=pl.BlockSpec((1, 8, ATT_HD), lambda b, g, pt_ref: (b, 0, 0)),
    )
    return pl.pallas_call(
        functools.partial(_page_values_kernel, n_pg=n_pg),
        grid_spec=grid_spec,
        out_shape=jax.ShapeDtypeStruct((nb, 8, ATT_HD), F32),
        compiler_params=_cparams(("arbitrary", "arbitrary"), 48 * 1024 * 1024),
    )(pt, p_pages, *([pool_v] * n_pg))


def _head_rows(q, scale):
    nb = q.shape[0]
    q3 = (q * scale).reshape(nb, ATT_HEADS, ATT_HD)
    return jnp.pad(q3, ((0, 0), (0, 8 - ATT_HEADS), (0, 0))).astype(BF16)


def dsa_sample(aq, ak, av, iq, iw, ik, pool_k, pool_v, pool_ki, page_table, n_pg=32):
    nb, n_pages = page_table.shape
    past = n_pages * PAGE_SIZE
    n_keys = past + PAGE_SIZE
    topk = min(TOPK_MAX, (past + 1) // 4)
    iq3 = iq.reshape(nb, IDX_HEADS, IDX_DIM)
    q_hi = iq3.astype(BF16)
    q_lo = (iq3 - q_hi.astype(F32)).astype(BF16)
    qx = jnp.concatenate([q_hi, q_lo], axis=1)
    w_col = (iw * (IDX_DIM ** -0.5 * IDX_HEADS ** -0.5)).reshape(nb, IDX_HEADS, 1)
    q8 = _head_rows(aq, ATT_HD ** -0.5)
    own = jnp.arange(nb, dtype=I32).reshape(nb, 1)
    pad_rows = lambda a, rows: jnp.pad(a, ((0, 0), (0, rows - a.shape[1]), (0, 0)))
    own_ki = pad_rows(ik[:, None, :], PAGE_SIZE)
    own_k = pad_rows(ak.reshape(nb, ATT_HEADS, ATT_HD), PAGE_ROWS)
    own_v = pad_rows(av.reshape(nb, ATT_HEADS, ATT_HD), PAGE_ROWS)
    isc_p, asc_p = page_scores(page_table, qx, w_col, q8, pool_ki, pool_k, n_pg)
    isc_n, asc_n = page_scores(own, qx, w_col, q8, own_ki, own_k, 1)
    isc = jnp.concatenate([isc_p, isc_n], axis=1).reshape(nb, n_keys)
    asc = jnp.concatenate([asc_p, asc_n], axis=1)[:, :, :ATT_HEADS]
    asc = asc.transpose(2, 0, 1, 3).reshape(ATT_HEADS, nb, n_keys * ATT_HEADS)
    p = select_softmax(isc, jnp.repeat(isc, ATT_HEADS, axis=1), asc, topk, past + 1)
    p = p.reshape(ATT_HEADS, nb, n_pages + 1, PAGE_ROWS).transpose(1, 2, 0, 3)
    p = jnp.pad(p, ((0, 0), (0, 0), (0, 8 - ATT_HEADS), (0, 0))).astype(BF16)
    o_p = page_values(page_table, p[:, :n_pages], pool_v, n_pg)
    o_n = page_values(own, p[:, n_pages:], own_v, 1)
    return (o_p + o_n)[:, :ATT_HEADS].reshape(nb, ATT_W)


def _smem_kernel(q8_ref, mk_ref, mv_ref, o_ref):
    rows = mk_ref.shape[1]
    head_of = lambda axis: lax.broadcasted_iota(I32, (8, rows), axis) & (MEM_HEADS - 1)
    own_head = head_of(1) == head_of(0)
    for b in range(q8_ref.shape[0]):
        s = lax.dot_general(q8_ref[b], mk_ref[b].astype(BF16), NT_DIMS, preferred_element_type=F32)
        s = jnp.where(own_head, s, -jnp.inf)
        e = jnp.exp(s - jnp.max(s, axis=-1, keepdims=True))
        p = e / jnp.sum(e, axis=-1, keepdims=True)
        o_ref[b] = jnp.dot(p.astype(BF16), mv_ref[b].astype(BF16), preferred_element_type=F32)


def sample_mem_core(q, mem_k, mem_v, layer, bb=8):
    nb = q.shape[0]
    rows = mem_k.shape[1]
    q8 = _head_rows(q, MEM_HD ** -0.5)
    first = layer * (nb // bb)
    o = pl.pallas_call(
        _smem_kernel,
        grid=(nb // bb,),
        in_specs=[pl.BlockSpec((bb, 8, MEM_HD), lambda i: (i, 0, 0)),
                  pl.BlockSpec((bb, rows, MEM_HD), lambda i: (first + i, 0, 0)),
                  pl.BlockSpec((bb, rows, MEM_HD), lambda i: (first + i, 0, 0))],
        out_specs=pl.BlockSpec((bb, 8, MEM_HD), lambda i: (i, 0, 0)),
        out_shape=jax.ShapeDtypeStruct((nb, 8, MEM_HD), F32),
        compiler_params=_cparams(("parallel",), 48 * 1024 * 1024),
    )(q8, mem_k, mem_v)
    return o[:, :MEM_HEADS].reshape(nb, MEM_W)


def _prep_layer(w_in, w_out, w_mq, w_mk, w_mv, w_mo, w_up, conv_w, conv_b, w_down, gm_ws, gm_bs):
    w_proj = jnp.concatenate([w_in[:, :_O_IQ_END], w_in[:, _O_GU:], w_in[:, _O_IK:_O_GU], w_in[:, _O_IW:_O_IK],
                              jnp.zeros((D_MODEL, N_PROJ - N_MAIN - IDX_DIM - IDX_HEADS), w_in.dtype)],
                             axis=1).astype(BF16)
    padc = lambda a: jnp.pad(a, ((0, 0), (0, D_FF_PAD - D_FF)))
    ffn_w = (padc(w_up[:, :D_FF]).astype(BF16), padc(w_up[:, D_FF:]).astype(BF16),
             padc(conv_w[:, :D_FF]), padc(conv_w[:, D_FF:]),
             padc(conv_b[None, :D_FF]), padc(conv_b[None, D_FF:]),
             jnp.pad(w_down, ((0, D_FF_PAD - D_FF), (0, 0))).astype(BF16))
    causal = jnp.tril(jnp.ones((CHUNK, CHUNK), dtype=bool))
    wtril = jnp.where(causal[None], gm_ws, 0.0).astype(BF16)
    bsm = jnp.broadcast_to(gm_bs[:, :, None], (GM_GROUPS, CHUNK, GM_GW))
    wrow = jnp.repeat(gm_ws[:, 0, 0], GM_GW).reshape(1, GM_W)
    brow = jnp.repeat(gm_bs[:, 0], GM_GW).reshape(1, GM_W)
    return dict(w_proj=w_proj, w_out=w_out.astype(BF16), w_mq=w_mq.astype(BF16),
                w_mkv=jnp.concatenate([w_mk, w_mv], axis=1).astype(BF16), w_mo=w_mo.astype(BF16), ffn_w=ffn_w,
                wtril=wtril, bsm=bsm, wrow=wrow, brow=brow)


def _retention_consts(seq, past):
    log_g = jnp.log(1.0 - 2.0 ** (-5.0 - jnp.arange(RET_HEADS, dtype=F32)))
    half = RET_DK // 2
    inv = ROPE_BASE ** (-jnp.arange(half, dtype=F32) / half)
    ang = jnp.arange(seq).astype(F32)[:, None] * inv[None, :]
    cos, sin = jnp.cos(ang), jnp.sin(ang)
    cosf = jnp.concatenate([cos, cos], axis=1)
    sinf = jnp.concatenate([-sin, sin], axis=1)
    n = jnp.arange(CHUNK, dtype=F32)
    diff = n[:, None] - n[None, :]
    causal = diff >= 0
    dmat = jnp.where(causal[None], jnp.exp(log_g[:, None, None] * jnp.where(causal, diff, 0.0)[None]), 0.0)
    cdec = jnp.broadcast_to(jnp.exp(log_g[:, None] * (n[None, :] + 1.0))[:, :, None], (RET_HEADS, CHUNK, RET_DV))
    kdec = jnp.broadcast_to(jnp.exp(log_g[:, None] * (CHUNK - 1.0 - n[None, :]))[:, :, None], (RET_HEADS, CHUNK, RET_DK))
    gc = jnp.broadcast_to(jnp.exp(log_g * CHUNK)[:, None, None], (RET_HEADS, RET_DK, RET_DV))
    ang_s = jnp.full((1,), past, dtype=F32)[:, None] * inv[None, :]
    cosc = jnp.cos(ang_s).reshape(half, 1)
    sinc = jnp.sin(ang_s).reshape(half, 1)
    g1 = jnp.broadcast_to(jnp.exp(log_g * 1.0)[:, None, None], (RET_HEADS, 1, RET_DV))
    return (cosf, sinf, dmat, cdec, kdec, gc), (cosc, sinc, g1)


def _pad_ff(a):
    return jnp.pad(a, ((0, 0), (0, D_FF_PAD - D_FF)))


def kernel(x_prompt, x_sample, mem_prompt, cache_k, cache_v, cache_kidx, page_table, cache_mem_k, cache_mem_v, state_ret, state_conv, norm_mix_g, w_in, ret_norm_g, gm_norm_g, gm_ws, gm_bs, w_out, norm_mem_g, mem_in_g, w_mq, w_mk, w_mv, w_mo, norm_ffn_g, w_up, conv_w, conv_b, w_down, final_norm_g):
    depth = w_in.shape[0]
    seq = x_prompt.shape[1]
    nbd = x_sample.shape[0]
    n_mem = mem_prompt.shape[1]
    past = page_table.shape[1] * PAGE_SIZE
    p_consts, s_consts = _retention_consts(seq, past)
    n_pool = cache_k.shape[1]
    pool_k = cache_k.reshape(depth * n_pool, PAGE_ROWS, ATT_HD)
    pool_v = cache_v.reshape(depth * n_pool, PAGE_ROWS, ATT_HD)
    pool_ki = cache_kidx.reshape(depth * n_pool, PAGE_SIZE, IDX_DIM)
    mem_k_all = cache_mem_k.reshape(depth * nbd, n_mem * MEM_HEADS, MEM_HD)
    mem_v_all = cache_mem_v.reshape(depth * nbd, n_mem * MEM_HEADS, MEM_HD)

    xp = x_prompt.reshape(seq, D_MODEL)
    xs = x_sample.reshape(nbd, D_MODEL)
    mem = mem_prompt.reshape(n_mem, D_MODEL)
    outs = {k: [] for k in ("kp", "vp", "kip", "ks", "vs", "kis", "mkp", "mvp", "rsp", "rss", "csp", "css", "gvp", "gvs")}

    for l in range(depth):
        w = _prep_layer(w_in[l], w_out[l], w_mq[l], w_mk[l], w_mv[l], w_mo[l], w_up[l], conv_w[l], conv_b[l],
                        w_down[l], gm_ws[l], gm_bs[l])
        retg_row = ret_norm_g[l].reshape(1, RET_V)
        gmg_row = gm_norm_g[l].reshape(1, GM_W)

        pm = norm_matmul(xp, norm_mix_g[l], w["w_proj"])
        ak, av = pm[:, _M_AK:_M_AV], pm[:, _M_AV:_M_IQ]
        ik = pm[:, _M_IK:_M_IW]
        a_out, c_out, r_state, gm_v = prompt_mixers(pm, p_consts, retg_row, gmg_row, w["wtril"], w["bsm"])
        b_out = dsa_prompt(pm[:, _M_AQ:_M_AK], ak, av, pm[:, _M_IQ:_M_GU], pm[:, _M_IW:_M_IW + IDX_HEADS], ik)
        xp = out_projection(a_out, b_out, c_out, w["w_out"], xp)
        mkv = norm_matmul(mem, mem_in_g[l], w["w_mkv"])
        mk, mv = mkv[:, :MEM_W], mkv[:, MEM_W:]
        xp = prompt_mem_attend(xp, norm_mem_g[l], w["w_mq"], mk.astype(BF16), mv.astype(BF16), w["w_mo"])
        xp, tail_g, tail_u = ffn_seq(xp, norm_ffn_g[l], w["ffn_w"])
        last_g, last_u = tail_g[-1], tail_u[-1]
        outs["kp"].append(ak.reshape(1, seq, ATT_HEADS, ATT_HD))
        outs["vp"].append(av.reshape(1, seq, ATT_HEADS, ATT_HD))
        outs["kip"].append(ik.reshape(1, seq, IDX_DIM))
        outs["mkp"].append(mk.reshape(1, n_mem, MEM_HEADS, MEM_HD))
        outs["mvp"].append(mv.reshape(1, n_mem, MEM_HEADS, MEM_HD))
        outs["rsp"].append(r_state.reshape(1, RET_HEADS, RET_DK, RET_DV))
        outs["csp"].append(jnp.concatenate([last_g[8 - (CONV_W - 1):, :D_FF], last_u[8 - (CONV_W - 1):, :D_FF]], axis=1)[None])
        outs["gvp"].append(gm_v[None])

        pm = norm_matmul(xs, norm_mix_g[l], w["w_proj"])
        ak, av = pm[:, _M_AK:_M_AV], pm[:, _M_AV:_M_IQ]
        ik = pm[:, _M_IK:_M_IW]
        a_out, c_out, r_state, gm_v = sample_mixers(pm, state_ret[l], s_consts, retg_row, gmg_row, w["wrow"], w["brow"])
        b_out = dsa_sample(pm[:, _M_AQ:_M_AK], ak, av, pm[:, _M_IQ:_M_GU], pm[:, _M_IW:_M_IW + IDX_HEADS], ik,
                           pool_k, pool_v, pool_ki, page_table + l * n_pool)
        xs = out_projection(a_out.reshape(nbd, RET_V), b_out, c_out.reshape(nbd, GM_W), w["w_out"], xs)
        q = norm_matmul(xs, norm_mem_g[l], w["w_mq"])
        o = sample_mem_core(q, mem_k_all, mem_v_all, l)
        xs = matmul(o.astype(BF16), w["w_mo"], res=xs)
        sc = state_conv[l]
        prev = (_pad_ff(sc[:, 0, :D_FF]), _pad_ff(sc[:, 1, :D_FF]), _pad_ff(sc[:, 0, D_FF:]), _pad_ff(sc[:, 1, D_FF:]))
        xs, a_g, a_u = ffn_step(xs, norm_ffn_g[l], w["ffn_w"], prev)
        a_new = jnp.concatenate([a_g[:, :D_FF], a_u[:, :D_FF]], axis=1)
        outs["ks"].append(ak.reshape(nbd, 1, ATT_HEADS, ATT_HD))
        outs["vs"].append(av.reshape(nbd, 1, ATT_HEADS, ATT_HD))
        outs["kis"].append(ik.reshape(nbd, 1, IDX_DIM))
        outs["rss"].append(r_state)
        outs["css"].append(jnp.stack([sc[:, 1, :], a_new], axis=1))
        outs["gvs"].append(gm_v)

    y_prompt = rmsnorm_rows(xp, final_norm_g, F32).reshape(1, seq, D_MODEL)
    y_sample = rmsnorm_rows(xs, final_norm_g, F32).reshape(nbd, 1, D_MODEL)
    st = lambda k: jnp.stack(outs[k])
    return (y_prompt, y_sample, st("kp"), st("vp"), st("kip"), st("ks"), st("vs"), st("kis"), st("mkp"), st("mvp"),
            st("rsp"), st("rss"), st("csp"), st("css"), st("gvp"), st("gvs"))
```

```python
import functools

import jax
import jax.numpy as jnp
from jax import lax
from jax.experimental import pallas as pl
from jax.experimental.pallas import tpu as pltpu

F32 = jnp.float32
BF16 = jnp.bfloat16
I32 = jnp.int32

D_MODEL = 2048
PAGE_SIZE = 128
RET_HEADS = 4
RET_DK = 128
RET_DV = 256
CHUNK = 128
ROPE_BASE = 10000.0
ATT_HEADS = 4
ATT_HD = 128
IDX_HEADS = 16
IDX_DIM = 64
TOPK_MAX = 256
GM_GROUPS = 4
GM_GW = 128
MEM_HEADS = 4
MEM_HD = 128
D_FF = 5504
CONV_W = 3
EPS = 1e-6

RET_QK = RET_HEADS * RET_DK
RET_V = RET_HEADS * RET_DV
ATT_W = ATT_HEADS * ATT_HD
IDX_Q = IDX_HEADS * IDX_DIM
GM_W = GM_GROUPS * GM_GW
MEM_W = MEM_HEADS * MEM_HD
PAGE_ROWS = PAGE_SIZE * ATT_HEADS

_O_IQ_END = 2 * RET_QK + 2 * RET_V + 3 * ATT_W + IDX_Q
_O_IW = _O_IQ_END
_O_IK = _O_IW + IDX_HEADS
_O_GU = _O_IK + IDX_DIM
_M_AQ = 2 * RET_QK + 2 * RET_V
_M_AK = _M_AQ + ATT_W
_M_AV = _M_AK + ATT_W
_M_IQ = _M_AV + ATT_W
_M_GU = _M_IQ + IDX_Q
_M_GV = _M_GU + GM_W
N_MAIN = _M_GV + GM_W
_M_IK = N_MAIN
_M_IW = _M_IK + IDX_DIM
N_PROJ = N_MAIN + 512
LANE = 128
D_FF_PAD = 5632
FFN_TF = 512
FFN_SUB = 512
DSA_GROUP = 512
DSA_SCORE_ROWS = 256
BISECT_STEPS_PER_CHECK = 4

INT_MIN = -2 ** 31
INT_MAX = 2 ** 31 - 1
V7X_VMEM_BYTES = 64 * 1024 * 1024
NT_DIMS = (((1,), (1,)), ((), ()))


def _cparams(semantics, vmem_bytes=None):
    return pltpu.CompilerParams(dimension_semantics=semantics, vmem_limit_bytes=vmem_bytes)


def _sortable_key(x):
    b = lax.bitcast_convert_type(x, I32)
    return jnp.where(x == 0.0, 0, b ^ ((b >> 31) & INT_MAX))


def _bisect_threshold(count_ge, lo, hi, topk):
    def cond(st):
        it, active, _, _ = st
        return jnp.logical_and(it < 32, active > 0)

    def body(st):
        it, _, lo, hi = st
        for _ in range(BISECT_STEPS_PER_CHECK):
            mid = (lo >> 1) + (hi >> 1) + (lo & hi & 1)
            cnt = count_ge(mid)
            ge = cnt >= topk
            lo = jnp.where(ge, mid, lo)
            hi = jnp.where(cnt == topk, mid + 1, jnp.where(ge, hi, mid))
        active = jnp.max((hi - 1 > lo).astype(I32))
        return it + BISECT_STEPS_PER_CHECK, active, lo, hi

    _, _, lo, _ = lax.while_loop(cond, body, (jnp.int32(0), jnp.int32(1), lo, hi))
    return lo


def _rmsnorm_bf16(x, g):
    ms = jnp.mean(x * x, axis=-1, keepdims=True)
    return (x * lax.rsqrt(ms + EPS) * g).astype(BF16)


def _norm_kernel(x_ref, g_ref, o_ref):
    x = x_ref[...]
    ms = jnp.mean(x * x, axis=-1, keepdims=True)
    o_ref[...] = (x * lax.rsqrt(ms + EPS) * g_ref[...]).astype(o_ref.dtype)


def rmsnorm_rows(x, g, out_dtype):
    m, d = x.shape
    tm = min(m, 256)
    return pl.pallas_call(
        _norm_kernel,
        grid=(m // tm,),
        in_specs=[pl.BlockSpec((tm, d), lambda i: (i, 0)), pl.BlockSpec((1, d), lambda i: (0, 0))],
        out_specs=pl.BlockSpec((tm, d), lambda i: (i, 0)),
        out_shape=jax.ShapeDtypeStruct((m, d), out_dtype),
        compiler_params=_cparams(("parallel",)),
    )(x, g.reshape(1, d))


def _mm_kernel(x_ref, w_ref, o_ref):
    o_ref[...] = jnp.dot(x_ref[...], w_ref[...], preferred_element_type=F32)


def _mm_res_kernel(x_ref, w_ref, r_ref, o_ref):
    o_ref[...] = r_ref[...] + jnp.dot(x_ref[...], w_ref[...], preferred_element_type=F32)


def matmul(x, w, res=None, tm=1024, tn=512):
    m, k = x.shape
    n = w.shape[1]
    tm = min(tm, m)
    tn = min(tn, n)
    assert m % tm == 0 and n % tn == 0
    in_specs = [pl.BlockSpec((tm, k), lambda i, j: (i, 0)), pl.BlockSpec((k, tn), lambda i, j: (0, j))]
    args = [x, w]
    body = _mm_kernel
    if res is not None:
        in_specs.append(pl.BlockSpec((tm, tn), lambda i, j: (i, j)))
        args.append(res)
        body = _mm_res_kernel
    return pl.pallas_call(
        body,
        grid=(m // tm, n // tn),
        in_specs=in_specs,
        out_specs=pl.BlockSpec((tm, tn), lambda i, j: (i, j)),
        out_shape=jax.ShapeDtypeStruct((m, n), F32),
        compiler_params=_cparams(("parallel", "parallel"), 48 * 1024 * 1024),
    )(*args)


def _out_proj_kernel(a_ref, b_ref, c_ref, wa_ref, wb_ref, wc_ref, r_ref, o_ref):
    acc = jnp.dot(a_ref[...].astype(BF16), wa_ref[...], preferred_element_type=F32)
    acc = acc + jnp.dot(b_ref[...].astype(BF16), wb_ref[...], preferred_element_type=F32)
    acc = acc + jnp.dot(c_ref[...].astype(BF16), wc_ref[...], preferred_element_type=F32)
    o_ref[...] = r_ref[...] + acc


def out_projection(a, b, c, w_out, res, tm=1024, tn=512):
    m = a.shape[0]
    n = w_out.shape[1]
    tm = min(tm, m)
    ka, kb, kc = a.shape[1], b.shape[1], c.shape[1]
    assert m % tm == 0 and n % tn == 0 and ka % kb == 0 and kb == kc
    rows = lambda k: pl.BlockSpec((tm, k), lambda i, j: (i, 0))
    return pl.pallas_call(
        _out_proj_kernel,
        grid=(m // tm, n // tn),
        in_specs=[rows(ka), rows(kb), rows(kc),
                  pl.BlockSpec((ka, tn), lambda i, j: (0, j)),
                  pl.BlockSpec((kb, tn), lambda i, j: (ka // kb, j)),
                  pl.BlockSpec((kc, tn), lambda i, j: (ka // kb + 1, j)),
                  pl.BlockSpec((tm, tn), lambda i, j: (i, j))],
        out_specs=pl.BlockSpec((tm, tn), lambda i, j: (i, j)),
        out_shape=jax.ShapeDtypeStruct((m, n), F32),
        compiler_params=_cparams(("parallel", "parallel"), 48 * 1024 * 1024),
    )(a, b, c, w_out, w_out, w_out, res)


def _norm_mm_kernel(x_ref, g_ref, w_ref, o_ref, h_ref):
    @pl.when(pl.program_id(1) == 0)
    def _():
        h_ref[...] = _rmsnorm_bf16(x_ref[...], g_ref[...])

    o_ref[...] = jnp.dot(h_ref[...], w_ref[...], preferred_element_type=F32)


def norm_matmul(x, g, w, tm=1024, tn=512):
    m, k = x.shape
    n = w.shape[1]
    tm = min(tm, m)
    tn = min(tn, n)
    assert m % tm == 0 and n % tn == 0
    return pl.pallas_call(
        _norm_mm_kernel,
        grid=(m // tm, n // tn),
        in_specs=[pl.BlockSpec((tm, k), lambda i, j: (i, 0)), pl.BlockSpec((1, k), lambda i, j: (0, 0)),
                  pl.BlockSpec((k, tn), lambda i, j: (0, j))],
        out_specs=pl.BlockSpec((tm, tn), lambda i, j: (i, j)),
        out_shape=jax.ShapeDtypeStruct((m, n), F32),
        scratch_shapes=[pltpu.VMEM((tm, k), BF16)],
        compiler_params=_cparams(("arbitrary", "arbitrary"), 48 * 1024 * 1024),
    )(x, g.reshape(1, k), w)


def _mix_kernel(rq_ref, rk_ref, rv_ref, rg_ref, gu_ref, gv_ref, cos_ref, sin_ref, dmat_ref, cdec_ref, kdec_ref,
                gc_ref, retg_ref, gmg_ref, wtril_ref, bsm_ref, a_ref, c_ref, sfin_ref, gmv_ref, s_ref):
    c = pl.program_id(0)
    last = pl.num_programs(0) - 1

    @pl.when(c == 0)
    def _():
        s_ref[...] = jnp.zeros_like(s_ref)

    cosf = cos_ref[...]
    sinf = sin_ref[...]
    for h in range(RET_HEADS):
        q = rq_ref[:, h * RET_DK:(h + 1) * RET_DK]
        k = rk_ref[:, h * RET_DK:(h + 1) * RET_DK]
        qr = q * cosf + pltpu.roll(q, RET_DK // 2, axis=1) * sinf
        kr = (k * cosf + pltpu.roll(k, RET_DK // 2, axis=1) * sinf) * RET_DK ** -0.5
        vb = rv_ref[:, h * RET_DV:(h + 1) * RET_DV].astype(BF16)
        qb = qr.astype(BF16)
        state = s_ref[h]
        a = lax.dot_general(qb, kr.astype(BF16), NT_DIMS, preferred_element_type=F32) * dmat_ref[h]
        inner = jnp.dot(a.astype(BF16), vb, preferred_element_type=F32)
        cross = jnp.dot(qb, state.astype(BF16), preferred_element_type=F32) * cdec_ref[h]
        o = inner + cross
        kd_t = jnp.transpose(kr * kdec_ref[h]).astype(BF16)
        s_ref[h] = gc_ref[h] * state + jnp.dot(kd_t, vb, preferred_element_type=F32)
        ms = jnp.mean(o * o, axis=-1, keepdims=True)
        y = o * lax.rsqrt(ms + EPS) * retg_ref[:, h * RET_DV:(h + 1) * RET_DV]
        g = rg_ref[:, h * RET_DV:(h + 1) * RET_DV]
        a_ref[:, h * RET_DV:(h + 1) * RET_DV] = (y * (g * jax.nn.sigmoid(g))).astype(a_ref.dtype)

    u = jax.nn.gelu(gu_ref[...])
    v = jax.nn.gelu(gv_ref[...])
    ms = jnp.mean(v * v, axis=-1, keepdims=True)
    vn = v * lax.rsqrt(ms + EPS) * gmg_ref[...]
    vb = vn.astype(BF16)
    for g in range(GM_GROUPS):
        sl = slice(g * GM_GW, (g + 1) * GM_GW)
        mixed = jnp.dot(wtril_ref[g], vb[:, sl], preferred_element_type=F32) + bsm_ref[g]
        c_ref[:, sl] = (u[:, sl] * mixed).astype(c_ref.dtype)

    @pl.when(c == last)
    def _():
        sfin_ref[...] = s_ref[...]
        gmv_ref[...] = vn


def prompt_mixers(p_main, consts, retg_row, gmg_row, wtril, bsm):
    seq = p_main.shape[0]
    nc = seq // CHUNK
    cosf, sinf, dmat, cdec, kdec, gc = consts
    w512 = lambda blk: pl.BlockSpec((CHUNK, 512), lambda c: (c, blk))
    w1024 = lambda blk: pl.BlockSpec((CHUNK, 1024), lambda c: (c, blk))
    full = lambda shape: pl.BlockSpec(shape, lambda c: (0,) * len(shape))
    return pl.pallas_call(
        _mix_kernel,
        grid=(nc,),
        in_specs=[w512(0), w512(1), w1024(1), w1024(2), w512(_M_GU // 512), w512(_M_GV // 512),
                  pl.BlockSpec((CHUNK, RET_DK), lambda c: (c, 0)), pl.BlockSpec((CHUNK, RET_DK), lambda c: (c, 0)),
                  full(dmat.shape), full(cdec.shape), full(kdec.shape), full(gc.shape),
                  full(retg_row.shape), full(gmg_row.shape), full(wtril.shape), full(bsm.shape)],
        out_specs=[pl.BlockSpec((CHUNK, RET_V), lambda c: (c, 0)), pl.BlockSpec((CHUNK, GM_W), lambda c: (c, 0)),
                   full((RET_HEADS, RET_DK, RET_DV)), full((CHUNK, GM_W))],
        out_shape=[jax.ShapeDtypeStruct((seq, RET_V), BF16), jax.ShapeDtypeStruct((seq, GM_W), BF16),
                   jax.ShapeDtypeStruct((RET_HEADS, RET_DK, RET_DV), F32),
                   jax.ShapeDtypeStruct((CHUNK, GM_W), F32)],
        scratch_shapes=[pltpu.VMEM((RET_HEADS, RET_DK, RET_DV), F32)],
        compiler_params=_cparams(("arbitrary",)),
    )(p_main, p_main, p_main, p_main, p_main, p_main, cosf, sinf, dmat, cdec, kdec, gc, retg_row, gmg_row, wtril, bsm)


def _dsa_prompt_kernel(kix_ref, qih_ref, qil_ref, w_ref, k_ref, vt_ref, qt_ref, o_ref, keys_ref, wq_ref, acc_ref, *,
                       topk):
    i = pl.program_id(0)
    grp = DSA_GROUP
    per_grp = grp // CHUNK
    ng = i // per_grp + 1

    def grp_rows(g):
        return pl.ds(pl.multiple_of(g * grp, grp), grp)

    for h in range(IDX_HEADS):
        cs = slice(h * CHUNK, (h + 1) * CHUNK)
        for part, src in enumerate((qih_ref, qil_ref, qih_ref, qil_ref)):
            wq_ref[part * IDX_DIM:(part + 1) * IDX_DIM, cs] = src[h]
    w_row = jnp.concatenate([w_ref[h:h + 1, :] for h in range(IDX_HEADS)], axis=1)

    def group_keys(g):
        parts = []
        for sub in range(grp // DSA_SCORE_ROWS):
            r0 = pl.multiple_of(g * grp + sub * DSA_SCORE_ROWS, DSA_SCORE_ROWS)
            kt = kix_ref[pl.ds(r0, DSA_SCORE_ROWS), :]
            acc = jnp.zeros((DSA_SCORE_ROWS, CHUNK), F32)
            for hp in range(IDX_HEADS // 2):
                cs = slice(hp * 2 * CHUNK, (hp + 1) * 2 * CHUNK)
                r = jnp.dot(kt, wq_ref[:, cs], preferred_element_type=F32)
                r = jnp.maximum(r, 0.0) * w_row[:, cs]
                acc = acc + r[:, :CHUNK] + r[:, CHUNK:]
            parts.append(_sortable_key(acc))
        return jnp.concatenate(parts, axis=0)

    def full_group(g, carry):
        keys_ref[grp_rows(g), :] = group_keys(g)
        return carry

    lax.fori_loop(0, ng - 1, full_group, 0)
    key_pos = (ng - 1) * grp + lax.broadcasted_iota(I32, (grp, CHUNK), 0)
    q_pos = i * CHUNK + lax.broadcasted_iota(I32, (grp, CHUNK), 1)
    keys_ref[grp_rows(ng - 1), :] = jnp.where(key_pos <= q_pos, group_keys(ng - 1), INT_MIN)

    def count_ge(mid):
        def body(g, cnt):
            m = (keys_ref[grp_rows(g), :] >= mid).astype(I32)
            for t in range(per_grp):
                cnt = cnt + m[t * CHUNK:(t + 1) * CHUNK]
            return cnt

        cnt = lax.fori_loop(0, ng, body, jnp.zeros((CHUNK, CHUNK), I32))
        return jnp.sum(cnt, axis=0, keepdims=True)

    if grp % topk == 0:
        def class_max(g, cm):
            kg = keys_ref[grp_rows(g), :]
            for t in range(grp // topk):
                cm = jnp.maximum(cm, kg[t * topk:(t + 1) * topk])
            return cm

        cm = lax.fori_loop(0, ng, class_max, jnp.full((topk, CHUNK), INT_MIN, I32))
        lo0 = jnp.min(cm, axis=0, keepdims=True)
        hi0 = jnp.minimum(jnp.max(cm, axis=0, keepdims=True), INT_MAX - 1) + 1
    else:
        lo0 = jnp.full((1, CHUNK), INT_MIN, I32)
        hi0 = jnp.full((1, CHUNK), INT_MAX, I32)
    t_k = _bisect_threshold(count_ge, lo0, hi0, topk)

    over = jnp.logical_and(count_ge(t_k) > topk, t_k > INT_MIN)

    @pl.when(jnp.max(over.astype(I32)) > 0)
    def _():
        need = topk - count_ge(t_k + 1)

        def pos_of(g):
            return g * grp + lax.broadcasted_iota(I32, (grp, CHUNK), 0)

        def count_tied_upto(j):
            def body(g, cnt):
                m = jnp.where(keys_ref[grp_rows(g), :] == t_k, jnp.where(pos_of(g) <= j, 1, 0), 0)
                for t in range(per_grp):
                    cnt = cnt + m[t * CHUNK:(t + 1) * CHUNK]
                return cnt

            cnt = lax.fori_loop(0, ng, body, jnp.zeros((CHUNK, CHUNK), I32))
            return jnp.sum(cnt, axis=0, keepdims=True)

        def step(_, lh):
            lo_j, hi_j = lh
            mid = (lo_j + hi_j) >> 1
            ok = count_tied_upto(mid) >= need
            return jnp.where(ok, lo_j, mid), jnp.where(ok, mid, hi_j)

        n_keys = keys_ref.shape[0]
        _, j_keep = lax.fori_loop(0, n_keys.bit_length(), step,
                                  (jnp.full((1, CHUNK), -1, I32), jnp.full((1, CHUNK), n_keys - 1, I32)))

        def drop_rest(g, carry):
            k = keys_ref[grp_rows(g), :]
            dropped = jnp.where(pos_of(g) > j_keep, jnp.where(over, INT_MIN, k), k)
            keys_ref[grp_rows(g), :] = jnp.where(k == t_k, dropped, k)
            return carry

        lax.fori_loop(0, ng, drop_rest, 0)

    thr = jnp.maximum(t_k, INT_MIN + 1)

    acc_ref[...] = jnp.zeros_like(acc_ref)

    def attend(g, carry):
        sel = keys_ref[grp_rows(g), :] >= thr
        out = []
        for h in range(ATT_HEADS):
            hs = slice(h * ATT_HD, (h + 1) * ATT_HD)
            m, l = carry[h]
            s = jnp.dot(k_ref[grp_rows(g), hs], qt_ref[hs, :], preferred_element_type=F32)
            s = jnp.where(sel, s, -jnp.inf)
            m_new = jnp.maximum(m, jnp.max(s, axis=0, keepdims=True))
            m_safe = jnp.where(m_new == -jnp.inf, 0.0, m_new)
            p = jnp.exp(s - m_safe)
            alpha = jnp.exp(m - m_safe)
            l = alpha * l + jnp.sum(p, axis=0, keepdims=True)
            acc_ref[h] = acc_ref[h] * alpha + jnp.dot(vt_ref[g, hs, :], p.astype(BF16), preferred_element_type=F32)
            out.append((m_new, l))
        return tuple(out)

    init = tuple((jnp.full((1, CHUNK), -jnp.inf, F32), jnp.zeros((1, CHUNK), F32)) for _ in range(ATT_HEADS))
    stats = lax.fori_loop(0, ng, attend, init)
    for h in range(ATT_HEADS):
        o_ref[h * ATT_HD:(h + 1) * ATT_HD, :] = acc_ref[h] / stats[h][1]


def dsa_prompt(aq, ak, av, iq, iw, ik):
    seq = aq.shape[0]
    assert seq % DSA_GROUP == 0
    nb = seq // CHUNK
    topk = min(TOPK_MAX, seq // 4)
    q_hi = iq.astype(BF16)
    q_lo = (iq - q_hi.astype(F32)).astype(BF16)
    qit_hi = q_hi.T.reshape(IDX_HEADS, IDX_DIM, seq)
    qit_lo = q_lo.T.reshape(IDX_HEADS, IDX_DIM, seq)
    k_hi = ik.astype(BF16)
    k_lo = (ik - k_hi.astype(F32)).astype(BF16)
    kix = jnp.concatenate([k_hi, k_hi, k_lo, k_lo], axis=-1)
    w_t = (iw * (IDX_DIM ** -0.5 * IDX_HEADS ** -0.5)).T
    qt = (aq * ATT_HD ** -0.5).T.astype(BF16)
    kb = ak.astype(BF16)
    vt = av.reshape(seq // DSA_GROUP, DSA_GROUP, ATT_W).transpose(0, 2, 1).astype(BF16)
    resident = lambda shape: pl.BlockSpec(shape, lambda i: (0,) * len(shape))
    o_t = pl.pallas_call(
        functools.partial(_dsa_prompt_kernel, topk=topk),
        grid=(nb,),
        in_specs=[resident(kix.shape),
                  pl.BlockSpec((IDX_HEADS, IDX_DIM, CHUNK), lambda i: (0, 0, i)),
                  pl.BlockSpec((IDX_HEADS, IDX_DIM, CHUNK), lambda i: (0, 0, i)),
                  pl.BlockSpec((IDX_HEADS, CHUNK), lambda i: (0, i)),
                  resident(kb.shape), resident(vt.shape),
                  pl.BlockSpec((ATT_W, CHUNK), lambda i: (0, i))],
        out_specs=pl.BlockSpec((ATT_W, CHUNK), lambda i: (0, i)),
        out_shape=jax.ShapeDtypeStruct((ATT_W, seq), F32),
        scratch_shapes=[pltpu.VMEM((seq, CHUNK), I32), pltpu.VMEM((4 * IDX_DIM, IDX_HEADS * CHUNK), BF16),
                        pltpu.VMEM((ATT_HEADS, ATT_HD, CHUNK), F32)],
        compiler_params=_cparams(("arbitrary",), 56 * 1024 * 1024),
    )(kix, qit_hi, qit_lo, w_t, kb, vt, qt)
    return o_t.T


def _pmem_kernel(x_ref, g_ref, wq_ref, mk_ref, mv_ref, wo_ref, o_ref):
    q = jnp.dot(_rmsnorm_bf16(x_ref[...], g_ref[...]), wq_ref[...], preferred_element_type=F32)
    outs = []
    for h in range(MEM_HEADS):
        hs = slice(h * MEM_HD, (h + 1) * MEM_HD)
        s = lax.dot_general(q[:, hs].astype(BF16), mk_ref[:, hs], NT_DIMS, preferred_element_type=F32) * MEM_HD ** -0.5
        e = jnp.exp(s - jnp.max(s, axis=-1, keepdims=True))
        oh = jnp.dot(e.astype(BF16), mv_ref[:, hs], preferred_element_type=F32) / jnp.sum(e, axis=-1, keepdims=True)
        outs.append(oh.astype(BF16))
    o = jnp.concatenate(outs, axis=1)
    o_ref[...] = x_ref[...] + jnp.dot(o, wo_ref[...], preferred_element_type=F32)


def prompt_mem_attend(x, g, wq, mk, mv, wo, tm=512):
    m, d = x.shape
    n_mem = mk.shape[0]
    full = lambda shape: pl.BlockSpec(shape, lambda i: (0,) * len(shape))
    return pl.pallas_call(
        _pmem_kernel,
        grid=(m // tm,),
        in_specs=[pl.BlockSpec((tm, d), lambda i: (i, 0)), full((1, d)), full((d, MEM_W)), full((n_mem, MEM_W)),
                  full((n_mem, MEM_W)), full((MEM_W, d))],
        out_specs=pl.BlockSpec((tm, d), lambda i: (i, 0)),
        out_shape=jax.ShapeDtypeStruct((m, d), F32),
        compiler_params=_cparams(("parallel",), 48 * 1024 * 1024),
    )(x, g.reshape(1, d), wq, mk, mv, wo)


def _conv_seq(a, prev, cw, cb, row):
    p1 = prev[7:8, :]
    p2 = prev[6:7, :]
    r1 = jnp.where(row == 0, p1, pltpu.roll(a, 1, axis=0))
    r2 = jnp.where(row == 0, p2, jnp.where(row == 1, p1, pltpu.roll(a, 2, axis=0)))
    return cb + cw[0:1, :] * r2 + cw[1:2, :] * r1 + cw[2:3, :] * a


def _ffn_seq_kernel(x_ref, g_ref, wg_ref, wu_ref, cwg_ref, cwu_ref, cbg_ref, cbu_ref, wd_ref,
                    o_ref, lg_ref, lu_ref, carry_ref, h_ref):
    i = pl.program_id(0)
    j = pl.program_id(1)
    tm = x_ref.shape[0]
    row = lax.broadcasted_iota(I32, (tm, 1), 0)

    @pl.when(j == 0)
    def _():
        h_ref[...] = _rmsnorm_bf16(x_ref[...], g_ref[...])
        o_ref[...] = x_ref[...]

    @pl.when(i == 0)
    def _():
        carry_ref[j] = jnp.zeros(carry_ref.shape[1:], F32)

    hx = h_ref[...]
    f = None
    for s in range(FFN_TF // FFN_SUB):
        cs = slice(s * FFN_SUB, (s + 1) * FFN_SUB)
        ag = jnp.dot(hx, wg_ref[:, cs], preferred_element_type=F32)
        au = jnp.dot(hx, wu_ref[:, cs], preferred_element_type=F32)
        lg_ref[0, :, cs] = ag[tm - 8:tm, :]
        lu_ref[0, :, cs] = au[tm - 8:tm, :]
        cg = _conv_seq(ag, carry_ref[j, 0, :, cs], cwg_ref[:, cs], cbg_ref[:, cs], row)
        cu = _conv_seq(au, carry_ref[j, 1, :, cs], cwu_ref[:, cs], cbu_ref[:, cs], row)
        carry_ref[j, 0, :, cs] = ag[tm - 8:tm, :]
        carry_ref[j, 1, :, cs] = au[tm - 8:tm, :]
        act = ((cg * jax.nn.sigmoid(cg)) * cu).astype(BF16)
        fs = jnp.dot(act, wd_ref[cs, :], preferred_element_type=F32)
        f = fs if f is None else f + fs
    o_ref[...] += f


def ffn_seq(x, g, ffn_w, tm=512):
    wg, wu, cwg, cwu, cbg, cbu, wd = ffn_w
    m, d = x.shape
    tf = FFN_TF
    nf = D_FF_PAD // tf
    col = lambda r: pl.BlockSpec((r, tf), lambda i, j: (0, j))
    tail = pl.BlockSpec((1, 8, tf), lambda i, j: (i, 0, j))
    return pl.pallas_call(
        _ffn_seq_kernel,
        grid=(m // tm, nf),
        in_specs=[pl.BlockSpec((tm, d), lambda i, j: (i, 0)), pl.BlockSpec((1, d), lambda i, j: (0, 0)),
                  col(d), col(d), col(CONV_W), col(CONV_W), col(1), col(1),
                  pl.BlockSpec((tf, d), lambda i, j: (j, 0))],
        out_specs=[pl.BlockSpec((tm, d), lambda i, j: (i, 0)), tail, tail],
        out_shape=[jax.ShapeDtypeStruct((m, d), F32), jax.ShapeDtypeStruct((m // tm, 8, D_FF_PAD), F32),
                   jax.ShapeDtypeStruct((m // tm, 8, D_FF_PAD), F32)],
        scratch_shapes=[pltpu.VMEM((nf, 2, 8, tf), F32), pltpu.VMEM((tm, d), BF16)],
        compiler_params=_cparams(("arbitrary", "arbitrary"), 48 * 1024 * 1024),
    )(x, g.reshape(1, d), wg, wu, cwg, cwu, cbg, cbu, wd)


def _ffn_step_kernel(x_ref, g_ref, wg_ref, wu_ref, cwg_ref, cwu_ref, cbg_ref, cbu_ref, wd_ref,
                     pg0_ref, pg1_ref, pu0_ref, pu1_ref, o_ref, ag_ref, au_ref, h_ref):
    j = pl.program_id(0)

    @pl.when(j == 0)
    def _():
        h_ref[...] = _rmsnorm_bf16(x_ref[...], g_ref[...])
        o_ref[...] = x_ref[...]

    hx = h_ref[...]
    ag = jnp.dot(hx, wg_ref[...], preferred_element_type=F32)
    au = jnp.dot(hx, wu_ref[...], preferred_element_type=F32)
    ag_ref[...] = ag
    au_ref[...] = au
    cg = cbg_ref[...] + cwg_ref[0:1, :] * pg0_ref[...] + cwg_ref[1:2, :] * pg1_ref[...] + cwg_ref[2:3, :] * ag
    cu = cbu_ref[...] + cwu_ref[0:1, :] * pu0_ref[...] + cwu_ref[1:2, :] * pu1_ref[...] + cwu_ref[2:3, :] * au
    act = ((cg * jax.nn.sigmoid(cg)) * cu).astype(BF16)
    o_ref[...] += jnp.dot(act, wd_ref[...], preferred_element_type=F32)


def ffn_step(x, g, ffn_w, prev):
    wg, wu, cwg, cwu, cbg, cbu, wd = ffn_w
    m, d = x.shape
    tf = FFN_TF
    nf = D_FF_PAD // tf
    col = lambda r: pl.BlockSpec((r, tf), lambda j: (0, j))
    row = pl.BlockSpec((m, d), lambda j: (0, 0))
    return pl.pallas_call(
        _ffn_step_kernel,
        grid=(nf,),
        in_specs=[row, pl.BlockSpec((1, d), lambda j: (0, 0)), col(d), col(d), col(CONV_W), col(CONV_W), col(1), col(1),
                  pl.BlockSpec((tf, d), lambda j: (j, 0)), col(m), col(m), col(m), col(m)],
        out_specs=[row, col(m), col(m)],
        out_shape=[jax.ShapeDtypeStruct((m, d), F32), jax.ShapeDtypeStruct((m, D_FF_PAD), F32),
                   jax.ShapeDtypeStruct((m, D_FF_PAD), F32)],
        scratch_shapes=[pltpu.VMEM((m, d), BF16)],
        compiler_params=_cparams(("arbitrary",), 48 * 1024 * 1024),
    )(x, g.reshape(1, d), wg, wu, cwg, cwu, cbg, cbu, wd, *prev)


def _split_rows_kernel(s_ref, o_ref):
    for j in range(o_ref.shape[1]):
        o_ref[0, j] = s_ref[0, :, j, :]


def split_conv_state(state):
    depth, nb, rows, feat = state.shape
    return pl.pallas_call(
        _split_rows_kernel,
        grid=(depth,),
        in_specs=[pl.BlockSpec((1, nb, rows, feat), lambda l: (l, 0, 0, 0))],
        out_specs=pl.BlockSpec((1, rows, nb, feat), lambda l: (l, 0, 0, 0)),
        out_shape=jax.ShapeDtypeStruct((depth, rows, nb, feat), state.dtype),
        compiler_params=_cparams(("parallel",), 48 * 1024 * 1024),
    )(state)


def _merge_rows_kernel(s_ref, a_ref, o_ref):
    rows = o_ref.shape[2]
    for j in range(rows - 1):
        o_ref[0, :, j, :] = s_ref[0, j + 1]
    o_ref[0, :, rows - 1, :] = a_ref[0]


def merge_conv_state(split_state, new_rows):
    depth, rows, nb, feat = split_state.shape
    return pl.pallas_call(
        _merge_rows_kernel,
        grid=(depth,),
        in_specs=[pl.BlockSpec((1, rows, nb, feat), lambda l: (l, 0, 0, 0)),
                  pl.BlockSpec((1, nb, feat), lambda l: (l, 0, 0))],
        out_specs=pl.BlockSpec((1, nb, rows, feat), lambda l: (l, 0, 0, 0)),
        out_shape=jax.ShapeDtypeStruct((depth, nb, rows, feat), split_state.dtype),
        compiler_params=_cparams(("parallel",), 48 * 1024 * 1024),
    )(split_state, new_rows)


def _smix_kernel(qc_ref, kc_ref, v_ref, rg_ref, gu_ref, gv_ref, st_ref, cos_ref, sin_ref, g1_ref, retg_ref,
                 gmg_ref, wrow_ref, brow_ref, a_ref, c_ref, ns_ref, gmv_ref):
    cosc = cos_ref[...]
    sinc = sin_ref[...]
    half = RET_DK // 2

    def rot(x):
        x1, x2 = x[:half], x[half:]
        return jnp.concatenate([x1 * cosc - x2 * sinc, x1 * sinc + x2 * cosc], axis=0)

    for h in range(RET_HEADS):
        vs = slice(h * RET_DV, (h + 1) * RET_DV)
        qr = rot(qc_ref[0, h])
        kr = rot(kc_ref[0, h]) * RET_DK ** -0.5
        v = v_ref[0, :, vs]
        state = st_ref[0, h]
        g1 = g1_ref[h]
        inner = jnp.sum(qr * kr, axis=0, keepdims=True) * v
        cross = jnp.sum(qr * state, axis=0, keepdims=True) * g1
        o = inner + cross
        ns_ref[0, h] = g1 * state + kr * v
        ms = jnp.mean(o * o, axis=-1, keepdims=True)
        y = o * lax.rsqrt(ms + EPS) * retg_ref[:, vs]
        g = rg_ref[0, :, vs]
        a_ref[0, :, vs] = y * (g * jax.nn.sigmoid(g))

    u = jax.nn.gelu(gu_ref[0])
    v = jax.nn.gelu(gv_ref[0])
    ms = jnp.mean(v * v, axis=-1, keepdims=True)
    vn = v * lax.rsqrt(ms + EPS) * gmg_ref[...]
    c_ref[0] = u * (wrow_ref[...] * vn + brow_ref[...])
    gmv_ref[0] = vn


def sample_mixers(p_main, state, consts, retg_row, gmg_row, wrow, brow):
    nb = p_main.shape[0]
    cosc, sinc, g1 = consts
    qc = p_main[:, 0:RET_QK].reshape(nb, RET_HEADS, RET_DK, 1)
    kc = p_main[:, RET_QK:2 * RET_QK].reshape(nb, RET_HEADS, RET_DK, 1)
    p3 = p_main.reshape(nb, 1, p_main.shape[1])
    per_b = lambda width, blk: pl.BlockSpec((1, 1, width), lambda b: (b, 0, blk))
    col4 = pl.BlockSpec((1, RET_HEADS, RET_DK, 1), lambda b: (b, 0, 0, 0))
    st4 = pl.BlockSpec((1, RET_HEADS, RET_DK, RET_DV), lambda b: (b, 0, 0, 0))
    full = lambda shape: pl.BlockSpec(shape, lambda b: (0,) * len(shape))
    return pl.pallas_call(
        _smix_kernel,
        grid=(nb,),
        in_specs=[col4, col4, per_b(1024, 1), per_b(1024, 2), per_b(512, _M_GU // 512), per_b(512, _M_GV // 512), st4,
                  full(cosc.shape), full(sinc.shape), full(g1.shape), full(retg_row.shape), full(gmg_row.shape),
                  full(wrow.shape), full(brow.shape)],
        out_specs=[per_b(RET_V, 0), per_b(GM_W, 0), st4, per_b(GM_W, 0)],
        out_shape=[jax.ShapeDtypeStruct((nb, 1, RET_V), F32), jax.ShapeDtypeStruct((nb, 1, GM_W), F32),
                   jax.ShapeDtypeStruct(state.shape, F32), jax.ShapeDtypeStruct((nb, 1, GM_W), F32)],
        compiler_params=_cparams(("parallel",)),
    )(qc, kc, p3, p3, p3, p3, state, cosc, sinc, g1, retg_row, gmg_row, wrow, brow)


def _page_scores_kernel(pt_ref, qx_ref, w_ref, q8_ref, *refs, n_pg):
    ki_refs = refs[:n_pg]
    k_refs = refs[n_pg:2 * n_pg]
    isc_ref, asc_ref = refs[2 * n_pg:]
    qx = qx_ref[0]
    q_hi = qx[:IDX_HEADS]
    w = w_ref[0]
    q8 = q8_ref[0]
    for n in range(n_pg):
        kp = ki_refs[n][0]
        k_hi = kp.astype(BF16)
        k_lo = (kp - k_hi.astype(F32)).astype(BF16)
        s2 = lax.dot_general(qx, k_hi, NT_DIMS, preferred_element_type=F32)
        s = s2[:IDX_HEADS] + s2[IDX_HEADS:] + lax.dot_general(q_hi, k_lo, NT_DIMS, preferred_element_type=F32)
        isc_ref[0, n] = jnp.sum(jnp.maximum(s, 0.0) * w, axis=0, keepdims=True)
        asc_ref[0, n] = lax.dot_general(q8, k_refs[n][0].astype(BF16), NT_DIMS, preferred_element_type=F32)


def page_scores(page_table, qx, w_col, q8, pool_ki, pool_k, n_pg):
    nb, n_pages = page_table.shape
    groups = n_pages // n_pg
    pt = page_table.reshape(-1)

    def pool_spec(shape, n):
        return pl.BlockSpec((1,) + shape, lambda b, g, pt_ref: (pt_ref[b * n_pages + g * n_pg + n], 0, 0))

    per_b = lambda shape: pl.BlockSpec((1,) + shape, lambda b, g, pt_ref: (b,) + (0,) * len(shape))
    grid_spec = pltpu.PrefetchScalarGridSpec(
        num_scalar_prefetch=1,
        grid=(nb, groups),
        in_specs=[per_b((2 * IDX_HEADS, IDX_DIM)), per_b((IDX_HEADS, 1)), per_b((8, ATT_HD))]
        + [pool_spec((PAGE_SIZE, IDX_DIM), n) for n in range(n_pg)]
        + [pool_spec((PAGE_ROWS, ATT_HD), n) for n in range(n_pg)],
        out_specs=[pl.BlockSpec((1, n_pg, 1, PAGE_SIZE), lambda b, g, pt_ref: (b, g, 0, 0)),
                   pl.BlockSpec((1, n_pg, 8, PAGE_ROWS), lambda b, g, pt_ref: (b, g, 0, 0))],
    )
    return pl.pallas_call(
        functools.partial(_page_scores_kernel, n_pg=n_pg),
        grid_spec=grid_spec,
        out_shape=[jax.ShapeDtypeStruct((nb, n_pages, 1, PAGE_SIZE), F32),
                   jax.ShapeDtypeStruct((nb, n_pages, 8, PAGE_ROWS), F32)],
        compiler_params=_cparams(("arbitrary", "arbitrary"), 48 * 1024 * 1024),
    )(pt, qx, w_col, q8, *([pool_ki] * n_pg), *([pool_k] * n_pg))


def _select_softmax_kernel(isc_ref, isce_ref, asc_ref, p_ref, keys_ref, thr_ref, keep_ref, *, topk, n_valid):
    h = pl.program_id(0)
    nb = isc_ref.shape[0]

    @pl.when(h == 0)
    def _():
        colid = lax.broadcasted_iota(I32, isc_ref.shape, 1)
        keys_ref[...] = jnp.where(colid < n_valid, _sortable_key(isc_ref[...]), INT_MIN)

        def count_ge(mid):
            return jnp.sum((keys_ref[...] >= mid).astype(I32), axis=1, keepdims=True)

        t_k = _bisect_threshold(count_ge, jnp.full((nb, 1), INT_MIN, I32), jnp.full((nb, 1), INT_MAX, I32), topk)
        thr_ref[...] = jnp.broadcast_to(jnp.maximum(t_k, INT_MIN + 1), thr_ref.shape)
        keep_ref[...] = jnp.full(keep_ref.shape, INT_MAX, I32)

        over = jnp.logical_and(count_ge(t_k) > topk, t_k > INT_MIN)

        @pl.when(jnp.max(over.astype(I32)) > 0)
        def _():
            need = topk - count_ge(t_k + 1)

            def count_tied_upto(j):
                tied = jnp.where(keys_ref[...] == t_k, jnp.where(colid <= j, 1, 0), 0)
                return jnp.sum(tied, axis=1, keepdims=True)

            def step(_, lh):
                lo_j, hi_j = lh
                mid = (lo_j + hi_j) >> 1
                ok = count_tied_upto(mid) >= need
                return jnp.where(ok, lo_j, mid), jnp.where(ok, mid, hi_j)

            n_keys = isc_ref.shape[1]
            _, j_keep = lax.fori_loop(0, n_keys.bit_length(), step,
                                      (jnp.full((nb, 1), -1, I32), jnp.full((nb, 1), n_keys - 1, I32)))
            keep_ref[...] = jnp.broadcast_to(jnp.where(over, j_keep, INT_MAX), keep_ref.shape)

    ce = lax.broadcasted_iota(I32, isce_ref.shape, 1)
    ke = _sortable_key(isce_ref[...])
    thr = thr_ref[:, 0:1]
    key_pos = ce >> (ATT_HEADS.bit_length() - 1)
    tied_ok = jnp.where(key_pos <= keep_ref[:, 0:1], asc_ref[0], -jnp.inf)
    s = jnp.where(ke > thr, asc_ref[0], jnp.where(ke == thr, tied_ok, -jnp.inf))
    s = jnp.where(ce < ATT_HEADS * n_valid, s, -jnp.inf)
    s = jnp.where((ce & (ATT_HEADS - 1)) == h, s, -jnp.inf)
    e = jnp.exp(s - jnp.max(s, axis=1, keepdims=True))
    p_ref[0] = e / jnp.sum(e, axis=1, keepdims=True)


def select_softmax(isc, isc_e, asc, topk, n_valid):
    nb = isc.shape[0]
    head_blk = pl.BlockSpec((1,) + asc.shape[1:], lambda h: (h, 0, 0))
    whole = lambda a: pl.BlockSpec(a.shape, lambda h: (0, 0))
    return pl.pallas_call(
        functools.partial(_select_softmax_kernel, topk=topk, n_valid=n_valid),
        grid=(ATT_HEADS,),
        in_specs=[whole(isc), whole(isc_e), head_blk],
        out_specs=head_blk,
        out_shape=jax.ShapeDtypeStruct(asc.shape, F32),
        scratch_shapes=[pltpu.VMEM(isc.shape, I32), pltpu.VMEM((nb, LANE), I32), pltpu.VMEM((nb, LANE), I32)],
        compiler_params=_cparams(("arbitrary",), 48 * 1024 * 1024),
    )(isc, isc_e, asc)


def _page_values_kernel(pt_ref, p_ref, *refs, n_pg):
    v_refs = refs[:n_pg]
    o_ref = refs[n_pg]
    acc = jnp.zeros(o_ref.shape[1:], F32)
    for n in range(n_pg):
        acc = acc + jnp.dot(p_ref[0, n], v_refs[n][0].astype(BF16), preferred_element_type=F32)

    @pl.when(pl.program_id(1) == 0)
    def _():
        o_ref[0] = acc

    @pl.when(pl.program_id(1) > 0)
    def _():
        o_ref[0] += acc


def page_values(page_table, p_pages, pool_v, n_pg):
    nb, n_pages = page_table.shape
    groups = n_pages // n_pg
    pt = page_table.reshape(-1)

    def pool_spec(n):
        return pl.BlockSpec((1, PAGE_ROWS, ATT_HD), lambda b, g, pt_ref: (pt_ref[b * n_pages + g * n_pg + n], 0, 0))

    grid_spec = pltpu.PrefetchScalarGridSpec(
        num_scalar_prefetch=1,
        grid=(nb, groups),
        in_specs=[pl.BlockSpec((1, n_pg, 8, PAGE_ROWS), lambda b, g, pt_ref: (b, g, 0, 0))]
        + [pool_spec(n) for n in range(n_pg)],
        out_specs=pl.BlockSpec((1, 8, ATT_HD), lambda b, g, pt_ref: (b, 0, 0)),
    )
    return pl.pallas_call(
        functools.partial(_page_values_kernel, n_pg=n_pg),
        grid_spec=grid_spec,
        out_shape=jax.ShapeDtypeStruct((nb, 8, ATT_HD), F32),
        compiler_params=_cparams(("arbitrary", "arbitrary"), 48 * 1024 * 1024),
    )(pt, p_pages, *([pool_v] * n_pg))


def _head_rows(q, scale):
    nb = q.shape[0]
    q3 = (q * scale).reshape(nb, ATT_HEADS, ATT_HD)
    return jnp.pad(q3, ((0, 0), (0, 8 - ATT_HEADS), (0, 0))).astype(BF16)


def dsa_sample(aq, ak, av, iq, iw, ik, pool_k, pool_v, pool_ki, page_table, n_pg=32):
    nb, n_pages = page_table.shape
    past = n_pages * PAGE_SIZE
    n_keys = past + PAGE_SIZE
    topk = min(TOPK_MAX, (past + 1) // 4)
    iq3 = iq.reshape(nb, IDX_HEADS, IDX_DIM)
    q_hi = iq3.astype(BF16)
    q_lo = (iq3 - q_hi.astype(F32)).astype(BF16)
    qx = jnp.concatenate([q_hi, q_lo], axis=1)
    w_col = (iw * (IDX_DIM ** -0.5 * IDX_HEADS ** -0.5)).reshape(nb, IDX_HEADS, 1)
    q8 = _head_rows(aq, ATT_HD ** -0.5)
    own = jnp.arange(nb, dtype=I32).reshape(nb, 1)
    pad_rows = lambda a, rows: jnp.pad(a, ((0, 0), (0, rows - a.shape[1]), (0, 0)))
    own_ki = pad_rows(ik[:, None, :], PAGE_SIZE)
    own_k = pad_rows(ak.reshape(nb, ATT_HEADS, ATT_HD), PAGE_ROWS)
    own_v = pad_rows(av.reshape(nb, ATT_HEADS, ATT_HD), PAGE_ROWS)
    isc_p, asc_p = page_scores(page_table, qx, w_col, q8, pool_ki, pool_k, n_pg)
    isc_n, asc_n = page_scores(own, qx, w_col, q8, own_ki, own_k, 1)
    isc = jnp.concatenate([isc_p, isc_n], axis=1).reshape(nb, n_keys)
    asc = jnp.concatenate([asc_p, asc_n], axis=1)[:, :, :ATT_HEADS]
    asc = asc.transpose(2, 0, 1, 3).reshape(ATT_HEADS, nb, n_keys * ATT_HEADS)
    p = select_softmax(isc, jnp.repeat(isc, ATT_HEADS, axis=1), asc, topk, past + 1)
    p = p.reshape(ATT_HEADS, nb, n_pages + 1, PAGE_ROWS).transpose(1, 2, 0, 3)
    p = jnp.pad(p, ((0, 0), (0, 0), (0, 8 - ATT_HEADS), (0, 0))).astype(BF16)
    o_p = page_values(page_table, p[:, :n_pages], pool_v, n_pg)
    o_n = page_values(own, p[:, n_pages:], own_v, 1)
    return (o_p + o_n)[:, :ATT_HEADS].reshape(nb, ATT_W)


def _smem_kernel(q8_ref, mk_ref, mv_ref, o_ref):
    rows = mk_ref.shape[1]
    head_of = lambda axis: lax.broadcasted_iota(I32, (8, rows), axis) & (MEM_HEADS - 1)
    own_head = head_of(1) == head_of(0)
    for b in range(q8_ref.shape[0]):
        s = lax.dot_general(q8_ref[b], mk_ref[b].astype(BF16), NT_DIMS, preferred_element_type=F32)
        s = jnp.where(own_head, s, -jnp.inf)
        e = jnp.exp(s - jnp.max(s, axis=-1, keepdims=True))
        p = e / jnp.sum(e, axis=-1, keepdims=True)
        o_ref[b] = jnp.dot(p.astype(BF16), mv_ref[b].astype(BF16), preferred_element_type=F32)


def sample_mem_core(q, mem_k, mem_v, layer, bb=8):
    nb = q.shape[0]
    rows = mem_k.shape[1]
    q8 = _head_rows(q, MEM_HD ** -0.5)
    first = layer * (nb // bb)
    o = pl.pallas_call(
        _smem_kernel,
        grid=(nb // bb,),
        in_specs=[pl.BlockSpec((bb, 8, MEM_HD), lambda i: (i, 0, 0)),
                  pl.BlockSpec((bb, rows, MEM_HD), lambda i: (first + i, 0, 0)),
                  pl.BlockSpec((bb, rows, MEM_HD), lambda i: (first + i, 0, 0))],
        out_specs=pl.BlockSpec((bb, 8, MEM_HD), lambda i: (i, 0, 0)),
        out_shape=jax.ShapeDtypeStruct((nb, 8, MEM_HD), F32),
        compiler_params=_cparams(("parallel",), 48 * 1024 * 1024),
    )(q8, mem_k, mem_v)
    return o[:, :MEM_HEADS].reshape(nb, MEM_W)


def _prep_layer(w_in, w_out, w_mq, w_mk, w_mv, w_mo, w_up, conv_w, conv_b, w_down, gm_ws, gm_bs):
    w_in = w_in.astype(BF16)
    w_proj = jnp.concatenate([w_in[:, :_O_IQ_END], w_in[:, _O_GU:], w_in[:, _O_IK:_O_GU], w_in[:, _O_IW:_O_IK],
                              jnp.zeros((D_MODEL, N_PROJ - N_MAIN - IDX_DIM - IDX_HEADS), BF16)], axis=1)
    padc = lambda a: jnp.pad(a, ((0, 0), (0, D_FF_PAD - D_FF)))
    ffn_w = (padc(w_up[:, :D_FF]).astype(BF16), padc(w_up[:, D_FF:]).astype(BF16),
             padc(conv_w[:, :D_FF]), padc(conv_w[:, D_FF:]),
             padc(conv_b[None, :D_FF]), padc(conv_b[None, D_FF:]),
             jnp.pad(w_down, ((0, D_FF_PAD - D_FF), (0, 0))).astype(BF16))
    causal = jnp.tril(jnp.ones((CHUNK, CHUNK), dtype=bool))
    wtril = jnp.where(causal[None], gm_ws, 0.0).astype(BF16)
    bsm = jnp.broadcast_to(gm_bs[:, :, None], (GM_GROUPS, CHUNK, GM_GW))
    wrow = jnp.repeat(gm_ws[:, 0, 0], GM_GW).reshape(1, GM_W)
    brow = jnp.repeat(gm_bs[:, 0], GM_GW).reshape(1, GM_W)
    return dict(w_proj=w_proj, w_out=w_out.astype(BF16), w_mq=w_mq.astype(BF16),
                w_mkv=jnp.concatenate([w_mk, w_mv], axis=1).astype(BF16), w_mo=w_mo.astype(BF16), ffn_w=ffn_w,
                wtril=wtril, bsm=bsm, wrow=wrow, brow=brow)


def _retention_consts(seq, past):
    log_g = jnp.log(1.0 - 2.0 ** (-5.0 - jnp.arange(RET_HEADS, dtype=F32)))
    half = RET_DK // 2
    inv = ROPE_BASE ** (-jnp.arange(half, dtype=F32) / half)
    ang = jnp.arange(seq).astype(F32)[:, None] * inv[None, :]
    cos, sin = jnp.cos(ang), jnp.sin(ang)
    cosf = jnp.concatenate([cos, cos], axis=1)
    sinf = jnp.concatenate([-sin, sin], axis=1)
    n = jnp.arange(CHUNK, dtype=F32)
    diff = n[:, None] - n[None, :]
    causal = diff >= 0
    dmat = jnp.where(causal[None], jnp.exp(log_g[:, None, None] * jnp.where(causal, diff, 0.0)[None]), 0.0)
    cdec = jnp.broadcast_to(jnp.exp(log_g[:, None] * (n[None, :] + 1.0))[:, :, None], (RET_HEADS, CHUNK, RET_DV))
    kdec = jnp.broadcast_to(jnp.exp(log_g[:, None] * (CHUNK - 1.0 - n[None, :]))[:, :, None], (RET_HEADS, CHUNK, RET_DK))
    gc = jnp.broadcast_to(jnp.exp(log_g * CHUNK)[:, None, None], (RET_HEADS, RET_DK, RET_DV))
    ang_s = jnp.full((1,), past, dtype=F32)[:, None] * inv[None, :]
    cosc = jnp.cos(ang_s).reshape(half, 1)
    sinc = jnp.sin(ang_s).reshape(half, 1)
    g1 = jnp.broadcast_to(jnp.exp(log_g * 1.0)[:, None, None], (RET_HEADS, 1, RET_DV))
    return (cosf, sinf, dmat, cdec, kdec, gc), (cosc, sinc, g1)


def _pad_ff(a):
    return jnp.pad(a, ((0, 0), (0, D_FF_PAD - D_FF)))


def kernel(x_prompt, x_sample, mem_prompt, cache_k, cache_v, cache_kidx, page_table, cache_mem_k, cache_mem_v, state_ret, state_conv, norm_mix_g, w_in, ret_norm_g, gm_norm_g, gm_ws, gm_bs, w_out, norm_mem_g, mem_in_g, w_mq, w_mk, w_mv, w_mo, norm_ffn_g, w_up, conv_w, conv_b, w_down, final_norm_g):
    depth = w_in.shape[0]
    seq = x_prompt.shape[1]
    nbd = x_sample.shape[0]
    n_mem = mem_prompt.shape[1]
    past = page_table.shape[1] * PAGE_SIZE
    p_consts, s_consts = _retention_consts(seq, past)
    n_pool = cache_k.shape[1]
    pool_k = cache_k.reshape(depth * n_pool, PAGE_ROWS, ATT_HD)
    pool_v = cache_v.reshape(depth * n_pool, PAGE_ROWS, ATT_HD)
    pool_ki = cache_kidx.reshape(depth * n_pool, PAGE_SIZE, IDX_DIM)
    mem_k_all = cache_mem_k.reshape(depth * nbd, n_mem * MEM_HEADS, MEM_HD)
    mem_v_all = cache_mem_v.reshape(depth * nbd, n_mem * MEM_HEADS, MEM_HD)
    conv_rows = split_conv_state(state_conv)

    xp = x_prompt.reshape(seq, D_MODEL)
    xs = x_sample.reshape(nbd, D_MODEL)
    mem = mem_prompt.reshape(n_mem, D_MODEL)
    outs = {k: [] for k in ("kp", "vp", "kip", "ks", "vs", "kis", "mkp", "mvp", "rsp", "rss", "csp", "css", "gvp", "gvs")}

    for l in range(depth):
        w = _prep_layer(w_in[l], w_out[l], w_mq[l], w_mk[l], w_mv[l], w_mo[l], w_up[l], conv_w[l], conv_b[l],
                        w_down[l], gm_ws[l], gm_bs[l])
        retg_row = ret_norm_g[l].reshape(1, RET_V)
        gmg_row = gm_norm_g[l].reshape(1, GM_W)

        pm = norm_matmul(xp, norm_mix_g[l], w["w_proj"])
        ak, av = pm[:, _M_AK:_M_AV], pm[:, _M_AV:_M_IQ]
        ik = pm[:, _M_IK:_M_IW]
        a_out, c_out, r_state, gm_v = prompt_mixers(pm, p_consts, retg_row, gmg_row, w["wtril"], w["bsm"])
        b_out = dsa_prompt(pm[:, _M_AQ:_M_AK], ak, av, pm[:, _M_IQ:_M_GU], pm[:, _M_IW:_M_IW + IDX_HEADS], ik)
        xp = out_projection(a_out, b_out, c_out, w["w_out"], xp)
        mkv = norm_matmul(mem, mem_in_g[l], w["w_mkv"])
        mk, mv = mkv[:, :MEM_W], mkv[:, MEM_W:]
        xp = prompt_mem_attend(xp, norm_mem_g[l], w["w_mq"], mk.astype(BF16), mv.astype(BF16), w["w_mo"])
        xp, tail_g, tail_u = ffn_seq(xp, norm_ffn_g[l], w["ffn_w"])
        last_g, last_u = tail_g[-1], tail_u[-1]
        outs["kp"].append(ak.reshape(1, seq, ATT_HEADS, ATT_HD))
        outs["vp"].append(av.reshape(1, seq, ATT_HEADS, ATT_HD))
        outs["kip"].append(ik.reshape(1, seq, IDX_DIM))
        outs["mkp"].append(mk.reshape(1, n_mem, MEM_HEADS, MEM_HD))
        outs["mvp"].append(mv.reshape(1, n_mem, MEM_HEADS, MEM_HD))
        outs["rsp"].append(r_state.reshape(1, RET_HEADS, RET_DK, RET_DV))
        outs["csp"].append(jnp.concatenate([last_g[8 - (CONV_W - 1):, :D_FF], last_u[8 - (CONV_W - 1):, :D_FF]], axis=1)[None])
        outs["gvp"].append(gm_v[None])

        pm = norm_matmul(xs, norm_mix_g[l], w["w_proj"])
        ak, av = pm[:, _M_AK:_M_AV], pm[:, _M_AV:_M_IQ]
        ik = pm[:, _M_IK:_M_IW]
        a_out, c_out, r_state, gm_v = sample_mixers(pm, state_ret[l], s_consts, retg_row, gmg_row, w["wrow"], w["brow"])
        b_out = dsa_sample(pm[:, _M_AQ:_M_AK], ak, av, pm[:, _M_IQ:_M_GU], pm[:, _M_IW:_M_IW + IDX_HEADS], ik,
                           pool_k, pool_v, pool_ki, page_table + l * n_pool)
        xs = out_projection(a_out.reshape(nbd, RET_V), b_out, c_out.reshape(nbd, GM_W), w["w_out"], xs)
        q = norm_matmul(xs, norm_mem_g[l], w["w_mq"])
        o = sample_mem_core(q, mem_k_all, mem_v_all, l)
        xs = matmul(o.astype(BF16), w["w_mo"], res=xs)
        prev = (_pad_ff(conv_rows[l, 0, :, :D_FF]), _pad_ff(conv_rows[l, 1, :, :D_FF]),
                _pad_ff(conv_rows[l, 0, :, D_FF:]), _pad_ff(conv_rows[l, 1, :, D_FF:]))
        xs, a_g, a_u = ffn_step(xs, norm_ffn_g[l], w["ffn_w"], prev)
        a_new = jnp.concatenate([a_g[:, :D_FF], a_u[:, :D_FF]], axis=1)
        outs["ks"].append(ak.reshape(nbd, 1, ATT_HEADS, ATT_HD))
        outs["vs"].append(av.reshape(nbd, 1, ATT_HEADS, ATT_HD))
        outs["kis"].append(ik.reshape(nbd, 1, IDX_DIM))
        outs["rss"].append(r_state)
        outs["css"].append(a_new)
        outs["gvs"].append(gm_v)

    y_prompt = rmsnorm_rows(xp, final_norm_g, F32).reshape(1, seq, D_MODEL)
    y_sample = rmsnorm_rows(xs, final_norm_g, F32).reshape(nbd, 1, D_MODEL)
    st = lambda k: jnp.stack(outs[k])
    conv_state_sample = merge_conv_state(conv_rows, st("css"))
    return (y_prompt, y_sample, st("kp"), st("vp"), st("kip"), st("ks"), st("vs"), st("kis"), st("mkp"), st("mvp"),
            st("rsp"), st("rss"), st("csp"), conv_state_sample, st("gvp"), st("gvs"))
```

```python
import functools

import jax
import jax.numpy as jnp
from jax import lax
from jax.experimental import pallas as pl
from jax.experimental.pallas import tpu as pltpu

F32 = jnp.float32
BF16 = jnp.bfloat16
I32 = jnp.int32

D_MODEL = 2048
PAGE_SIZE = 128
RET_HEADS = 4
RET_DK = 128
RET_DV = 256
CHUNK = 128
ROPE_BASE = 10000.0
ATT_HEADS = 4
ATT_HD = 128
IDX_HEADS = 16
IDX_DIM = 64
TOPK_MAX = 256
GM_GROUPS = 4
GM_GW = 128
MEM_HEADS = 4
MEM_HD = 128
D_FF = 5504
CONV_W = 3
EPS = 1e-6

RET_QK = RET_HEADS * RET_DK
RET_V = RET_HEADS * RET_DV
ATT_W = ATT_HEADS * ATT_HD
IDX_Q = IDX_HEADS * IDX_DIM
GM_W = GM_GROUPS * GM_GW
MEM_W = MEM_HEADS * MEM_HD
PAGE_ROWS = PAGE_SIZE * ATT_HEADS

_O_IQ_END = 2 * RET_QK + 2 * RET_V + 3 * ATT_W + IDX_Q
_O_IW = _O_IQ_END
_O_IK = _O_IW + IDX_HEADS
_O_GU = _O_IK + IDX_DIM
_M_AQ = 2 * RET_QK + 2 * RET_V
_M_AK = _M_AQ + ATT_W
_M_AV = _M_AK + ATT_W
_M_IQ = _M_AV + ATT_W
_M_GU = _M_IQ + IDX_Q
_M_GV = _M_GU + GM_W
N_MAIN = _M_GV + GM_W
_M_IK = N_MAIN
_M_IW = _M_IK + IDX_DIM
N_PROJ = N_MAIN + 512
LANE = 128
D_FF_PAD = 5632
FFN_TF = 512
FFN_SUB = 512
DSA_GROUP = 512
DSA_SCORE_ROWS = 256
BISECT_STEPS_PER_CHECK = 4
BISECT_MAX_STEPS = 64
F32_LOWEST = float(jnp.finfo(jnp.float32).min)

INT_MIN = -2 ** 31
INT_MAX = 2 ** 31 - 1
V7X_VMEM_BYTES = 64 * 1024 * 1024
NT_DIMS = (((1,), (1,)), ((), ()))


def _cparams(semantics, vmem_bytes=None):
    return pltpu.CompilerParams(dimension_semantics=semantics, vmem_limit_bytes=vmem_bytes)


def _bisect_threshold(count_ge, lo, hi, topk):
    def cond(st):
        it, active, _, _ = st
        return jnp.logical_and(it < BISECT_MAX_STEPS, active > 0)

    def body(st):
        it, _, lo, hi = st
        for _ in range(BISECT_STEPS_PER_CHECK):
            mid = lo + (hi - lo) * 0.5
            cnt = count_ge(mid)
            ge = cnt >= topk
            lo = jnp.where(ge, mid, lo)
            hi = jnp.where(cnt == topk, mid, jnp.where(ge, hi, mid))
        mid = lo + (hi - lo) * 0.5
        active = jnp.max(jnp.where(mid > lo, jnp.where(mid < hi, 1, 0), 0))
        return it + BISECT_STEPS_PER_CHECK, active, lo, hi

    _, _, lo, hi = lax.while_loop(cond, body, (jnp.int32(0), jnp.int32(1), lo, hi))
    return lo, hi


def _bracket_top(vmax):
    return jnp.where(vmax > 0.0, vmax * 2.0, vmax * 0.5) + 1e-30


def _rmsnorm_bf16(x, g):
    ms = jnp.mean(x * x, axis=-1, keepdims=True)
    return (x * lax.rsqrt(ms + EPS) * g).astype(BF16)


def _norm_kernel(x_ref, g_ref, o_ref):
    x = x_ref[...]
    ms = jnp.mean(x * x, axis=-1, keepdims=True)
    o_ref[...] = (x * lax.rsqrt(ms + EPS) * g_ref[...]).astype(o_ref.dtype)


def rmsnorm_rows(x, g, out_dtype):
    m, d = x.shape
    tm = min(m, 256)
    return pl.pallas_call(
        _norm_kernel,
        grid=(m // tm,),
        in_specs=[pl.BlockSpec((tm, d), lambda i: (i, 0)), pl.BlockSpec((1, d), lambda i: (0, 0))],
        out_specs=pl.BlockSpec((tm, d), lambda i: (i, 0)),
        out_shape=jax.ShapeDtypeStruct((m, d), out_dtype),
        compiler_params=_cparams(("parallel",)),
    )(x, g.reshape(1, d))


def _mm_kernel(x_ref, w_ref, o_ref):
    o_ref[...] = jnp.dot(x_ref[...], w_ref[...], preferred_element_type=F32)


def _mm_res_kernel(x_ref, w_ref, r_ref, o_ref):
    o_ref[...] = r_ref[...] + jnp.dot(x_ref[...], w_ref[...], preferred_element_type=F32)


def matmul(x, w, res=None, tm=1024, tn=512):
    m, k = x.shape
    n = w.shape[1]
    tm = min(tm, m)
    tn = min(tn, n)
    assert m % tm == 0 and n % tn == 0
    in_specs = [pl.BlockSpec((tm, k), lambda i, j: (i, 0)), pl.BlockSpec((k, tn), lambda i, j: (0, j))]
    args = [x, w]
    body = _mm_kernel
    if res is not None:
        in_specs.append(pl.BlockSpec((tm, tn), lambda i, j: (i, j)))
        args.append(res)
        body = _mm_res_kernel
    return pl.pallas_call(
        body,
        grid=(m // tm, n // tn),
        in_specs=in_specs,
        out_specs=pl.BlockSpec((tm, tn), lambda i, j: (i, j)),
        out_shape=jax.ShapeDtypeStruct((m, n), F32),
        compiler_params=_cparams(("parallel", "parallel"), 48 * 1024 * 1024),
    )(*args)


def _out_proj_kernel(a_ref, b_ref, c_ref, wa_ref, wb_ref, wc_ref, r_ref, o_ref):
    acc = jnp.dot(a_ref[...].astype(BF16), wa_ref[...], preferred_element_type=F32)
    acc = acc + jnp.dot(b_ref[...].astype(BF16), wb_ref[...], preferred_element_type=F32)
    acc = acc + jnp.dot(c_ref[...].astype(BF16), wc_ref[...], preferred_element_type=F32)
    o_ref[...] = r_ref[...] + acc


def out_projection(a, b, c, w_out, res, tm=1024, tn=512):
    m = a.shape[0]
    n = w_out.shape[1]
    tm = min(tm, m)
    ka, kb, kc = a.shape[1], b.shape[1], c.shape[1]
    assert m % tm == 0 and n % tn == 0 and ka % kb == 0 and kb == kc
    rows = lambda k: pl.BlockSpec((tm, k), lambda i, j: (i, 0))
    return pl.pallas_call(
        _out_proj_kernel,
        grid=(m // tm, n // tn),
        in_specs=[rows(ka), rows(kb), rows(kc),
                  pl.BlockSpec((ka, tn), lambda i, j: (0, j)),
                  pl.BlockSpec((kb, tn), lambda i, j: (ka // kb, j)),
                  pl.BlockSpec((kc, tn), lambda i, j: (ka // kb + 1, j)),
                  pl.BlockSpec((tm, tn), lambda i, j: (i, j))],
        out_specs=pl.BlockSpec((tm, tn), lambda i, j: (i, j)),
        out_shape=jax.ShapeDtypeStruct((m, n), F32),
        compiler_params=_cparams(("parallel", "parallel"), 48 * 1024 * 1024),
    )(a, b, c, w_out, w_out, w_out, res)


def _norm_mm_kernel(x_ref, g_ref, w_ref, o_ref, h_ref):
    @pl.when(pl.program_id(1) == 0)
    def _():
        h_ref[...] = _rmsnorm_bf16(x_ref[...], g_ref[...])

    o_ref[...] = jnp.dot(h_ref[...], w_ref[...], preferred_element_type=F32)


def norm_matmul(x, g, w, tm=1024, tn=512):
    m, k = x.shape
    n = w.shape[1]
    tm = min(tm, m)
    tn = min(tn, n)
    assert m % tm == 0 and n % tn == 0
    return pl.pallas_call(
        _norm_mm_kernel,
        grid=(m // tm, n // tn),
        in_specs=[pl.BlockSpec((tm, k), lambda i, j: (i, 0)), pl.BlockSpec((1, k), lambda i, j: (0, 0)),
                  pl.BlockSpec((k, tn), lambda i, j: (0, j))],
        out_specs=pl.BlockSpec((tm, tn), lambda i, j: (i, j)),
        out_shape=jax.ShapeDtypeStruct((m, n), F32),
        scratch_shapes=[pltpu.VMEM((tm, k), BF16)],
        compiler_params=_cparams(("arbitrary", "arbitrary"), 48 * 1024 * 1024),
    )(x, g.reshape(1, k), w)


def _mix_kernel(rq_ref, rk_ref, rv_ref, rg_ref, gu_ref, gv_ref, cos_ref, sin_ref, dmat_ref, cdec_ref, kdec_ref,
                gc_ref, retg_ref, gmg_ref, wtril_ref, bsm_ref, a_ref, c_ref, sfin_ref, gmv_ref, s_ref):
    c = pl.program_id(0)
    last = pl.num_programs(0) - 1

    @pl.when(c == 0)
    def _():
        s_ref[...] = jnp.zeros_like(s_ref)

    cosf = cos_ref[...]
    sinf = sin_ref[...]
    for h in range(RET_HEADS):
        q = rq_ref[:, h * RET_DK:(h + 1) * RET_DK]
        k = rk_ref[:, h * RET_DK:(h + 1) * RET_DK]
        qr = q * cosf + pltpu.roll(q, RET_DK // 2, axis=1) * sinf
        kr = (k * cosf + pltpu.roll(k, RET_DK // 2, axis=1) * sinf) * RET_DK ** -0.5
        vb = rv_ref[:, h * RET_DV:(h + 1) * RET_DV].astype(BF16)
        qb = qr.astype(BF16)
        state = s_ref[h]
        a = lax.dot_general(qb, kr.astype(BF16), NT_DIMS, preferred_element_type=F32) * dmat_ref[h]
        inner = jnp.dot(a.astype(BF16), vb, preferred_element_type=F32)
        cross = jnp.dot(qb, state.astype(BF16), preferred_element_type=F32) * cdec_ref[h]
        o = inner + cross
        kd_t = jnp.transpose(kr * kdec_ref[h]).astype(BF16)
        s_ref[h] = gc_ref[h] * state + jnp.dot(kd_t, vb, preferred_element_type=F32)
        ms = jnp.mean(o * o, axis=-1, keepdims=True)
        y = o * lax.rsqrt(ms + EPS) * retg_ref[:, h * RET_DV:(h + 1) * RET_DV]
        g = rg_ref[:, h * RET_DV:(h + 1) * RET_DV]
        a_ref[:, h * RET_DV:(h + 1) * RET_DV] = (y * (g * jax.nn.sigmoid(g))).astype(a_ref.dtype)

    u = jax.nn.gelu(gu_ref[...])
    v = jax.nn.gelu(gv_ref[...])
    ms = jnp.mean(v * v, axis=-1, keepdims=True)
    vn = v * lax.rsqrt(ms + EPS) * gmg_ref[...]
    vb = vn.astype(BF16)
    for g in range(GM_GROUPS):
        sl = slice(g * GM_GW, (g + 1) * GM_GW)
        mixed = jnp.dot(wtril_ref[g], vb[:, sl], preferred_element_type=F32) + bsm_ref[g]
        c_ref[:, sl] = (u[:, sl] * mixed).astype(c_ref.dtype)

    @pl.when(c == last)
    def _():
        sfin_ref[...] = s_ref[...]
        gmv_ref[...] = vn


def prompt_mixers(p_main, consts, retg_row, gmg_row, wtril, bsm):
    seq = p_main.shape[0]
    nc = seq // CHUNK
    cosf, sinf, dmat, cdec, kdec, gc = consts
    w512 = lambda blk: pl.BlockSpec((CHUNK, 512), lambda c: (c, blk))
    w1024 = lambda blk: pl.BlockSpec((CHUNK, 1024), lambda c: (c, blk))
    full = lambda shape: pl.BlockSpec(shape, lambda c: (0,) * len(shape))
    return pl.pallas_call(
        _mix_kernel,
        grid=(nc,),
        in_specs=[w512(0), w512(1), w1024(1), w1024(2), w512(_M_GU // 512), w512(_M_GV // 512),
                  pl.BlockSpec((CHUNK, RET_DK), lambda c: (c, 0)), pl.BlockSpec((CHUNK, RET_DK), lambda c: (c, 0)),
                  full(dmat.shape), full(cdec.shape), full(kdec.shape), full(gc.shape),
                  full(retg_row.shape), full(gmg_row.shape), full(wtril.shape), full(bsm.shape)],
        out_specs=[pl.BlockSpec((CHUNK, RET_V), lambda c: (c, 0)), pl.BlockSpec((CHUNK, GM_W), lambda c: (c, 0)),
                   full((RET_HEADS, RET_DK, RET_DV)), full((CHUNK, GM_W))],
        out_shape=[jax.ShapeDtypeStruct((seq, RET_V), BF16), jax.ShapeDtypeStruct((seq, GM_W), BF16),
                   jax.ShapeDtypeStruct((RET_HEADS, RET_DK, RET_DV), F32),
                   jax.ShapeDtypeStruct((CHUNK, GM_W), F32)],
        scratch_shapes=[pltpu.VMEM((RET_HEADS, RET_DK, RET_DV), F32)],
        compiler_params=_cparams(("arbitrary",)),
    )(p_main, p_main, p_main, p_main, p_main, p_main, cosf, sinf, dmat, cdec, kdec, gc, retg_row, gmg_row, wtril, bsm)


def _dsa_prompt_kernel(kix_ref, qih_ref, qil_ref, w_ref, k_ref, vt_ref, qt_ref, o_ref, keys_ref, wq_ref, acc_ref, *,
                       topk):
    i = pl.program_id(0)
    grp = DSA_GROUP
    per_grp = grp // CHUNK
    ng = i // per_grp + 1

    def grp_rows(g):
        return pl.ds(pl.multiple_of(g * grp, grp), grp)

    for h in range(IDX_HEADS):
        cs = slice(h * CHUNK, (h + 1) * CHUNK)
        for part, src in enumerate((qih_ref, qil_ref, qih_ref, qil_ref)):
            wq_ref[part * IDX_DIM:(part + 1) * IDX_DIM, cs] = src[h]
    w_row = jnp.concatenate([w_ref[h:h + 1, :] for h in range(IDX_HEADS)], axis=1)

    def group_keys(g):
        parts = []
        for sub in range(grp // DSA_SCORE_ROWS):
            r0 = pl.multiple_of(g * grp + sub * DSA_SCORE_ROWS, DSA_SCORE_ROWS)
            kt = kix_ref[pl.ds(r0, DSA_SCORE_ROWS), :]
            acc = jnp.zeros((DSA_SCORE_ROWS, CHUNK), F32)
            for hp in range(IDX_HEADS // 2):
                cs = slice(hp * 2 * CHUNK, (hp + 1) * 2 * CHUNK)
                r = jnp.dot(kt, wq_ref[:, cs], preferred_element_type=F32)
                r = jnp.maximum(r, 0.0) * w_row[:, cs]
                acc = acc + r[:, :CHUNK] + r[:, CHUNK:]
            parts.append(acc)
        return jnp.concatenate(parts, axis=0)

    def full_group(g, carry):
        keys_ref[grp_rows(g), :] = group_keys(g)
        return carry

    lax.fori_loop(0, ng - 1, full_group, 0)
    key_pos = (ng - 1) * grp + lax.broadcasted_iota(I32, (grp, CHUNK), 0)
    q_pos = i * CHUNK + lax.broadcasted_iota(I32, (grp, CHUNK), 1)
    keys_ref[grp_rows(ng - 1), :] = jnp.where(key_pos <= q_pos, group_keys(ng - 1), -jnp.inf)

    def count_ge(mid):
        def body(g, cnt):
            m = (keys_ref[grp_rows(g), :] >= mid).astype(I32)
            for t in range(per_grp):
                cnt = cnt + m[t * CHUNK:(t + 1) * CHUNK]
            return cnt

        cnt = lax.fori_loop(0, ng, body, jnp.zeros((CHUNK, CHUNK), I32))
        return jnp.sum(cnt, axis=0, keepdims=True)

    def class_max(g, cm):
        kg = keys_ref[grp_rows(g), :]
        for t in range(grp // topk):
            cm = jnp.maximum(cm, kg[t * topk:(t + 1) * topk])
        return cm

    cm = lax.fori_loop(0, ng, class_max, jnp.full((topk, CHUNK), -jnp.inf, F32))
    lo0 = jnp.maximum(jnp.min(cm, axis=0, keepdims=True), F32_LOWEST)
    hi0 = _bracket_top(jnp.max(cm, axis=0, keepdims=True))
    t_k, t_up = _bisect_threshold(count_ge, lo0, hi0, topk)

    over = count_ge(t_k) > topk

    @pl.when(jnp.max(over.astype(I32)) > 0)
    def _():
        need = topk - count_ge(t_up)

        def pos_of(g):
            return g * grp + lax.broadcasted_iota(I32, (grp, CHUNK), 0)

        def count_tied_upto(j):
            def body(g, cnt):
                m = jnp.where(keys_ref[grp_rows(g), :] == t_k, jnp.where(pos_of(g) <= j, 1, 0), 0)
                for t in range(per_grp):
                    cnt = cnt + m[t * CHUNK:(t + 1) * CHUNK]
                return cnt

            cnt = lax.fori_loop(0, ng, body, jnp.zeros((CHUNK, CHUNK), I32))
            return jnp.sum(cnt, axis=0, keepdims=True)

        def step(_, lh):
            lo_j, hi_j = lh
            mid = (lo_j + hi_j) >> 1
            ok = count_tied_upto(mid) >= need
            return jnp.where(ok, lo_j, mid), jnp.where(ok, mid, hi_j)

        n_keys = keys_ref.shape[0]
        _, j_keep = lax.fori_loop(0, n_keys.bit_length(), step,
                                  (jnp.full((1, CHUNK), -1, I32), jnp.full((1, CHUNK), n_keys - 1, I32)))

        def drop_rest(g, carry):
            k = keys_ref[grp_rows(g), :]
            dropped = jnp.where(pos_of(g) > j_keep, jnp.where(over, -jnp.inf, k), k)
            keys_ref[grp_rows(g), :] = jnp.where(k == t_k, dropped, k)
            return carry

        lax.fori_loop(0, ng, drop_rest, 0)

    thr = t_k

    acc_ref[...] = jnp.zeros_like(acc_ref)

    def attend(g, carry):
        sel = keys_ref[grp_rows(g), :] >= thr
        out = []
        for h in range(ATT_HEADS):
            hs = slice(h * ATT_HD, (h + 1) * ATT_HD)
            m, l = carry[h]
            s = jnp.dot(k_ref[grp_rows(g), hs], qt_ref[hs, :], preferred_element_type=F32)
            s = jnp.where(sel, s, -jnp.inf)
            m_new = jnp.maximum(m, jnp.max(s, axis=0, keepdims=True))
            m_safe = jnp.where(m_new == -jnp.inf, 0.0, m_new)
            p = jnp.exp(s - m_safe)
            alpha = jnp.exp(m - m_safe)
            l = alpha * l + jnp.sum(p, axis=0, keepdims=True)
            acc_ref[h] = acc_ref[h] * alpha + jnp.dot(vt_ref[g, hs, :], p.astype(BF16), preferred_element_type=F32)
            out.append((m_new, l))
        return tuple(out)

    init = tuple((jnp.full((1, CHUNK), -jnp.inf, F32), jnp.zeros((1, CHUNK), F32)) for _ in range(ATT_HEADS))
    stats = lax.fori_loop(0, ng, attend, init)
    for h in range(ATT_HEADS):
        o_ref[h * ATT_HD:(h + 1) * ATT_HD, :] = acc_ref[h] / stats[h][1]


def dsa_prompt(aq, ak, av, iq, iw, ik):
    seq = aq.shape[0]
    nb = seq // CHUNK
    topk = min(TOPK_MAX, seq // 4)
    assert seq % DSA_GROUP == 0 and DSA_GROUP % topk == 0
    q_hi = iq.astype(BF16)
    q_lo = (iq - q_hi.astype(F32)).astype(BF16)
    qit_hi = q_hi.T.reshape(IDX_HEADS, IDX_DIM, seq)
    qit_lo = q_lo.T.reshape(IDX_HEADS, IDX_DIM, seq)
    k_hi = ik.astype(BF16)
    k_lo = (ik - k_hi.astype(F32)).astype(BF16)
    kix = jnp.concatenate([k_hi, k_hi, k_lo, k_lo], axis=-1)
    w_t = (iw * (IDX_DIM ** -0.5 * IDX_HEADS ** -0.5)).T
    qt = (aq * ATT_HD ** -0.5).T.astype(BF16)
    kb = ak.astype(BF16)
    vt = av.reshape(seq // DSA_GROUP, DSA_GROUP, ATT_W).transpose(0, 2, 1).astype(BF16)
    resident = lambda shape: pl.BlockSpec(shape, lambda i: (0,) * len(shape))
    o_t = pl.pallas_call(
        functools.partial(_dsa_prompt_kernel, topk=topk),
        grid=(nb,),
        in_specs=[resident(kix.shape),
                  pl.BlockSpec((IDX_HEADS, IDX_DIM, CHUNK), lambda i: (0, 0, i)),
                  pl.BlockSpec((IDX_HEADS, IDX_DIM, CHUNK), lambda i: (0, 0, i)),
                  pl.BlockSpec((IDX_HEADS, CHUNK), lambda i: (0, i)),
                  resident(kb.shape), resident(vt.shape),
                  pl.BlockSpec((ATT_W, CHUNK), lambda i: (0, i))],
        out_specs=pl.BlockSpec((ATT_W, CHUNK), lambda i: (0, i)),
        out_shape=jax.ShapeDtypeStruct((ATT_W, seq), F32),
        scratch_shapes=[pltpu.VMEM((seq, CHUNK), F32), pltpu.VMEM((4 * IDX_DIM, IDX_HEADS * CHUNK), BF16),
                        pltpu.VMEM((ATT_HEADS, ATT_HD, CHUNK), F32)],
        compiler_params=_cparams(("arbitrary",), 56 * 1024 * 1024),
    )(kix, qit_hi, qit_lo, w_t, kb, vt, qt)
    return o_t.T


def _pmem_kernel(x_ref, g_ref, wq_ref, mk_ref, mv_ref, wo_ref, o_ref):
    q = jnp.dot(_rmsnorm_bf16(x_ref[...], g_ref[...]), wq_ref[...], preferred_element_type=F32)
    outs = []
    for h in range(MEM_HEADS):
        hs = slice(h * MEM_HD, (h + 1) * MEM_HD)
        s = lax.dot_general(q[:, hs].astype(BF16), mk_ref[:, hs], NT_DIMS, preferred_element_type=F32) * MEM_HD ** -0.5
        e = jnp.exp(s - jnp.max(s, axis=-1, keepdims=True))
        oh = jnp.dot(e.astype(BF16), mv_ref[:, hs], preferred_element_type=F32) / jnp.sum(e, axis=-1, keepdims=True)
        outs.append(oh.astype(BF16))
    o = jnp.concatenate(outs, axis=1)
    o_ref[...] = x_ref[...] + jnp.dot(o, wo_ref[...], preferred_element_type=F32)


def prompt_mem_attend(x, g, wq, mk, mv, wo, tm=512):
    m, d = x.shape
    n_mem = mk.shape[0]
    full = lambda shape: pl.BlockSpec(shape, lambda i: (0,) * len(shape))
    return pl.pallas_call(
        _pmem_kernel,
        grid=(m // tm,),
        in_specs=[pl.BlockSpec((tm, d), lambda i: (i, 0)), full((1, d)), full((d, MEM_W)), full((n_mem, MEM_W)),
                  full((n_mem, MEM_W)), full((MEM_W, d))],
        out_specs=pl.BlockSpec((tm, d), lambda i: (i, 0)),
        out_shape=jax.ShapeDtypeStruct((m, d), F32),
        compiler_params=_cparams(("parallel",), 48 * 1024 * 1024),
    )(x, g.reshape(1, d), wq, mk, mv, wo)


def _conv_seq(a, prev, cw, cb, row):
    p1 = prev[7:8, :]
    p2 = prev[6:7, :]
    r1 = jnp.where(row == 0, p1, pltpu.roll(a, 1, axis=0))
    r2 = jnp.where(row == 0, p2, jnp.where(row == 1, p1, pltpu.roll(a, 2, axis=0)))
    return cb + cw[0:1, :] * r2 + cw[1:2, :] * r1 + cw[2:3, :] * a


def _ffn_seq_kernel(x_ref, g_ref, wg_ref, wu_ref, cwg_ref, cwu_ref, cbg_ref, cbu_ref, wd_ref,
                    o_ref, lg_ref, lu_ref, carry_ref, h_ref):
    i = pl.program_id(0)
    j = pl.program_id(1)
    tm = x_ref.shape[0]
    row = lax.broadcasted_iota(I32, (tm, 1), 0)

    @pl.when(j == 0)
    def _():
        h_ref[...] = _rmsnorm_bf16(x_ref[...], g_ref[...])
        o_ref[...] = x_ref[...]

    @pl.when(i == 0)
    def _():
        carry_ref[j] = jnp.zeros(carry_ref.shape[1:], F32)

    hx = h_ref[...]
    f = None
    for s in range(FFN_TF // FFN_SUB):
        cs = slice(s * FFN_SUB, (s + 1) * FFN_SUB)
        ag = jnp.dot(hx, wg_ref[:, cs], preferred_element_type=F32)
        au = jnp.dot(hx, wu_ref[:, cs], preferred_element_type=F32)
        lg_ref[0, :, cs] = ag[tm - 8:tm, :]
        lu_ref[0, :, cs] = au[tm - 8:tm, :]
        cg = _conv_seq(ag, carry_ref[j, 0, :, cs], cwg_ref[:, cs], cbg_ref[:, cs], row)
        cu = _conv_seq(au, carry_ref[j, 1, :, cs], cwu_ref[:, cs], cbu_ref[:, cs], row)
        carry_ref[j, 0, :, cs] = ag[tm - 8:tm, :]
        carry_ref[j, 1, :, cs] = au[tm - 8:tm, :]
        act = ((cg * jax.nn.sigmoid(cg)) * cu).astype(BF16)
        fs = jnp.dot(act, wd_ref[cs, :], preferred_element_type=F32)
        f = fs if f is None else f + fs
    o_ref[...] += f


def ffn_seq(x, g, ffn_w, tm=512):
    wg, wu, cwg, cwu, cbg, cbu, wd = ffn_w
    m, d = x.shape
    tf = FFN_TF
    nf = D_FF_PAD // tf
    col = lambda r: pl.BlockSpec((r, tf), lambda i, j: (0, j))
    tail = pl.BlockSpec((1, 8, tf), lambda i, j: (i, 0, j))
    return pl.pallas_call(
        _ffn_seq_kernel,
        grid=(m // tm, nf),
        in_specs=[pl.BlockSpec((tm, d), lambda i, j: (i, 0)), pl.BlockSpec((1, d), lambda i, j: (0, 0)),
                  col(d), col(d), col(CONV_W), col(CONV_W), col(1), col(1),
                  pl.BlockSpec((tf, d), lambda i, j: (j, 0))],
        out_specs=[pl.BlockSpec((tm, d), lambda i, j: (i, 0)), tail, tail],
        out_shape=[jax.ShapeDtypeStruct((m, d), F32), jax.ShapeDtypeStruct((m // tm, 8, D_FF_PAD), F32),
                   jax.ShapeDtypeStruct((m // tm, 8, D_FF_PAD), F32)],
        scratch_shapes=[pltpu.VMEM((nf, 2, 8, tf), F32), pltpu.VMEM((tm, d), BF16)],
        compiler_params=_cparams(("arbitrary", "arbitrary"), 48 * 1024 * 1024),
    )(x, g.reshape(1, d), wg, wu, cwg, cwu, cbg, cbu, wd)


def _ffn_step_kernel(x_ref, g_ref, wg_ref, wu_ref, cwg_ref, cwu_ref, cbg_ref, cbu_ref, wd_ref,
                     pg0_ref, pg1_ref, pu0_ref, pu1_ref, o_ref, ag_ref, au_ref, h_ref):
    j = pl.program_id(0)

    @pl.when(j == 0)
    def _():
        h_ref[...] = _rmsnorm_bf16(x_ref[...], g_ref[...])
        o_ref[...] = x_ref[...]

    hx = h_ref[...]
    ag = jnp.dot(hx, wg_ref[...], preferred_element_type=F32)
    au = jnp.dot(hx, wu_ref[...], preferred_element_type=F32)
    ag_ref[...] = ag
    au_ref[...] = au
    cg = cbg_ref[...] + cwg_ref[0:1, :] * pg0_ref[...] + cwg_ref[1:2, :] * pg1_ref[...] + cwg_ref[2:3, :] * ag
    cu = cbu_ref[...] + cwu_ref[0:1, :] * pu0_ref[...] + cwu_ref[1:2, :] * pu1_ref[...] + cwu_ref[2:3, :] * au
    act = ((cg * jax.nn.sigmoid(cg)) * cu).astype(BF16)
    o_ref[...] += jnp.dot(act, wd_ref[...], preferred_element_type=F32)


def ffn_step(x, g, ffn_w, prev):
    wg, wu, cwg, cwu, cbg, cbu, wd = ffn_w
    m, d = x.shape
    tf = FFN_TF
    nf = D_FF_PAD // tf
    col = lambda r: pl.BlockSpec((r, tf), lambda j: (0, j))
    row = pl.BlockSpec((m, d), lambda j: (0, 0))
    return pl.pallas_call(
        _ffn_step_kernel,
        grid=(nf,),
        in_specs=[row, pl.BlockSpec((1, d), lambda j: (0, 0)), col(d), col(d), col(CONV_W), col(CONV_W), col(1), col(1),
                  pl.BlockSpec((tf, d), lambda j: (j, 0)), col(m), col(m), col(m), col(m)],
        out_specs=[row, col(m), col(m)],
        out_shape=[jax.ShapeDtypeStruct((m, d), F32), jax.ShapeDtypeStruct((m, D_FF_PAD), F32),
                   jax.ShapeDtypeStruct((m, D_FF_PAD), F32)],
        scratch_shapes=[pltpu.VMEM((m, d), BF16)],
        compiler_params=_cparams(("arbitrary",), 48 * 1024 * 1024),
    )(x, g.reshape(1, d), wg, wu, cwg, cwu, cbg, cbu, wd, *prev)


def _smix_kernel(qc_ref, kc_ref, v_ref, rg_ref, gu_ref, gv_ref, st_ref, cos_ref, sin_ref, g1_ref, retg_ref,
                 gmg_ref, wrow_ref, brow_ref, a_ref, c_ref, ns_ref, gmv_ref):
    cosc = cos_ref[...]
    sinc = sin_ref[...]
    half = RET_DK // 2

    def rot(x):
        x1, x2 = x[:half], x[half:]
        return jnp.concatenate([x1 * cosc - x2 * sinc, x1 * sinc + x2 * cosc], axis=0)

    for h in range(RET_HEADS):
        vs = slice(h * RET_DV, (h + 1) * RET_DV)
        qr = rot(qc_ref[0, h])
        kr = rot(kc_ref[0, h]) * RET_DK ** -0.5
        v = v_ref[0, :, vs]
        state = st_ref[0, h]
        g1 = g1_ref[h]
        inner = jnp.sum(qr * kr, axis=0, keepdims=True) * v
        cross = jnp.sum(qr * state, axis=0, keepdims=True) * g1
        o = inner + cross
        ns_ref[0, h] = g1 * state + kr * v
        ms = jnp.mean(o * o, axis=-1, keepdims=True)
        y = o * lax.rsqrt(ms + EPS) * retg_ref[:, vs]
        g = rg_ref[0, :, vs]
        a_ref[0, :, vs] = y * (g * jax.nn.sigmoid(g))

    u = jax.nn.gelu(gu_ref[0])
    v = jax.nn.gelu(gv_ref[0])
    ms = jnp.mean(v * v, axis=-1, keepdims=True)
    vn = v * lax.rsqrt(ms + EPS) * gmg_ref[...]
    c_ref[0] = u * (wrow_ref[...] * vn + brow_ref[...])
    gmv_ref[0] = vn


def sample_mixers(p_main, state, consts, retg_row, gmg_row, wrow, brow):
    nb = p_main.shape[0]
    cosc, sinc, g1 = consts
    qc = p_main[:, 0:RET_QK].reshape(nb, RET_HEADS, RET_DK, 1)
    kc = p_main[:, RET_QK:2 * RET_QK].reshape(nb, RET_HEADS, RET_DK, 1)
    p3 = p_main.reshape(nb, 1, p_main.shape[1])
    per_b = lambda width, blk: pl.BlockSpec((1, 1, width), lambda b: (b, 0, blk))
    col4 = pl.BlockSpec((1, RET_HEADS, RET_DK, 1), lambda b: (b, 0, 0, 0))
    st4 = pl.BlockSpec((1, RET_HEADS, RET_DK, RET_DV), lambda b: (b, 0, 0, 0))
    full = lambda shape: pl.BlockSpec(shape, lambda b: (0,) * len(shape))
    return pl.pallas_call(
        _smix_kernel,
        grid=(nb,),
        in_specs=[col4, col4, per_b(1024, 1), per_b(1024, 2), per_b(512, _M_GU // 512), per_b(512, _M_GV // 512), st4,
                  full(cosc.shape), full(sinc.shape), full(g1.shape), full(retg_row.shape), full(gmg_row.shape),
                  full(wrow.shape), full(brow.shape)],
        out_specs=[per_b(RET_V, 0), per_b(GM_W, 0), st4, per_b(GM_W, 0)],
        out_shape=[jax.ShapeDtypeStruct((nb, 1, RET_V), F32), jax.ShapeDtypeStruct((nb, 1, GM_W), F32),
                   jax.ShapeDtypeStruct(state.shape, F32), jax.ShapeDtypeStruct((nb, 1, GM_W), F32)],
        compiler_params=_cparams(("parallel",)),
    )(qc, kc, p3, p3, p3, p3, state, cosc, sinc, g1, retg_row, gmg_row, wrow, brow)


def _page_scores_kernel(pt_ref, qx_ref, w_ref, q8_ref, *refs, n_pg):
    ki_refs = refs[:n_pg]
    k_refs = refs[n_pg:2 * n_pg]
    isc_ref, asc_ref = refs[2 * n_pg:]
    qx = qx_ref[0]
    q_hi = qx[:IDX_HEADS]
    w = w_ref[0]
    q8 = q8_ref[0]
    for n in range(n_pg):
        kp = ki_refs[n][0]
        k_hi = kp.astype(BF16)
        k_lo = (kp - k_hi.astype(F32)).astype(BF16)
        s2 = lax.dot_general(qx, k_hi, NT_DIMS, preferred_element_type=F32)
        s = s2[:IDX_HEADS] + s2[IDX_HEADS:] + lax.dot_general(q_hi, k_lo, NT_DIMS, preferred_element_type=F32)
        isc_ref[0, n] = jnp.sum(jnp.maximum(s, 0.0) * w, axis=0, keepdims=True)
        asc_ref[0, n] = lax.dot_general(q8, k_refs[n][0].astype(BF16), NT_DIMS, preferred_element_type=F32)


def page_scores(page_table, qx, w_col, q8, pool_ki, pool_k, n_pg):
    nb, n_pages = page_table.shape
    groups = n_pages // n_pg
    pt = page_table.reshape(-1)

    def pool_spec(shape, n):
        return pl.BlockSpec((1,) + shape, lambda b, g, pt_ref: (pt_ref[b * n_pages + g * n_pg + n], 0, 0))

    per_b = lambda shape: pl.BlockSpec((1,) + shape, lambda b, g, pt_ref: (b,) + (0,) * len(shape))
    grid_spec = pltpu.PrefetchScalarGridSpec(
        num_scalar_prefetch=1,
        grid=(nb, groups),
        in_specs=[per_b((2 * IDX_HEADS, IDX_DIM)), per_b((IDX_HEADS, 1)), per_b((8, ATT_HD))]
        + [pool_spec((PAGE_SIZE, IDX_DIM), n) for n in range(n_pg)]
        + [pool_spec((PAGE_ROWS, ATT_HD), n) for n in range(n_pg)],
        out_specs=[pl.BlockSpec((1, n_pg, 1, PAGE_SIZE), lambda b, g, pt_ref: (b, g, 0, 0)),
                   pl.BlockSpec((1, n_pg, 8, PAGE_ROWS), lambda b, g, pt_ref: (b, g, 0, 0))],
    )
    return pl.pallas_call(
        functools.partial(_page_scores_kernel, n_pg=n_pg),
        grid_spec=grid_spec,
        out_shape=[jax.ShapeDtypeStruct((nb, n_pages, 1, PAGE_SIZE), F32),
                   jax.ShapeDtypeStruct((nb, n_pages, 8, PAGE_ROWS), F32)],
        compiler_params=_cparams(("arbitrary", "arbitrary"), 56 * 1024 * 1024),
    )(pt, qx, w_col, q8, *([pool_ki] * n_pg), *([pool_k] * n_pg))


def _select_softmax_kernel(isc_ref, isce_ref, asc_ref, p_ref, keys_ref, thr_ref, keep_ref, *, topk, n_valid):
    h = pl.program_id(0)
    nb = isc_ref.shape[0]

    @pl.when(h == 0)
    def _():
        colid = lax.broadcasted_iota(I32, isc_ref.shape, 1)
        visible = colid < n_valid
        keys_ref[...] = jnp.where(visible, isc_ref[...], -jnp.inf)

        def count_ge(mid):
            return jnp.sum((keys_ref[...] >= mid).astype(I32), axis=1, keepdims=True)

        lo0 = jnp.min(jnp.where(visible, isc_ref[...], jnp.inf), axis=1, keepdims=True)
        hi0 = _bracket_top(jnp.max(keys_ref[...], axis=1, keepdims=True))
        t_k, t_up = _bisect_threshold(count_ge, lo0, hi0, topk)
        thr_ref[...] = jnp.broadcast_to(t_k, thr_ref.shape)
        keep_ref[...] = jnp.full(keep_ref.shape, INT_MAX, I32)

        over = count_ge(t_k) > topk

        @pl.when(jnp.max(over.astype(I32)) > 0)
        def _():
            need = topk - count_ge(t_up)

            def count_tied_upto(j):
                tied = jnp.where(keys_ref[...] == t_k, jnp.where(colid <= j, 1, 0), 0)
                return jnp.sum(tied, axis=1, keepdims=True)

            def step(_, lh):
                lo_j, hi_j = lh
                mid = (lo_j + hi_j) >> 1
                ok = count_tied_upto(mid) >= need
                return jnp.where(ok, lo_j, mid), jnp.where(ok, mid, hi_j)

            n_keys = isc_ref.shape[1]
            _, j_keep = lax.fori_loop(0, n_keys.bit_length(), step,
                                      (jnp.full((nb, 1), -1, I32), jnp.full((nb, 1), n_keys - 1, I32)))
            keep_ref[...] = jnp.broadcast_to(jnp.where(over, j_keep, INT_MAX), keep_ref.shape)

    ce = lax.broadcasted_iota(I32, isce_ref.shape, 1)
    ke = isce_ref[...]
    thr = thr_ref[:, 0:1]
    key_pos = ce >> (ATT_HEADS.bit_length() - 1)
    tied_ok = jnp.where(key_pos <= keep_ref[:, 0:1], asc_ref[0], -jnp.inf)
    s = jnp.where(ke > thr, asc_ref[0], jnp.where(ke == thr, tied_ok, -jnp.inf))
    s = jnp.where(ce < ATT_HEADS * n_valid, s, -jnp.inf)
    s = jnp.where((ce & (ATT_HEADS - 1)) == h, s, -jnp.inf)
    e = jnp.exp(s - jnp.max(s, axis=1, keepdims=True))
    p_ref[0] = e / jnp.sum(e, axis=1, keepdims=True)


def select_softmax(isc, isc_e, asc, topk, n_valid):
    nb = isc.shape[0]
    head_blk = pl.BlockSpec((1,) + asc.shape[1:], lambda h: (h, 0, 0))
    whole = lambda a: pl.BlockSpec(a.shape, lambda h: (0, 0))
    return pl.pallas_call(
        functools.partial(_select_softmax_kernel, topk=topk, n_valid=n_valid),
        grid=(ATT_HEADS,),
        in_specs=[whole(isc), whole(isc_e), head_blk],
        out_specs=head_blk,
        out_shape=jax.ShapeDtypeStruct(asc.shape, F32),
        scratch_shapes=[pltpu.VMEM(isc.shape, F32), pltpu.VMEM((nb, LANE), F32), pltpu.VMEM((nb, LANE), I32)],
        compiler_params=_cparams(("arbitrary",), 48 * 1024 * 1024),
    )(isc, isc_e, asc)


def _page_values_kernel(pt_ref, p_ref, *refs, n_pg):
    v_refs = refs[:n_pg]
    o_ref = refs[n_pg]
    acc = jnp.zeros(o_ref.shape[1:], F32)
    for n in range(n_pg):
        acc = acc + jnp.dot(p_ref[0, n], v_refs[n][0].astype(BF16), preferred_element_type=F32)

    @pl.when(pl.program_id(1) == 0)
    def _():
        o_ref[0] = acc

    @pl.when(pl.program_id(1) > 0)
    def _():
        o_ref[0] += acc


def page_values(page_table, p_pages, pool_v, n_pg):
    nb, n_pages = page_table.shape
    groups = n_pages // n_pg
    pt = page_table.reshape(-1)

    def pool_spec(n):
        return pl.BlockSpec((1, PAGE_ROWS, ATT_HD), lambda b, g, pt_ref: (pt_ref[b * n_pages + g * n_pg + n], 0, 0))

    grid_spec = pltpu.PrefetchScalarGridSpec(
        num_scalar_prefetch=1,
        grid=(nb, groups),
        in_specs=[pl.BlockSpec((1, n_pg, 8, PAGE_ROWS), lambda b, g, pt_ref: (b, g, 0, 0))]
        + [pool_spec(n) for n in range(n_pg)],
        out_specs=pl.BlockSpec((1, 8, ATT_HD), lambda b, g, pt_ref: (b, 0, 0)),
    )
    return pl.pallas_call(
        functools.partial(_page_values_kernel, n_pg=n_pg),
        grid_spec=grid_spec,
        out_shape=jax.ShapeDtypeStruct((nb, 8, ATT_HD), F32),
        compiler_params=_cparams(("arbitrary", "arbitrary"), 48 * 1024 * 1024),
    )(pt, p_pages, *([pool_v] * n_pg))


def _head_rows(q, scale):
    nb = q.shape[0]
    q3 = (q * scale).reshape(nb, ATT_HEADS, ATT_HD)
    return jnp.pad(q3, ((0, 0), (0, 8 - ATT_HEADS), (0, 0))).astype(BF16)


def dsa_sample(aq, ak, av, iq, iw, ik, pool_k, pool_v, pool_ki, page_table, n_pg=64):
    nb, n_pages = page_table.shape
    past = n_pages * PAGE_SIZE
    n_keys = past + PAGE_SIZE
    topk = min(TOPK_MAX, (past + 1) // 4)
    iq3 = iq.reshape(nb, IDX_HEADS, IDX_DIM)
    q_hi = iq3.astype(BF16)
    q_lo = (iq3 - q_hi.astype(F32)).astype(BF16)
    qx = jnp.concatenate([q_hi, q_lo], axis=1)
    w_col = (iw * (IDX_DIM ** -0.5 * IDX_HEADS ** -0.5)).reshape(nb, IDX_HEADS, 1)
    q8 = _head_rows(aq, ATT_HD ** -0.5)
    own = jnp.arange(nb, dtype=I32).reshape(nb, 1)
    pad_rows = lambda a, rows: jnp.pad(a, ((0, 0), (0, rows - a.shape[1]), (0, 0)))
    own_ki = pad_rows(ik[:, None, :], PAGE_SIZE)
    own_k = pad_rows(ak.reshape(nb, ATT_HEADS, ATT_HD), PAGE_ROWS)
    own_v = pad_rows(av.reshape(nb, ATT_HEADS, ATT_HD), PAGE_ROWS)
    isc_p, asc_p = page_scores(page_table, qx, w_col, q8, pool_ki, pool_k, n_pg)
    isc_n, asc_n = page_scores(own, qx, w_col, q8, own_ki, own_k, 1)
    isc = jnp.concatenate([isc_p, isc_n], axis=1).reshape(nb, n_keys)
    asc = jnp.concatenate([asc_p, asc_n], axis=1)[:, :, :ATT_HEADS]
    asc = asc.transpose(2, 0, 1, 3).reshape(ATT_HEADS, nb, n_keys * ATT_HEADS)
    p = select_softmax(isc, jnp.repeat(isc, ATT_HEADS, axis=1), asc, topk, past + 1)
    p = p.reshape(ATT_HEADS, nb, n_pages + 1, PAGE_ROWS).transpose(1, 2, 0, 3)
    p = jnp.pad(p, ((0, 0), (0, 0), (0, 8 - ATT_HEADS), (0, 0))).astype(BF16)
    o_p = page_values(page_table, p[:, :n_pages], pool_v, n_pg)
    o_n = page_values(own, p[:, n_pages:], own_v, 1)
    return (o_p + o_n)[:, :ATT_HEADS].reshape(nb, ATT_W)


def _smem_kernel(q8_ref, mk_ref, mv_ref, o_ref):
    rows = mk_ref.shape[1]
    head_of = lambda axis: lax.broadcasted_iota(I32, (8, rows), axis) & (MEM_HEADS - 1)
    own_head = head_of(1) == head_of(0)
    for b in range(q8_ref.shape[0]):
        s = lax.dot_general(q8_ref[b], mk_ref[b].astype(BF16), NT_DIMS, preferred_element_type=F32)
        s = jnp.where(own_head, s, -jnp.inf)
        e = jnp.exp(s - jnp.max(s, axis=-1, keepdims=True))
        p = e / jnp.sum(e, axis=-1, keepdims=True)
        o_ref[b] = jnp.dot(p.astype(BF16), mv_ref[b].astype(BF16), preferred_element_type=F32)


def sample_mem_core(q, mem_k, mem_v, layer, bb=8):
    nb = q.shape[0]
    rows = mem_k.shape[1]
    q8 = _head_rows(q, MEM_HD ** -0.5)
    first = layer * (nb // bb)
    o = pl.pallas_call(
        _smem_kernel,
        grid=(nb // bb,),
        in_specs=[pl.BlockSpec((bb, 8, MEM_HD), lambda i: (i, 0, 0)),
                  pl.BlockSpec((bb, rows, MEM_HD), lambda i: (first + i, 0, 0)),
                  pl.BlockSpec((bb, rows, MEM_HD), lambda i: (first + i, 0, 0))],
        out_specs=pl.BlockSpec((bb, 8, MEM_HD), lambda i: (i, 0, 0)),
        out_shape=jax.ShapeDtypeStruct((nb, 8, MEM_HD), F32),
        compiler_params=_cparams(("parallel",), 48 * 1024 * 1024),
    )(q8, mem_k, mem_v)
    return o[:, :MEM_HEADS].reshape(nb, MEM_W)


def _prep_layer(w_in, w_out, w_mq, w_mk, w_mv, w_mo, w_up, conv_w, conv_b, w_down, gm_ws, gm_bs):
    w_proj = jnp.concatenate([w_in[:, :_O_IQ_END], w_in[:, _O_GU:], w_in[:, _O_IK:_O_GU], w_in[:, _O_IW:_O_IK],
                              jnp.zeros((D_MODEL, N_PROJ - N_MAIN - IDX_DIM - IDX_HEADS), w_in.dtype)],
                             axis=1).astype(BF16)
    padc = lambda a: jnp.pad(a, ((0, 0), (0, D_FF_PAD - D_FF)))
    ffn_w = (padc(w_up[:, :D_FF]).astype(BF16), padc(w_up[:, D_FF:]).astype(BF16),
             padc(conv_w[:, :D_FF]), padc(conv_w[:, D_FF:]),
             padc(conv_b[None, :D_FF]), padc(conv_b[None, D_FF:]),
             jnp.pad(w_down, ((0, D_FF_PAD - D_FF), (0, 0))).astype(BF16))
    causal = jnp.tril(jnp.ones((CHUNK, CHUNK), dtype=bool))
    wtril = jnp.where(causal[None], gm_ws, 0.0).astype(BF16)
    bsm = jnp.broadcast_to(gm_bs[:, :, None], (GM_GROUPS, CHUNK, GM_GW))
    wrow = jnp.repeat(gm_ws[:, 0, 0], GM_GW).reshape(1, GM_W)
    brow = jnp.repeat(gm_bs[:, 0], GM_GW).reshape(1, GM_W)
    return dict(w_proj=w_proj, w_out=w_out.astype(BF16), w_mq=w_mq.astype(BF16),
                w_mkv=jnp.concatenate([w_mk, w_mv], axis=1).astype(BF16), w_mo=w_mo.astype(BF16), ffn_w=ffn_w,
                wtril=wtril, bsm=bsm, wrow=wrow, brow=brow)


def _retention_consts(seq, past):
    log_g = jnp.log(1.0 - 2.0 ** (-5.0 - jnp.arange(RET_HEADS, dtype=F32)))
    half = RET_DK // 2
    inv = ROPE_BASE ** (-jnp.arange(half, dtype=F32) / half)
    ang = jnp.arange(seq).astype(F32)[:, None] * inv[None, :]
    cos, sin = jnp.cos(ang), jnp.sin(ang)
    cosf = jnp.concatenate([cos, cos], axis=1)
    sinf = jnp.concatenate([-sin, sin], axis=1)
    n = jnp.arange(CHUNK, dtype=F32)
    diff = n[:, None] - n[None, :]
    causal = diff >= 0
    dmat = jnp.where(causal[None], jnp.exp(log_g[:, None, None] * jnp.where(causal, diff, 0.0)[None]), 0.0)
    cdec = jnp.broadcast_to(jnp.exp(log_g[:, None] * (n[None, :] + 1.0))[:, :, None], (RET_HEADS, CHUNK, RET_DV))
    kdec = jnp.broadcast_to(jnp.exp(log_g[:, None] * (CHUNK - 1.0 - n[None, :]))[:, :, None], (RET_HEADS, CHUNK, RET_DK))
    gc = jnp.broadcast_to(jnp.exp(log_g * CHUNK)[:, None, None], (RET_HEADS, RET_DK, RET_DV))
    ang_s = jnp.full((1,), past, dtype=F32)[:, None] * inv[None, :]
    cosc = jnp.cos(ang_s).reshape(half, 1)
    sinc = jnp.sin(ang_s).reshape(half, 1)
    g1 = jnp.broadcast_to(jnp.exp(log_g * 1.0)[:, None, None], (RET_HEADS, 1, RET_DV))
    return (cosf, sinf, dmat, cdec, kdec, gc), (cosc, sinc, g1)


def _pad_ff(a):
    return jnp.pad(a, ((0, 0), (0, D_FF_PAD - D_FF)))


def kernel(x_prompt, x_sample, mem_prompt, cache_k, cache_v, cache_kidx, page_table, cache_mem_k, cache_mem_v, state_ret, state_conv, norm_mix_g, w_in, ret_norm_g, gm_norm_g, gm_ws, gm_bs, w_out, norm_mem_g, mem_in_g, w_mq, w_mk, w_mv, w_mo, norm_ffn_g, w_up, conv_w, conv_b, w_down, final_norm_g):
    depth = w_in.shape[0]
    seq = x_prompt.shape[1]
    nbd = x_sample.shape[0]
    n_mem = mem_prompt.shape[1]
    past = page_table.shape[1] * PAGE_SIZE
    p_consts, s_consts = _retention_consts(seq, past)
    n_pool = cache_k.shape[1]
    pool_k = cache_k.reshape(depth * n_pool, PAGE_ROWS, ATT_HD)
    pool_v = cache_v.reshape(depth * n_pool, PAGE_ROWS, ATT_HD)
    pool_ki = cache_kidx.reshape(depth * n_pool, PAGE_SIZE, IDX_DIM)
    mem_k_all = cache_mem_k.reshape(depth * nbd, n_mem * MEM_HEADS, MEM_HD)
    mem_v_all = cache_mem_v.reshape(depth * nbd, n_mem * MEM_HEADS, MEM_HD)

    xp = x_prompt.reshape(seq, D_MODEL)
    xs = x_sample.reshape(nbd, D_MODEL)
    mem = mem_prompt.reshape(n_mem, D_MODEL)
    outs = {k: [] for k in ("kp", "vp", "kip", "ks", "vs", "kis", "mkp", "mvp", "rsp", "rss", "csp", "css", "gvp", "gvs")}

    for l in range(depth):
        w = _prep_layer(w_in[l], w_out[l], w_mq[l], w_mk[l], w_mv[l], w_mo[l], w_up[l], conv_w[l], conv_b[l],
                        w_down[l], gm_ws[l], gm_bs[l])
        retg_row = ret_norm_g[l].reshape(1, RET_V)
        gmg_row = gm_norm_g[l].reshape(1, GM_W)

        pm = norm_matmul(xp, norm_mix_g[l], w["w_proj"], tn=1024)
        ak, av = pm[:, _M_AK:_M_AV], pm[:, _M_AV:_M_IQ]
        ik = pm[:, _M_IK:_M_IW]
        a_out, c_out, r_state, gm_v = prompt_mixers(pm, p_consts, retg_row, gmg_row, w["wtril"], w["bsm"])
        b_out = dsa_prompt(pm[:, _M_AQ:_M_AK], ak, av, pm[:, _M_IQ:_M_GU], pm[:, _M_IW:_M_IW + IDX_HEADS], ik)
        xp = out_projection(a_out, b_out, c_out, w["w_out"], xp)
        mkv = norm_matmul(mem, mem_in_g[l], w["w_mkv"])
        mk, mv = mkv[:, :MEM_W], mkv[:, MEM_W:]
        xp = prompt_mem_attend(xp, norm_mem_g[l], w["w_mq"], mk.astype(BF16), mv.astype(BF16), w["w_mo"])
        xp, tail_g, tail_u = ffn_seq(xp, norm_ffn_g[l], w["ffn_w"])
        last_g, last_u = tail_g[-1], tail_u[-1]
        outs["kp"].append(ak.reshape(1, seq, ATT_HEADS, ATT_HD))
        outs["vp"].append(av.reshape(1, seq, ATT_HEADS, ATT_HD))
        outs["kip"].append(ik.reshape(1, seq, IDX_DIM))
        outs["mkp"].append(mk.reshape(1, n_mem, MEM_HEADS, MEM_HD))
        outs["mvp"].append(mv.reshape(1, n_mem, MEM_HEADS, MEM_HD))
        outs["rsp"].append(r_state.reshape(1, RET_HEADS, RET_DK, RET_DV))
        outs["csp"].append(jnp.concatenate([last_g[8 - (CONV_W - 1):, :D_FF], last_u[8 - (CONV_W - 1):, :D_FF]], axis=1)[None])
        outs["gvp"].append(gm_v[None])

        pm = norm_matmul(xs, norm_mix_g[l], w["w_proj"])
        ak, av = pm[:, _M_AK:_M_AV], pm[:, _M_AV:_M_IQ]
        ik = pm[:, _M_IK:_M_IW]
        a_out, c_out, r_state, gm_v = sample_mixers(pm, state_ret[l], s_consts, retg_row, gmg_row, w["wrow"], w["brow"])
        b_out = dsa_sample(pm[:, _M_AQ:_M_AK], ak, av, pm[:, _M_IQ:_M_GU], pm[:, _M_IW:_M_IW + IDX_HEADS], ik,
                           pool_k, pool_v, pool_ki, page_table + l * n_pool)
        xs = out_projection(a_out.reshape(nbd, RET_V), b_out, c_out.reshape(nbd, GM_W), w["w_out"], xs)
        q = norm_matmul(xs, norm_mem_g[l], w["w_mq"])
        o = sample_mem_core(q, mem_k_all, mem_v_all, l)
        xs = matmul(o.astype(BF16), w["w_mo"], res=xs)
        sc = state_conv[l]
        prev = (_pad_ff(sc[:, 0, :D_FF]), _pad_ff(sc[:, 1, :D_FF]), _pad_ff(sc[:, 0, D_FF:]), _pad_ff(sc[:, 1, D_FF:]))
        xs, a_g, a_u = ffn_step(xs, norm_ffn_g[l], w["ffn_w"], prev)
        a_new = jnp.concatenate([a_g[:, :D_FF], a_u[:, :D_FF]], axis=1)
        outs["ks"].append(ak.reshape(nbd, 1, ATT_HEADS, ATT_HD))
        outs["vs"].append(av.reshape(nbd, 1, ATT_HEADS, ATT_HD))
        outs["kis"].append(ik.reshape(nbd, 1, IDX_DIM))
        outs["rss"].append(r_state)
        outs["css"].append(jnp.stack([sc[:, 1, :], a_new], axis=1))
        outs["gvs"].append(gm_v)

    y_prompt = rmsnorm_rows(xp, final_norm_g, F32).reshape(1, seq, D_MODEL)
    y_sample = rmsnorm_rows(xs, final_norm_g, F32).reshape(nbd, 1, D_MODEL)
    st = lambda k: jnp.stack(outs[k])
    return (y_prompt, y_sample, st("kp"), st("vp"), st("kip"), st("ks"), st("vs"), st("kis"), st("mkp"), st("mvp"),
            st("rsp"), st("rss"), st("csp"), st("css"), st("gvp"), st("gvs"))
```
